```python
import math
import jax
import jax.numpy as jnp
from jax import lax
import numpy as np

D_MODEL = 4096
BATCH = 1
SEQ = 16384
DEPTH = 2

HEAD_DIM = 128
A_WIDTH = D_MODEL // 2
A_HEADS = A_WIDTH // HEAD_DIM
B_WIDTH = D_MODEL - A_WIDTH
CONV_WIDTH = 3
DILATED_PATTERNS = ((128, 1), (512, 4), (2048, 16))

MLA_HEADS = D_MODEL // 128
Q_LORA = 1536
KV_LORA = 512
QK_NOPE = 128
QK_ROPE = 64
V_DIM = 128
MLA_IN = Q_LORA + KV_LORA + QK_ROPE

FFN_HIDDEN = -(-8 * D_MODEL // (3 * 256)) * 256

ROPE_THETA = 10000.0
Q_BLOCK = 128
LN_EPS = 1e-5
RMS_EPS = 1e-6
NEG = -1e30
ALPHA = (2.0 * DEPTH) ** 0.25
BETA = (8.0 * DEPTH) ** -0.25
N_EVEN = (DEPTH + 1) // 2
N_ODD = DEPTH // 2

kernel_name = "hybrid_dilated_conv_mla_deepnorm"


def layernorm(x, g, b):
    x32 = x.astype(jnp.float32)
    mu = jnp.mean(x32, axis=-1, keepdims=True)
    var = jnp.mean(jnp.square(x32 - mu), axis=-1, keepdims=True)
    return ((x32 - mu) * lax.rsqrt(var + LN_EPS) * g.astype(jnp.float32) + b.astype(jnp.float32)).astype(x.dtype)


def rmsnorm(x, g):
    x32 = x.astype(jnp.float32)
    r = lax.rsqrt(jnp.mean(jnp.square(x32), axis=-1, keepdims=True) + RMS_EPS)
    return (x32 * r * g.astype(jnp.float32)).astype(x.dtype)


def rope(x, pos):
    d = x.shape[-1]
    half = d // 2
    inv = ROPE_THETA ** (-jnp.arange(half, dtype=jnp.float32) * 2.0 / d)
    ang = pos.astype(jnp.float32)[:, None] * inv[None, :]
    cos = jnp.cos(ang)[:, None, :]
    sin = jnp.sin(ang)[:, None, :]
    x32 = x.astype(jnp.float32)
    x1, x2 = x32[..., :half], x32[..., half:]
    return jnp.concatenate([x1 * cos - x2 * sin, x2 * cos + x1 * sin], axis=-1).astype(x.dtype)


def dilated_branch(q, k, v, window, dilation):
    B, S, H, D = q.shape
    blk = window // dilation
    span = dilation * blk
    sp = -(-S // span) * span
    nb = sp // span

    def to_sub(t):
        t = jnp.pad(t, ((0, 0), (0, sp - S), (0, 0), (0, 0)))
        t = t.reshape(B, sp // dilation, dilation, H, D).swapaxes(1, 2)
        return t.reshape(B, dilation, nb, blk, H, D)

    def with_prev(t):
        prev = jnp.pad(t[:, :, :-1], ((0, 0), (0, 0), (1, 0), (0, 0), (0, 0), (0, 0)))
        return jnp.concatenate([prev, t], axis=3)

    qs = to_sub(q).astype(jnp.float32)
    kk = with_prev(to_sub(k)).astype(jnp.float32)
    vv = with_prev(to_sub(v)).astype(jnp.float32)
    s = jnp.einsum('brnqhd,brnkhd->brnhqk', qs, kk) * (D ** -0.5)
    qi = jnp.arange(blk)[:, None]
    ki = jnp.arange(2 * blk)[None, :]
    dist = qi + blk - ki
    band = (dist >= 0) & (dist <= blk)
    has_prev = (jnp.arange(nb) > 0)[:, None, None] | (ki >= blk)[None]
    valid = band[None] & has_prev
    s = jnp.where(valid[None, None, :, None], s, NEG)
    m = jnp.max(s, axis=-1, keepdims=True)
    p = jnp.exp(s - m)
    den = jnp.sum(p, axis=-1)
    o = jnp.einsum('brnhqk,brnkhd->brnqhd', p, vv) / den.swapaxes(-1, -2)[..., None]
    lse = (m[..., 0] + jnp.log(den)).swapaxes(-1, -2)

    def from_sub(t):
        t = t.reshape(B, dilation, sp // dilation, *t.shape[4:]).swapaxes(1, 2)
        return t.reshape(B, sp, *t.shape[3:])[:, :S]

    return from_sub(o), from_sub(lse)


def dilated_mixture(q, k, v):
    outs, lses = [], []
    for window, dilation in DILATED_PATTERNS:
        o, l = dilated_branch(q, k, v, window, dilation)
        outs.append(o)
        lses.append(l)
    wts = jax.nn.softmax(jnp.stack(lses), axis=0)
    return jnp.sum(wts[..., None] * jnp.stack(outs), axis=0)


def attn_conv_mixer(x, w_in, conv_w, w_out, pos):
    B, S, _ = x.shape
    h = x @ w_in
    idx = np.cumsum([A_WIDTH, A_WIDTH, A_WIDTH, B_WIDTH, B_WIDTH])
    q, k, v, gb, gc, hin = jnp.split(h, idx, axis=-1)
    q = rope(q.reshape(B, S, A_HEADS, HEAD_DIM), pos)
    k = rope(k.reshape(B, S, A_HEADS, HEAD_DIM), pos)
    v = v.reshape(B, S, A_HEADS, HEAD_DIM)
    a = dilated_mixture(q, k, v).reshape(B, S, A_WIDTH).astype(x.dtype)
    u = gc * hin
    y = lax.conv_general_dilated(u, conv_w[:, None, :], window_strides=(1,),
                                 padding=((CONV_WIDTH - 1, 0),),
                                 dimension_numbers=('NWC', 'WIO', 'NWC'),
                                 feature_group_count=B_WIDTH)
    b = gb * y
    return jnp.concatenate([a, b], axis=-1) @ w_out


def mla_mixer(x, w_in, q_norm, kv_norm, w_uq, w_ukv, w_out, pos):
    B, S, _ = x.shape
    h = x @ w_in
    cq, ckv, kr = jnp.split(h, [Q_LORA, Q_LORA + KV_LORA], axis=-1)
    q = (rmsnorm(cq, q_norm) @ w_uq).reshape(B, S, MLA_HEADS, QK_NOPE + QK_ROPE)
    qn = q[..., :QK_NOPE]
    qr = rope(q[..., QK_NOPE:], pos)
    kv = (rmsnorm(ckv, kv_norm) @ w_ukv).reshape(B, S, MLA_HEADS, QK_NOPE + V_DIM)
    kn = kv[..., :QK_NOPE].astype(jnp.float32)
    vh = kv[..., QK_NOPE:].astype(jnp.float32)
    kr = rope(kr[:, :, None, :], pos)[:, :, 0].astype(jnp.float32)
    scale = (QK_NOPE + QK_ROPE) ** -0.5
    nq = S // Q_BLOCK
    kpos = jnp.arange(S)

    def blockify(t):
        return t.reshape(B, nq, Q_BLOCK, *t.shape[2:]).swapaxes(0, 1)

    def one_block(args):
        qn_b, qr_b, i = args
        s = (jnp.einsum('bqhd,bkhd->bhqk', qn_b.astype(jnp.float32), kn)
             + jnp.einsum('bqhr,bkr->bhqk', qr_b.astype(jnp.float32), kr)) * scale
        qpos = i * Q_BLOCK + jnp.arange(Q_BLOCK)
        s = jnp.where(kpos[None, :] <= qpos[:, None], s, NEG)
        p = jax.nn.softmax(s, axis=-1)
        return jnp.einsum('bhqk,bkhd->bqhd', p, vh)

    o = lax.map(one_block, (blockify(qn), blockify(qr), jnp.arange(nq)))
    o = o.swapaxes(0, 1).reshape(B, S, MLA_HEADS * V_DIM).astype(x.dtype)
    return o @ w_out


def swiglu(x, w_gate, w_up, w_down):
    return (jax.nn.silu(x @ w_gate) * (x @ w_up)) @ w_down


def setup_inputs(seed: int = 0) -> dict:
    key = jax.random.key(seed)
    ks = jax.random.split(key, 20)
    f32 = jnp.float32

    def nrm(k, shape, fan_in, mult=1.0):
        return jax.random.normal(k, shape, f32) * (fan_in ** -0.5) * mult

    d = D_MODEL
    return {
        "x": jax.random.normal(ks[0], (BATCH, SEQ, d), f32),
        "w_in_a": nrm(ks[1], (N_EVEN, d, 3 * A_WIDTH + 3 * B_WIDTH), d),
        "conv_w": nrm(ks[2], (N_EVEN, CONV_WIDTH, B_WIDTH), CONV_WIDTH),
        "w_out_a": nrm(ks[3], (N_EVEN, A_WIDTH + B_WIDTH, d), A_WIDTH + B_WIDTH, BETA),
        "w_in_c": nrm(ks[4], (N_ODD, d, MLA_IN), d),
        "q_norm": 1.0 + 0.02 * jax.random.normal(ks[5], (N_ODD, Q_LORA), f32),
        "kv_norm": 1.0 + 0.02 * jax.random.normal(ks[6], (N_ODD, KV_LORA), f32),
        "w_uq": nrm(ks[7], (N_ODD, Q_LORA, MLA_HEADS * (QK_NOPE + QK_ROPE)), Q_LORA),
        "w_ukv": nrm(ks[8], (N_ODD, KV_LORA, MLA_HEADS * (QK_NOPE + V_DIM)), KV_LORA),
        "w_out_c": nrm(ks[9], (N_ODD, MLA_HEADS * V_DIM, d), MLA_HEADS * V_DIM, BETA),
        "ln1_g": 1.0 + 0.02 * jax.random.normal(ks[10], (DEPTH, d), f32),
        "ln1_b": 0.02 * jax.random.normal(ks[11], (DEPTH, d), f32),
        "w_gate": nrm(ks[12], (DEPTH, d, FFN_HIDDEN), d),
        "w_up": nrm(ks[13], (DEPTH, d, FFN_HIDDEN), d),
        "w_down": nrm(ks[14], (DEPTH, FFN_HIDDEN, d), FFN_HIDDEN, BETA),
        "ln2_g": 1.0 + 0.02 * jax.random.normal(ks[15], (DEPTH, d), f32),
        "ln2_b": 0.02 * jax.random.normal(ks[16], (DEPTH, d), f32),
    }


def reference(x, w_in_a, conv_w, w_out_a, w_in_c, q_norm, kv_norm, w_uq, w_ukv,
              w_out_c, ln1_g, ln1_b, w_gate, w_up, w_down, ln2_g, ln2_b):
    pos = jnp.arange(x.shape[1])
    for l in range(DEPTH):
        j = l // 2
        if l % 2 == 0:
            mix = attn_conv_mixer(x, w_in_a[j], conv_w[j], w_out_a[j], pos)
        else:
            mix = mla_mixer(x, w_in_c[j], q_norm[j], kv_norm[j], w_uq[j], w_ukv[j],
                            w_out_c[j], pos)
        x = layernorm(ALPHA * x + mix, ln1_g[l], ln1_b[l])
        x = layernorm(ALPHA * x + swiglu(x, w_gate[l], w_up[l], w_down[l]), ln2_g[l], ln2_b[l])
    return x
```

```python
import functools
import math

import jax
import jax.numpy as jnp
from jax import lax
from jax.experimental import pallas as pl
from jax.experimental.pallas import tpu as pltpu

F32 = jnp.float32
BF16 = jnp.bfloat16

HEAD_DIM = 128
CONV_WIDTH = 3
DILATED_PATTERNS = ((128, 1), (512, 4), (2048, 16))
DIL_BLOCK = 128
Q_LORA = 1536
KV_LORA = 512
QK_NOPE = 128
QK_ROPE = 64
V_DIM = 128
ROPE_THETA = 10000.0
LN_EPS = 1e-5
RMS_EPS = 1e-6
NEG = -1e30

LANES = 128
SUBLANES = 8
VMEM_LIMIT_BYTES = 56 * 1024 * 1024


def _round_up(n, m):
    return -(-n // m) * m


def _tile(n, pref):
    if n <= pref:
        return n
    t = pref
    while n % t:
        t //= 2
    return t


def _params(*sem):
    return pltpu.CompilerParams(dimension_semantics=sem, vmem_limit_bytes=VMEM_LIMIT_BYTES)


def _rope_tables(seq, dim):
    half = dim // 2
    inv = ROPE_THETA ** (-jnp.arange(half, dtype=F32) * 2.0 / dim)
    ang = jnp.arange(seq).astype(F32)[:, None] * inv[None, :]
    cos, sin = jnp.cos(ang), jnp.sin(ang)
    pad = jnp.zeros((seq, LANES - dim), F32)
    return (jnp.concatenate([cos, cos, pad], axis=1),
            jnp.concatenate([-sin, sin, pad], axis=1))


def _rope_full_lanes(t, cos, sin):
    return t * cos + pltpu.roll(t, HEAD_DIM // 2, axis=1) * sin


def _rope_low_lanes(y, cos, sin):
    half = QK_ROPE // 2
    lane = lax.broadcasted_iota(jnp.int32, y.shape, 1)
    partner = jnp.where(lane < half, pltpu.roll(y, LANES - half, axis=1),
                        pltpu.roll(y, half, axis=1))
    return y * cos + partner * sin


def _mm_kernel(a_ref, w_ref, *rest, rope, scale, n_scaled_tiles):
    if rope:
        cos_ref, sin_ref, o_ref = rest
    else:
        (o_ref,) = rest
    acc = jnp.dot(a_ref[...], w_ref[...], preferred_element_type=F32)
    if not rope:
        o_ref[...] = acc.astype(o_ref.dtype)
        return
    cos = cos_ref[...]
    sin = sin_ref[...]
    mult = jnp.where(pl.program_id(1) < n_scaled_tiles, scale, 1.0).astype(F32)
    for c in range(acc.shape[1] // HEAD_DIM):
        sl = slice(c * HEAD_DIM, (c + 1) * HEAD_DIM)
        o_ref[:, sl] = (_rope_full_lanes(acc[:, sl], cos, sin) * mult).astype(o_ref.dtype)


def _matmul(a, w, col0, ncols, out_dtype, *, rope_tables=None, scale=1.0, scaled_cols=0,
            tm_pref=1024, tn_pref=512):
    m, k = a.shape
    tm = _tile(m, tm_pref)
    tn = _tile(math.gcd(ncols, col0, scaled_cols), tn_pref)
    off = col0 // tn
    in_specs = [pl.BlockSpec((tm, k), lambda i, j: (i, 0)),
                pl.BlockSpec((k, tn), lambda i, j: (0, j + off))]
    args = [a, w]
    if rope_tables is not None:
        in_specs += [pl.BlockSpec((tm, LANES), lambda i, j: (i, 0))] * 2
        args += list(rope_tables)
    return pl.pallas_call(
        functools.partial(_mm_kernel, rope=rope_tables is not None, scale=scale,
                          n_scaled_tiles=scaled_cols // tn),
        out_shape=jax.ShapeDtypeStruct((m, ncols), out_dtype),
        grid=(m // tm, ncols // tn),
        in_specs=in_specs,
        out_specs=pl.BlockSpec((tm, tn), lambda i, j: (i, j)),
        compiler_params=_params("parallel", "arbitrary"),
    )(*args)


def _dilated_kernel(q_ref, kp_ref, kc_ref, vp_ref, vc_ref, o_ref, lse_ref, *, heads):
    n = pl.program_id(1)
    blk = DIL_BLOCK
    qi = lax.broadcasted_iota(jnp.int32, (blk, 2 * blk), 0)
    ki = lax.broadcasted_iota(jnp.int32, (blk, 2 * blk), 1)
    dist = qi + blk - ki
    valid = (dist >= 0) & (dist <= blk) & ((ki >= blk) | (n > 0))
    lane = lax.broadcasted_iota(jnp.int32, (blk, LANES), 1)
    lse = jnp.zeros((blk, LANES), F32)
    for h in range(heads):
        sl = slice(h * HEAD_DIM, (h + 1) * HEAD_DIM)
        q = q_ref[:, sl]
        kk = jnp.concatenate([kp_ref[:, sl], kc_ref[:, sl]], axis=0)
        vv = jnp.concatenate([vp_ref[:, sl], vc_ref[:, sl]], axis=0)
        s = lax.dot_general(q, kk, (((1,), (1,)), ((), ())), preferred_element_type=F32)
        s = jnp.where(valid, s, NEG)
        m = jnp.max(s, axis=-1, keepdims=True)
        p = jnp.exp(s - m)
        den = jnp.sum(p, axis=-1, keepdims=True)
        o = jnp.dot(p.astype(BF16), vv, preferred_element_type=F32)
        o_ref[:, sl] = o / den
        lse = jnp.where(lane == h, m + jnp.log(den), lse)
    lse_ref[...] = lse


def _dilated_branch(qk, v, dilation):
    s, a2 = qk.shape
    a = a2 // 2
    heads = a // HEAD_DIM
    d = dilation
    rows = s // d
    nb = rows // DIL_BLOCK
    qk_r = qk.reshape(rows, d * a2)
    v_r = v.reshape(rows, d * a)
    blk = DIL_BLOCK
    cur = lambda r, n: n
    prev = lambda r, n: jnp.maximum(n - 1, 0)
    o, lse = pl.pallas_call(
        functools.partial(_dilated_kernel, heads=heads),
        out_shape=(jax.ShapeDtypeStruct((rows, d * a), F32),
                   jax.ShapeDtypeStruct((rows, d * LANES), F32)),
        grid=(d, nb),
        in_specs=[
            pl.BlockSpec((blk, a), lambda r, n: (cur(r, n), 2 * r)),
            pl.BlockSpec((blk, a), lambda r, n: (prev(r, n), 2 * r + 1)),
            pl.BlockSpec((blk, a), lambda r, n: (cur(r, n), 2 * r + 1)),
            pl.BlockSpec((blk, a), lambda r, n: (prev(r, n), r)),
            pl.BlockSpec((blk, a), lambda r, n: (cur(r, n), r)),
        ],
        out_specs=(pl.BlockSpec((blk, a), lambda r, n: (n, r)),
                   pl.BlockSpec((blk, LANES), lambda r, n: (n, r))),
        compiler_params=_params("parallel", "arbitrary"),
    )(qk_r, qk_r, qk_r, v_r, v_r)
    return o.reshape(s, a), lse.reshape(s, LANES)


def _mix_kernel(o1_ref, o2_ref, o3_ref, l1_ref, l2_ref, l3_ref,
                gb_ref, gc_ref, hin_ref, gch_ref, hinh_ref, cw_ref, out_ref, *, heads, a_width):
    i = pl.program_id(0)
    l1, l2, l3 = l1_ref[...], l2_ref[...], l3_ref[...]
    m = jnp.maximum(jnp.maximum(l1, l2), l3)
    e1, e2, e3 = jnp.exp(l1 - m), jnp.exp(l2 - m), jnp.exp(l3 - m)
    inv = 1.0 / (e1 + e2 + e3)
    w1, w2, w3 = e1 * inv, e2 * inv, e3 * inv
    for h in range(heads):
        sl = slice(h * HEAD_DIM, (h + 1) * HEAD_DIM)
        hs = slice(h, h + 1)
        mixed = (w1[:, hs] * o1_ref[:, sl] + w2[:, hs] * o2_ref[:, sl]
                 + w3[:, hs] * o3_ref[:, sl])
        out_ref[:, sl] = mixed.astype(out_ref.dtype)

    u = gc_ref[...] * hin_ref[...]
    halo = gch_ref[...] * hinh_ref[...]
    halo = halo * jnp.where(i > 0, 1.0, 0.0).astype(F32)
    row = lax.broadcasted_iota(jnp.int32, u.shape, 0)
    hm1 = halo[SUBLANES - 1:SUBLANES, :]
    hm2 = halo[SUBLANES - 2:SUBLANES - 1, :]
    u1 = jnp.where(row == 0, hm1, pltpu.roll(u, 1, axis=0))
    u2 = jnp.where(row == 0, hm2, jnp.where(row == 1, hm1, pltpu.roll(u, 2, axis=0)))
    cw = cw_ref[...]
    y = cw[0:1, :] * u2 + cw[1:2, :] * u1 + cw[2:3, :] * u
    out_ref[:, a_width:] = (gb_ref[...] * y).astype(out_ref.dtype)


def _mix(o_list, lse_list, gates, conv_w, a_width, tm_pref=256):
    s = gates.shape[0]
    b_width = gates.shape[1] // 3
    assert b_width == a_width
    heads = a_width // HEAD_DIM
    tm = _tile(s, tm_pref)
    hb = tm // SUBLANES
    row_blk = lambda c: pl.BlockSpec((tm, a_width), lambda i: (i, c))
    halo_blk = lambda c: pl.BlockSpec((SUBLANES, b_width),
                                      lambda i: (jnp.maximum(i * hb - 1, 0), c))
    lse_blk = pl.BlockSpec((tm, LANES), lambda i: (i, 0))
    return pl.pallas_call(
        functools.partial(_mix_kernel, heads=heads, a_width=a_width),
        out_shape=jax.ShapeDtypeStruct((s, a_width + b_width), BF16),
        grid=(s // tm,),
        in_specs=[row_blk(0)] * 3 + [lse_blk] * 3
        + [row_blk(0), row_blk(1), row_blk(2), halo_blk(1), halo_blk(2),
           pl.BlockSpec((CONV_WIDTH, b_width), lambda i: (0, 0))],
        out_specs=pl.BlockSpec((tm, a_width + b_width), lambda i: (i, 0)),
        compiler_params=_params("parallel"),
    )(*o_list, *lse_list, gates, gates, gates, gates, gates, conv_w)


LN_COL_CHUNK = 1024
LN_ROW_CHUNK = 64


def _mm_res_ln_kernel(a_ref, w_ref, x_ref, g_ref, b_ref, o_ref, *maybe_ob, alpha):
    k = pl.program_id(1)
    tm, n = o_ref.shape

    @pl.when(k == 0)
    def _():
        o_ref[...] = jnp.zeros(o_ref.shape, F32)

    a = a_ref[...]
    nc = _tile(n, LN_COL_CHUNK)
    for c in range(n // nc):
        sl = slice(c * nc, (c + 1) * nc)
        o_ref[:, sl] += jnp.dot(a, w_ref[:, sl], preferred_element_type=F32)

    @pl.when(k == pl.num_programs(1) - 1)
    def _():
        rc = _tile(tm, LN_ROW_CHUNK)

        def chunk(r, carry):
            rows = pl.ds(pl.multiple_of(r * rc, rc), rc)
            z = alpha * x_ref[rows, :] + o_ref[rows, :]
            mu = jnp.mean(z, axis=-1, keepdims=True)
            zc = z - mu
            var = jnp.mean(zc * zc, axis=-1, keepdims=True)
            y = zc * lax.rsqrt(var + LN_EPS) * g_ref[...] + b_ref[...]
            o_ref[rows, :] = y
            if maybe_ob:
                maybe_ob[0][rows, :] = y.astype(BF16)
            return carry

        lax.fori_loop(0, tm // rc, chunk, 0)


def _mm_res_ln(a, w, xres, g, b, alpha, *, want_bf16, tm_pref=512, tk_pref=512):
    m, k = a.shape
    n = w.shape[1]
    tm = _tile(m, tm_pref)
    tk = _tile(k, tk_pref)
    out_shape = [jax.ShapeDtypeStruct((m, n), F32)]
    out_specs = [pl.BlockSpec((tm, n), lambda i, kk: (i, 0))]
    if want_bf16:
        out_shape.append(jax.ShapeDtypeStruct((m, n), BF16))
        out_specs.append(pl.BlockSpec((tm, n), lambda i, kk: (i, 0)))
    res = pl.pallas_call(
        functools.partial(_mm_res_ln_kernel, alpha=alpha),
        out_shape=tuple(out_shape),
        grid=(m // tm, k // tk),
        in_specs=[
            pl.BlockSpec((tm, tk), lambda i, kk: (i, kk)),
            pl.BlockSpec((tk, n), lambda i, kk: (kk, 0)),
            pl.BlockSpec((tm, n), lambda i, kk: (i, 0), pipeline_mode=pl.Buffered(1)),
            pl.BlockSpec((1, n), lambda i, kk: (0, 0)),
            pl.BlockSpec((1, n), lambda i, kk: (0, 0)),
        ],
        out_specs=tuple(out_specs),
        compiler_params=_params("parallel", "arbitrary"),
    )(a, w, xres, g.reshape(1, n), b.reshape(1, n))
    return res if want_bf16 else (res[0], None)


def _ffn_up_kernel(a_ref, wg_ref, wu_ref, o_ref):
    a = a_ref[...]
    g = jnp.dot(a, wg_ref[...], preferred_element_type=F32)
    u = jnp.dot(a, wu_ref[...], preferred_element_type=F32)
    o_ref[...] = (g * jax.nn.sigmoid(g) * u).astype(o_ref.dtype)


def _ffn_up(a, wg, wu, tm_pref=1024, tn_pref=512):
    m, k = a.shape
    n = wg.shape[1]
    tm = _tile(m, tm_pref)
    tn = _tile(n, tn_pref)
    return pl.pallas_call(
        _ffn_up_kernel,
        out_shape=jax.ShapeDtypeStruct((m, n), BF16),
        grid=(m // tm, n // tn),
        in_specs=[pl.BlockSpec((tm, k), lambda i, j: (i, 0)),
                  pl.BlockSpec((k, tn), lambda i, j: (0, j)),
                  pl.BlockSpec((k, tn), lambda i, j: (0, j))],
        out_specs=pl.BlockSpec((tm, tn), lambda i, j: (i, j)),
        compiler_params=_params("parallel", "arbitrary"),
    )(a, wg, wu)


def _mla_in_kernel(a_ref, w_ref, qn_ref, kvn_ref, cos_ref, sin_ref,
                   cq_ref, ckv_ref, kr_ref, acc_ref):
    k = pl.program_id(1)
    part = jnp.dot(a_ref[...], w_ref[...], preferred_element_type=F32)

    @pl.when(k == 0)
    def _():
        acc_ref[...] = part

    @pl.when(k > 0)
    def _():
        acc_ref[...] += part

    @pl.when(k == pl.num_programs(1) - 1)
    def _():
        def rms(t, gain):
            r = lax.rsqrt(jnp.mean(t * t, axis=-1, keepdims=True) + RMS_EPS)
            return t * r * gain

        cq_ref[...] = rms(acc_ref[:, :Q_LORA], qn_ref[...]).astype(BF16)
        ckv_ref[...] = rms(acc_ref[:, Q_LORA:Q_LORA + KV_LORA], kvn_ref[...]).astype(BF16)
        y = acc_ref[:, Q_LORA + KV_LORA:]
        kr_ref[...] = _rope_low_lanes(y, cos_ref[...], sin_ref[...]).astype(BF16)


def _mla_in(a, w, q_norm, kv_norm, tables, tm_pref=1024, tk_pref=1024):
    m, k = a.shape
    n = w.shape[1]
    tm = _tile(m, tm_pref)
    tk = _tile(k, tk_pref)
    row = lambda width: pl.BlockSpec((tm, width), lambda i, kk: (i, 0))
    return pl.pallas_call(
        _mla_in_kernel,
        out_shape=(jax.ShapeDtypeStruct((m, Q_LORA), BF16),
                   jax.ShapeDtypeStruct((m, KV_LORA), BF16),
                   jax.ShapeDtypeStruct((m, LANES), BF16)),
        grid=(m // tm, k // tk),
        in_specs=[pl.BlockSpec((tm, tk), lambda i, kk: (i, kk)),
                  pl.BlockSpec((tk, n), lambda i, kk: (kk, 0)),
                  pl.BlockSpec((1, Q_LORA), lambda i, kk: (0, 0)),
                  pl.BlockSpec((1, KV_LORA), lambda i, kk: (0, 0)),
                  row(LANES), row(LANES)],
        out_specs=(row(Q_LORA), row(KV_LORA), row(LANES)),
        scratch_shapes=[pltpu.VMEM((tm, n), F32)],
        compiler_params=_params("parallel", "arbitrary"),
    )(a, w, q_norm.reshape(1, Q_LORA), kv_norm.reshape(1, KV_LORA), *tables)


QK_DIM = QK_NOPE + QK_ROPE
Q_PAD = 2 * LANES


def _q_up_kernel(a_ref, w_ref, cos_ref, sin_ref, q_ref, *, heads_per_step, scale):
    acc = jnp.dot(a_ref[...], w_ref[...], preferred_element_type=F32)
    cos, sin = cos_ref[...], sin_ref[...]
    for h in range(heads_per_step):
        base = h * Q_PAD
        q_ref[h, :, :QK_NOPE] = (acc[:, base:base + QK_NOPE] * scale).astype(BF16)
        roped = _rope_low_lanes(acc[:, base + QK_NOPE:base + Q_PAD], cos, sin) * scale
        q_ref[h, :, QK_NOPE:] = roped[:, :QK_ROPE].astype(BF16)


def _q_up(cq, w_uq_p, tables, heads, scale, tm_pref=1024, hps_pref=4):
    m, k = cq.shape
    tm = _tile(m, tm_pref)
    hps = _tile(heads, hps_pref)
    return pl.pallas_call(
        functools.partial(_q_up_kernel, heads_per_step=hps, scale=scale),
        out_shape=jax.ShapeDtypeStruct((heads, m, QK_DIM), BF16),
        grid=(m // tm, heads // hps),
        in_specs=[pl.BlockSpec((tm, k), lambda i, j: (i, 0)),
                  pl.BlockSpec((k, hps * Q_PAD), lambda i, j: (0, j)),
                  pl.BlockSpec((tm, LANES), lambda i, j: (i, 0)),
                  pl.BlockSpec((tm, LANES), lambda i, j: (i, 0))],
        out_specs=pl.BlockSpec((hps, tm, QK_DIM), lambda i, j: (j, i, 0)),
        compiler_params=_params("parallel", "arbitrary"),
    )(cq, w_uq_p, *tables)


def _kv_up_kernel(a_ref, w_ref, kr_ref, k_ref, v_ref, *, heads_per_step):
    acc = jnp.dot(a_ref[...], w_ref[...], preferred_element_type=F32)
    kr = kr_ref[:, :QK_ROPE]
    for h in range(heads_per_step):
        base = h * (QK_NOPE + V_DIM)
        k_ref[h, :, :QK_NOPE] = acc[:, base:base + QK_NOPE].astype(BF16)
        k_ref[h, :, QK_NOPE:] = kr
        v_ref[h] = acc[:, base + QK_NOPE:base + QK_NOPE + V_DIM].astype(BF16)


def _kv_up(ckv, w_ukv, kr, heads, tm_pref=1024, hps_pref=4):
    m, k = ckv.shape
    tm = _tile(m, tm_pref)
    hps = _tile(heads, hps_pref)
    width = QK_NOPE + V_DIM
    return pl.pallas_call(
        functools.partial(_kv_up_kernel, heads_per_step=hps),
        out_shape=(jax.ShapeDtypeStruct((heads, m, QK_DIM), BF16),
                   jax.ShapeDtypeStruct((heads, m, V_DIM), BF16)),
        grid=(m // tm, heads // hps),
        in_specs=[pl.BlockSpec((tm, k), lambda i, j: (i, 0)),
                  pl.BlockSpec((k, hps * width), lambda i, j: (0, j)),
                  pl.BlockSpec((tm, LANES), lambda i, j: (i, 0))],
        out_specs=(pl.BlockSpec((hps, tm, QK_DIM), lambda i, j: (j, i, 0)),
                   pl.BlockSpec((hps, tm, V_DIM), lambda i, j: (j, i, 0))),
        compiler_params=_params("parallel", "arbitrary"),
    )(ckv, w_ukv, kr)


def _mla_attn_kernel(q_ref, k_ref, v_ref, o_ref, m_ref, l_ref, acc_ref, *, blk):
    qi = pl.program_id(1)
    q = q_ref[0]

    m_ref[...] = jnp.full(m_ref.shape, NEG, F32)
    l_ref[...] = jnp.zeros(l_ref.shape, F32)
    acc_ref[...] = jnp.zeros(acc_ref.shape, F32)

    def step(j, masked):
        start = pl.multiple_of(j * blk, blk)
        kk = k_ref[0, pl.ds(start, blk), :]
        vv = v_ref[0, pl.ds(start, blk), :]
        s = lax.dot_general(q, kk, (((1,), (1,)), ((), ())), preferred_element_type=F32)
        if masked:
            row = lax.broadcasted_iota(jnp.int32, s.shape, 0)
            col = lax.broadcasted_iota(jnp.int32, s.shape, 1)
            s = jnp.where(col <= row, s, NEG)
        m_prev = m_ref[...]
        m_new = jnp.maximum(m_prev, jnp.max(s, axis=-1, keepdims=True))
        corr = jnp.exp(m_prev - m_new)
        p = jnp.exp(s - m_new)
        l_ref[...] = corr * l_ref[...] + jnp.sum(p, axis=-1, keepdims=True)
        acc_ref[...] = corr * acc_ref[...] + jnp.dot(p.astype(BF16), vv,
                                                     preferred_element_type=F32)
        m_ref[...] = m_new

    def body(j, carry):
        step(j, masked=False)
        return carry

    lax.fori_loop(0, qi, body, 0)
    step(qi, masked=True)
    o_ref[...] = (acc_ref[...] / l_ref[...]).astype(o_ref.dtype)


def _mla_attn(q, k, v, blk_pref=512):
    heads, s, _ = q.shape
    blk = _tile(s, blk_pref)
    return pl.pallas_call(
        functools.partial(_mla_attn_kernel, blk=blk),
        out_shape=jax.ShapeDtypeStruct((s, heads * V_DIM), BF16),
        grid=(heads, s // blk),
        in_specs=[pl.BlockSpec((1, blk, QK_DIM), lambda h, i: (h, i, 0)),
                  pl.BlockSpec((1, s, QK_DIM), lambda h, i: (h, 0, 0)),
                  pl.BlockSpec((1, s, V_DIM), lambda h, i: (h, 0, 0))],
        out_specs=pl.BlockSpec((blk, V_DIM), lambda h, i: (i, h)),
        scratch_shapes=[pltpu.VMEM((blk, 1), F32), pltpu.VMEM((blk, 1), F32),
                        pltpu.VMEM((blk, V_DIM), F32)],
        compiler_params=_params("parallel", "arbitrary"),
    )(q, k, v)


def _ffn_block(x, xb, w_gate, w_up, w_down, g, b, alpha, want_bf16):
    d, hidden = w_gate.shape
    hp = _round_up(hidden, 512)
    wg = jnp.pad(w_gate, ((0, 0), (0, hp - hidden))).astype(BF16)
    wu = jnp.pad(w_up, ((0, 0), (0, hp - hidden))).astype(BF16)
    wd = jnp.pad(w_down, ((0, hp - hidden), (0, 0))).astype(BF16)
    hmid = _ffn_up(xb, wg, wu)
    return _mm_res_ln(hmid, wd, x, g, b, alpha, want_bf16=want_bf16)


def _attn_conv_layer(x, xb, w_in, conv_w, w_out, g, b, alpha):
    s, d = x.shape
    a_width = d // 2
    w_in_b = w_in.astype(BF16)
    tables = _rope_tables(s, HEAD_DIM)
    qk = _matmul(xb, w_in_b, 0, 2 * a_width, BF16, rope_tables=tables,
                 scale=HEAD_DIM ** -0.5, scaled_cols=a_width)
    v = _matmul(xb, w_in_b, 2 * a_width, a_width, BF16)
    gates = _matmul(xb, w_in_b, 3 * a_width, 3 * (d - a_width), F32)
    outs, lses = [], []
    for window, dilation in DILATED_PATTERNS:
        assert window // dilation == DIL_BLOCK and s % window == 0
        o, lse = _dilated_branch(qk, v, dilation)
        outs.append(o)
        lses.append(lse)
    ab = _mix(outs, lses, gates, conv_w, a_width)
    return _mm_res_ln(ab, w_out.astype(BF16), x, g, b, alpha, want_bf16=True)


def _mla_layer(x, xb, w_in, q_norm, kv_norm, w_uq, w_ukv, w_out, g, b, alpha):
    s, d = x.shape
    heads = d // 128
    w_in_p = jnp.pad(w_in, ((0, 0), (0, LANES - QK_ROPE))).astype(BF16)
    cq, ckv, kr = _mla_in(xb, w_in_p, q_norm, kv_norm, _rope_tables(s, QK_ROPE))
    w_uq_p = jnp.pad(w_uq.reshape(Q_LORA, heads, QK_DIM),
                     ((0, 0), (0, 0), (0, Q_PAD - QK_DIM))).reshape(Q_LORA, heads * Q_PAD)
    q = _q_up(cq, w_uq_p.astype(BF16), _rope_tables(s, QK_ROPE), heads, QK_DIM ** -0.5)
    k, v = _kv_up(ckv, w_ukv.astype(BF16), kr, heads)
    o = _mla_attn(q, k, v)
    return _mm_res_ln(o, w_out.astype(BF16), x, g, b, alpha, want_bf16=True)


def kernel(x, w_in_a, conv_w, w_out_a, w_in_c, q_norm, kv_norm, w_uq, w_ukv, w_out_c,
           ln1_g, ln1_b, w_gate, w_up, w_down, ln2_g, ln2_b):
    batch, s, d = x.shape
    depth = ln1_g.shape[0]
    alpha = (2.0 * depth) ** 0.25
    outs = []
    for bi in range(batch):
        xf = x[bi]
        xb = xf.astype(BF16)
        for l in range(depth):
            j = l // 2
            if l % 2 == 0:
                xf, xb = _attn_conv_layer(xf, xb, w_in_a[j], conv_w[j], w_out_a[j],
                                          ln1_g[l], ln1_b[l], alpha)
            else:
                xf, xb = _mla_layer(xf, xb, w_in_c[j], q_norm[j], kv_norm[j], w_uq[j],
                                    w_ukv[j], w_out_c[j], ln1_g[l], ln1_b[l], alpha)
            xf, xb = _ffn_block(xf, xb, w_gate[l], w_up[l], w_down[l], ln2_g[l], ln2_b[l],
                                alpha, want_bf16=l + 1 < depth)
        outs.append(xf)
    return jnp.stack(outs)
```

```python
import functools
import math

import jax
import jax.numpy as jnp
from jax import lax
from jax.experimental import pallas as pl
from jax.experimental.pallas import tpu as pltpu

F32 = jnp.float32
BF16 = jnp.bfloat16

HEAD_DIM = 128
CONV_WIDTH = 3
DILATED_PATTERNS = ((128, 1), (512, 4), (2048, 16))
DIL_BLOCK = 128
Q_LORA = 1536
KV_LORA = 512
QK_NOPE = 128
QK_ROPE = 64
V_DIM = 128
ROPE_THETA = 10000.0
LN_EPS = 1e-5
RMS_EPS = 1e-6
NEG = -1e30

LANES = 128
SUBLANES = 8
VMEM_LIMIT_BYTES = 56 * 1024 * 1024


def _round_up(n, m):
    return -(-n // m) * m


def _tile(n, pref):
    if n <= pref:
        return n
    t = pref
    while n % t:
        t //= 2
    return t


def _params(*sem):
    return pltpu.CompilerParams(dimension_semantics=sem, vmem_limit_bytes=VMEM_LIMIT_BYTES)


def _rope_tables(seq, dim):
    half = dim // 2
    inv = ROPE_THETA ** (-jnp.arange(half, dtype=F32) * 2.0 / dim)
    ang = jnp.arange(seq).astype(F32)[:, None] * inv[None, :]
    cos, sin = jnp.cos(ang), jnp.sin(ang)
    pad = jnp.zeros((seq, LANES - dim), F32)
    return (jnp.concatenate([cos, cos, pad], axis=1),
            jnp.concatenate([-sin, sin, pad], axis=1))


def _rope_full_lanes(t, cos, sin):
    return t * cos + pltpu.roll(t, HEAD_DIM // 2, axis=1) * sin


def _rope_low_lanes(y, cos, sin):
    half = QK_ROPE // 2
    lane = lax.broadcasted_iota(jnp.int32, y.shape, 1)
    partner = jnp.where(lane < half, pltpu.roll(y, LANES - half, axis=1),
                        pltpu.roll(y, half, axis=1))
    return y * cos + partner * sin


def _mm_kernel(a_ref, w_ref, *rest, rope, scale, n_scaled_tiles):
    if rope:
        cos_ref, sin_ref, o_ref = rest
    else:
        (o_ref,) = rest
    acc = jnp.dot(a_ref[...], w_ref[...], preferred_element_type=F32)
    if not rope:
        o_ref[...] = acc.astype(o_ref.dtype)
        return
    cos = cos_ref[...]
    sin = sin_ref[...]
    mult = jnp.where(pl.program_id(1) < n_scaled_tiles, scale, 1.0).astype(F32)
    for c in range(acc.shape[1] // HEAD_DIM):
        sl = slice(c * HEAD_DIM, (c + 1) * HEAD_DIM)
        o_ref[:, sl] = (_rope_full_lanes(acc[:, sl], cos, sin) * mult).astype(o_ref.dtype)


def _matmul(a, w, col0, ncols, out_dtype, *, rope_tables=None, scale=1.0, scaled_cols=0,
            tm_pref=1024, tn_pref=512):
    m, k = a.shape
    tm = _tile(m, tm_pref)
    tn = _tile(math.gcd(ncols, col0, scaled_cols), tn_pref)
    off = col0 // tn
    in_specs = [pl.BlockSpec((tm, k), lambda i, j: (i, 0)),
                pl.BlockSpec((k, tn), lambda i, j: (0, j + off))]
    args = [a, w]
    if rope_tables is not None:
        in_specs += [pl.BlockSpec((tm, LANES), lambda i, j: (i, 0))] * 2
        args += list(rope_tables)
    return pl.pallas_call(
        functools.partial(_mm_kernel, rope=rope_tables is not None, scale=scale,
                          n_scaled_tiles=scaled_cols // tn),
        out_shape=jax.ShapeDtypeStruct((m, ncols), out_dtype),
        grid=(m // tm, ncols // tn),
        in_specs=in_specs,
        out_specs=pl.BlockSpec((tm, tn), lambda i, j: (i, j)),
        compiler_params=_params("parallel", "arbitrary"),
    )(*args)


def _dilated_kernel(q_ref, kp_ref, kc_ref, vp_ref, vc_ref, o_ref, lse_ref, *, heads):
    n = pl.program_id(1)
    blk = DIL_BLOCK
    qi = lax.broadcasted_iota(jnp.int32, (blk, 2 * blk), 0)
    ki = lax.broadcasted_iota(jnp.int32, (blk, 2 * blk), 1)
    dist = qi + blk - ki
    valid = (dist >= 0) & (dist <= blk) & ((ki >= blk) | (n > 0))
    lane = lax.broadcasted_iota(jnp.int32, (blk, LANES), 1)
    lse = jnp.zeros((blk, LANES), F32)
    for h in range(heads):
        sl = slice(h * HEAD_DIM, (h + 1) * HEAD_DIM)
        q = q_ref[:, sl]
        kk = jnp.concatenate([kp_ref[:, sl], kc_ref[:, sl]], axis=0)
        vv = jnp.concatenate([vp_ref[:, sl], vc_ref[:, sl]], axis=0)
        s = lax.dot_general(q, kk, (((1,), (1,)), ((), ())), preferred_element_type=F32)
        s = jnp.where(valid, s, NEG)
        m = jnp.max(s, axis=-1, keepdims=True)
        p = jnp.exp(s - m)
        den = jnp.sum(p, axis=-1, keepdims=True)
        o = jnp.dot(p.astype(BF16), vv, preferred_element_type=F32)
        o_ref[:, sl] = o / den
        lse = jnp.where(lane == h, m + jnp.log(den), lse)
    lse_ref[...] = lse


def _dilated_branch(qk, v, dilation):
    s, a2 = qk.shape
    a = a2 // 2
    heads = a // HEAD_DIM
    d = dilation
    rows = s // d
    nb = rows // DIL_BLOCK
    qk_r = qk.reshape(rows, d * a2)
    v_r = v.reshape(rows, d * a)
    blk = DIL_BLOCK
    cur = lambda r, n: n
    prev = lambda r, n: jnp.maximum(n - 1, 0)
    o, lse = pl.pallas_call(
        functools.partial(_dilated_kernel, heads=heads),
        out_shape=(jax.ShapeDtypeStruct((rows, d * a), F32),
                   jax.ShapeDtypeStruct((rows, d * LANES), F32)),
        grid=(d, nb),
        in_specs=[
            pl.BlockSpec((blk, a), lambda r, n: (cur(r, n), 2 * r)),
            pl.BlockSpec((blk, a), lambda r, n: (prev(r, n), 2 * r + 1)),
            pl.BlockSpec((blk, a), lambda r, n: (cur(r, n), 2 * r + 1)),
            pl.BlockSpec((blk, a), lambda r, n: (prev(r, n), r)),
            pl.BlockSpec((blk, a), lambda r, n: (cur(r, n), r)),
        ],
        out_specs=(pl.BlockSpec((blk, a), lambda r, n: (n, r)),
                   pl.BlockSpec((blk, LANES), lambda r, n: (n, r))),
        compiler_params=_params("parallel", "arbitrary"),
    )(qk_r, qk_r, qk_r, v_r, v_r)
    return o.reshape(s, a), lse.reshape(s, LANES)


def _mix_kernel(o1_ref, o2_ref, o3_ref, l1_ref, l2_ref, l3_ref,
                gb_ref, gc_ref, hin_ref, gch_ref, hinh_ref, cw_ref, out_ref, *, heads, a_width):
    i = pl.program_id(0)
    l1, l2, l3 = l1_ref[...], l2_ref[...], l3_ref[...]
    m = jnp.maximum(jnp.maximum(l1, l2), l3)
    e1, e2, e3 = jnp.exp(l1 - m), jnp.exp(l2 - m), jnp.exp(l3 - m)
    inv = 1.0 / (e1 + e2 + e3)
    w1, w2, w3 = e1 * inv, e2 * inv, e3 * inv
    for h in range(heads):
        sl = slice(h * HEAD_DIM, (h + 1) * HEAD_DIM)
        hs = slice(h, h + 1)
        mixed = (w1[:, hs] * o1_ref[:, sl] + w2[:, hs] * o2_ref[:, sl]
                 + w3[:, hs] * o3_ref[:, sl])
        out_ref[:, sl] = mixed.astype(out_ref.dtype)

    u = gc_ref[...] * hin_ref[...]
    halo = gch_ref[...] * hinh_ref[...]
    halo = halo * jnp.where(i > 0, 1.0, 0.0).astype(F32)
    row = lax.broadcasted_iota(jnp.int32, u.shape, 0)
    hm1 = halo[SUBLANES - 1:SUBLANES, :]
    hm2 = halo[SUBLANES - 2:SUBLANES - 1, :]
    u1 = jnp.where(row == 0, hm1, pltpu.roll(u, 1, axis=0))
    u2 = jnp.where(row == 0, hm2, jnp.where(row == 1, hm1, pltpu.roll(u, 2, axis=0)))
    cw = cw_ref[...]
    y = cw[0:1, :] * u2 + cw[1:2, :] * u1 + cw[2:3, :] * u
    out_ref[:, a_width:] = (gb_ref[...] * y).astype(out_ref.dtype)


def _mix(o_list, lse_list, gates, conv_w, a_width, tm_pref=256):
    s = gates.shape[0]
    b_width = gates.shape[1] // 3
    assert b_width == a_width
    heads = a_width // HEAD_DIM
    tm = _tile(s, tm_pref)
    hb = tm // SUBLANES
    row_blk = lambda c: pl.BlockSpec((tm, a_width), lambda i: (i, c))
    halo_blk = lambda c: pl.BlockSpec((SUBLANES, b_width),
                                      lambda i: (jnp.maximum(i * hb - 1, 0), c))
    lse_blk = pl.BlockSpec((tm, LANES), lambda i: (i, 0))
    return pl.pallas_call(
        functools.partial(_mix_kernel, heads=heads, a_width=a_width),
        out_shape=jax.ShapeDtypeStruct((s, a_width + b_width), BF16),
        grid=(s // tm,),
        in_specs=[row_blk(0)] * 3 + [lse_blk] * 3
        + [row_blk(0), row_blk(1), row_blk(2), halo_blk(1), halo_blk(2),
           pl.BlockSpec((CONV_WIDTH, b_width), lambda i: (0, 0))],
        out_specs=pl.BlockSpec((tm, a_width + b_width), lambda i: (i, 0)),
        compiler_params=_params("parallel"),
    )(*o_list, *lse_list, gates, gates, gates, gates, gates, conv_w)


LN_COL_CHUNK = 1024
LN_ROW_CHUNK = 64


def _mm_res_ln_kernel(a_ref, w_ref, x_ref, g_ref, b_ref, o_ref, *maybe_ob, alpha):
    k = pl.program_id(1)
    tm, n = o_ref.shape

    @pl.when(k == 0)
    def _():
        o_ref[...] = jnp.zeros(o_ref.shape, F32)

    a = a_ref[...]
    nc = _tile(n, LN_COL_CHUNK)
    for c in range(n // nc):
        sl = slice(c * nc, (c + 1) * nc)
        o_ref[:, sl] += jnp.dot(a, w_ref[:, sl], preferred_element_type=F32)

    @pl.when(k == pl.num_programs(1) - 1)
    def _():
        rc = _tile(tm, LN_ROW_CHUNK)

        def chunk(r, carry):
            rows = pl.ds(pl.multiple_of(r * rc, rc), rc)
            z = alpha * x_ref[rows, :] + o_ref[rows, :]
            mu = jnp.mean(z, axis=-1, keepdims=True)
            zc = z - mu
            var = jnp.mean(zc * zc, axis=-1, keepdims=True)
            y = zc * lax.rsqrt(var + LN_EPS) * g_ref[...] + b_ref[...]
            o_ref[rows, :] = y
            if maybe_ob:
                maybe_ob[0][rows, :] = y.astype(BF16)
            return carry

        lax.fori_loop(0, tm // rc, chunk, 0)


def _mm_res_ln(a, w, xres, g, b, alpha, *, want_bf16, tm_pref=512, tk_pref=512):
    m, k = a.shape
    n = w.shape[1]
    tm = _tile(m, tm_pref)
    tk = _tile(k, tk_pref)
    out_shape = [jax.ShapeDtypeStruct((m, n), F32)]
    out_specs = [pl.BlockSpec((tm, n), lambda i, kk: (i, 0))]
    if want_bf16:
        out_shape.append(jax.ShapeDtypeStruct((m, n), BF16))
        out_specs.append(pl.BlockSpec((tm, n), lambda i, kk: (i, 0)))
    res = pl.pallas_call(
        functools.partial(_mm_res_ln_kernel, alpha=alpha),
        out_shape=tuple(out_shape),
        grid=(m // tm, k // tk),
        in_specs=[
            pl.BlockSpec((tm, tk), lambda i, kk: (i, kk)),
            pl.BlockSpec((tk, n), lambda i, kk: (kk, 0)),
            pl.BlockSpec((tm, n), lambda i, kk: (i, 0), pipeline_mode=pl.Buffered(1)),
            pl.BlockSpec((1, n), lambda i, kk: (0, 0)),
            pl.BlockSpec((1, n), lambda i, kk: (0, 0)),
        ],
        out_specs=tuple(out_specs),
        compiler_params=_params("parallel", "arbitrary"),
    )(a, w, xres, g.reshape(1, n), b.reshape(1, n))
    return res if want_bf16 else (res[0], None)


def _ffn_up_kernel(a_ref, wg_ref, wu_ref, o_ref):
    a = a_ref[...]
    g = jnp.dot(a, wg_ref[...], preferred_element_type=F32)
    u = jnp.dot(a, wu_ref[...], preferred_element_type=F32)
    o_ref[...] = (g * jax.nn.sigmoid(g) * u).astype(o_ref.dtype)


def _ffn_up(a, wg, wu, tm_pref=1024, tn_pref=512):
    m, k = a.shape
    n = wg.shape[1]
    tm = _tile(m, tm_pref)
    tn = _tile(n, tn_pref)
    return pl.pallas_call(
        _ffn_up_kernel,
        out_shape=jax.ShapeDtypeStruct((m, n), BF16),
        grid=(m // tm, n // tn),
        in_specs=[pl.BlockSpec((tm, k), lambda i, j: (i, 0)),
                  pl.BlockSpec((k, tn), lambda i, j: (0, j)),
                  pl.BlockSpec((k, tn), lambda i, j: (0, j))],
        out_specs=pl.BlockSpec((tm, tn), lambda i, j: (i, j)),
        compiler_params=_params("parallel", "arbitrary"),
    )(a, wg, wu)


def _mla_in_kernel(a_ref, w_ref, qn_ref, kvn_ref, cos_ref, sin_ref,
                   cq_ref, ckv_ref, kr_ref, acc_ref):
    k = pl.program_id(1)
    part = jnp.dot(a_ref[...], w_ref[...], preferred_element_type=F32)

    @pl.when(k == 0)
    def _():
        acc_ref[...] = part

    @pl.when(k > 0)
    def _():
        acc_ref[...] += part

    @pl.when(k == pl.num_programs(1) - 1)
    def _():
        def rms(t, gain):
            r = lax.rsqrt(jnp.mean(t * t, axis=-1, keepdims=True) + RMS_EPS)
            return t * r * gain

        cq_ref[...] = rms(acc_ref[:, :Q_LORA], qn_ref[...]).astype(BF16)
        ckv_ref[...] = rms(acc_ref[:, Q_LORA:Q_LORA + KV_LORA], kvn_ref[...]).astype(BF16)
        y = acc_ref[:, Q_LORA + KV_LORA:]
        kr_ref[...] = _rope_low_lanes(y, cos_ref[...], sin_ref[...]).astype(BF16)


def _mla_in(a, w, q_norm, kv_norm, tables, tm_pref=1024, tk_pref=1024):
    m, k = a.shape
    n = w.shape[1]
    tm = _tile(m, tm_pref)
    tk = _tile(k, tk_pref)
    row = lambda width: pl.BlockSpec((tm, width), lambda i, kk: (i, 0))
    return pl.pallas_call(
        _mla_in_kernel,
        out_shape=(jax.ShapeDtypeStruct((m, Q_LORA), BF16),
                   jax.ShapeDtypeStruct((m, KV_LORA), BF16),
                   jax.ShapeDtypeStruct((m, LANES), BF16)),
        grid=(m // tm, k // tk),
        in_specs=[pl.BlockSpec((tm, tk), lambda i, kk: (i, kk)),
                  pl.BlockSpec((tk, n), lambda i, kk: (kk, 0)),
                  pl.BlockSpec((1, Q_LORA), lambda i, kk: (0, 0)),
                  pl.BlockSpec((1, KV_LORA), lambda i, kk: (0, 0)),
                  row(LANES), row(LANES)],
        out_specs=(row(Q_LORA), row(KV_LORA), row(LANES)),
        scratch_shapes=[pltpu.VMEM((tm, n), F32)],
        compiler_params=_params("parallel", "arbitrary"),
    )(a, w, q_norm.reshape(1, Q_LORA), kv_norm.reshape(1, KV_LORA), *tables)


QK_DIM = QK_NOPE + QK_ROPE
Q_PAD = 2 * LANES


def _q_up_kernel(a_ref, w_ref, cos_ref, sin_ref, q_ref, *, heads_per_step, scale):
    acc = jnp.dot(a_ref[...], w_ref[...], preferred_element_type=F32)
    cos, sin = cos_ref[...], sin_ref[...]
    for h in range(heads_per_step):
        base = h * Q_PAD
        q_ref[h, :, :QK_NOPE] = (acc[:, base:base + QK_NOPE] * scale).astype(BF16)
        roped = _rope_low_lanes(acc[:, base + QK_NOPE:base + Q_PAD], cos, sin) * scale
        q_ref[h, :, QK_NOPE:] = roped[:, :QK_ROPE].astype(BF16)


def _q_up(cq, w_uq_p, tables, heads, scale, tm_pref=1024, hps_pref=4):
    m, k = cq.shape
    tm = _tile(m, tm_pref)
    hps = _tile(heads, hps_pref)
    return pl.pallas_call(
        functools.partial(_q_up_kernel, heads_per_step=hps, scale=scale),
        out_shape=jax.ShapeDtypeStruct((heads, m, QK_DIM), BF16),
        grid=(m // tm, heads // hps),
        in_specs=[pl.BlockSpec((tm, k), lambda i, j: (i, 0)),
                  pl.BlockSpec((k, hps * Q_PAD), lambda i, j: (0, j)),
                  pl.BlockSpec((tm, LANES), lambda i, j: (i, 0)),
                  pl.BlockSpec((tm, LANES), lambda i, j: (i, 0))],
        out_specs=pl.BlockSpec((hps, tm, QK_DIM), lambda i, j: (j, i, 0)),
        compiler_params=_params("parallel", "arbitrary"),
    )(cq, w_uq_p, *tables)


def _kv_up_kernel(a_ref, w_ref, kr_ref, k_ref, v_ref, *, heads_per_step):
    acc = jnp.dot(a_ref[...], w_ref[...], preferred_element_type=F32)
    kr = kr_ref[:, :QK_ROPE]
    for h in range(heads_per_step):
        base = h * (QK_NOPE + V_DIM)
        k_ref[h, :, :QK_NOPE] = acc[:, base:base + QK_NOPE].astype(BF16)
        k_ref[h, :, QK_NOPE:] = kr
        v_ref[h] = acc[:, base + QK_NOPE:base + QK_NOPE + V_DIM].astype(BF16)


def _kv_up(ckv, w_ukv, kr, heads, tm_pref=1024, hps_pref=4):
    m, k = ckv.shape
    tm = _tile(m, tm_pref)
    hps = _tile(heads, hps_pref)
    width = QK_NOPE + V_DIM
    return pl.pallas_call(
        functools.partial(_kv_up_kernel, heads_per_step=hps),
        out_shape=(jax.ShapeDtypeStruct((heads, m, QK_DIM), BF16),
                   jax.ShapeDtypeStruct((heads, m, V_DIM), BF16)),
        grid=(m // tm, heads // hps),
        in_specs=[pl.BlockSpec((tm, k), lambda i, j: (i, 0)),
                  pl.BlockSpec((k, hps * width), lambda i, j: (0, j)),
                  pl.BlockSpec((tm, LANES), lambda i, j: (i, 0))],
        out_specs=(pl.BlockSpec((hps, tm, QK_DIM), lambda i, j: (j, i, 0)),
                   pl.BlockSpec((hps, tm, V_DIM), lambda i, j: (j, i, 0))),
        compiler_params=_params("parallel", "arbitrary"),
    )(ckv, w_ukv, kr)


def _mla_attn_kernel(q_ref, k_ref, v_ref, o_ref, m_ref, acc_ref, s0_ref, s1_ref, *, blk, nsub):
    qi = pl.program_id(1)
    m_ref[...] = jnp.full(m_ref.shape, NEG, F32)
    acc_ref[...] = jnp.zeros(acc_ref.shape, F32)
    ones = jnp.ones((blk, V_DIM), BF16)
    slots = (s0_ref, s1_ref)

    def scores(sub, j, slot):
        start = pl.multiple_of(j * blk, blk)
        kk = k_ref[0, pl.ds(start, blk), :]
        q = q_ref[0, sub * blk:(sub + 1) * blk, :]
        slot[sub] = lax.dot_general(q, kk, (((1,), (1,)), ((), ())),
                                    preferred_element_type=F32)

    def softmax_pv(sub, j, slot, masked):
        start = pl.multiple_of(j * blk, blk)
        vv = jnp.concatenate([v_ref[0, pl.ds(start, blk), :], ones], axis=1)
        s = slot[sub]
        if masked:
            row = lax.broadcasted_iota(jnp.int32, s.shape, 0)
            col = lax.broadcasted_iota(jnp.int32, s.shape, 1)
            s = jnp.where(col <= row, s, NEG)
        chunks = [s[:, c * LANES:(c + 1) * LANES] for c in range(blk // LANES)]
        mx = functools.reduce(jnp.maximum, chunks)
        m_prev = m_ref[sub]
        m_new = jnp.maximum(m_prev, jnp.max(mx, axis=-1, keepdims=True))
        corr = jnp.exp(m_prev - m_new)
        p = jnp.concatenate([jnp.exp(c - m_new) for c in chunks], axis=1).astype(BF16)
        pv = jnp.dot(p, vv, preferred_element_type=F32)
        acc_ref[sub] = jnp.concatenate([corr, corr], axis=1) * acc_ref[sub] + pv
        m_ref[sub] = m_new

    for sub in range(nsub):
        scores(sub, 0, slots[0])

    def body(t, carry):
        for c in range(nsub):
            j = t * nsub + c
            for sub in range(nsub):
                scores(sub, j + 1, slots[(c + 1) % 2])
                softmax_pv(sub, j, slots[c % 2], masked=False)
        return carry

    lax.fori_loop(0, qi, body, 0)
    base = qi * nsub
    for c in range(nsub):
        for sub in range(c, nsub):
            if sub > c:
                scores(sub, base + c + 1, slots[(c + 1) % 2])
            softmax_pv(sub, base + c, slots[c % 2], masked=(sub == c))
    for sub in range(nsub):
        acc = acc_ref[sub]
        o_ref[sub * blk:(sub + 1) * blk, :] = (acc[:, :V_DIM] / acc[:, V_DIM:]).astype(o_ref.dtype)


def _mla_attn(q, k, v, blk_pref=512, nsub_pref=2):
    heads, s, _ = q.shape
    blk = _tile(s, blk_pref)
    nsub = _tile(s // blk, nsub_pref)
    assert nsub % 2 == 0, "score slots alternate with key-block parity"
    tq = blk * nsub
    return pl.pallas_call(
        functools.partial(_mla_attn_kernel, blk=blk, nsub=nsub),
        out_shape=jax.ShapeDtypeStruct((s, heads * V_DIM), BF16),
        grid=(heads, s // tq),
        in_specs=[pl.BlockSpec((1, tq, QK_DIM), lambda h, i: (h, i, 0)),
                  pl.BlockSpec((1, s, QK_DIM), lambda h, i: (h, 0, 0)),
                  pl.BlockSpec((1, s, V_DIM), lambda h, i: (h, 0, 0))],
        out_specs=pl.BlockSpec((tq, V_DIM), lambda h, i: (i, h)),
        scratch_shapes=[pltpu.VMEM((nsub, blk, LANES), F32),
                        pltpu.VMEM((nsub, blk, 2 * V_DIM), F32),
                        pltpu.VMEM((nsub, blk, blk), F32),
                        pltpu.VMEM((nsub, blk, blk), F32)],
        compiler_params=_params("parallel", "arbitrary"),
    )(q, k, v)


def _ffn_block(x, xb, w_gate, w_up, w_down, g, b, alpha, want_bf16):
    d, hidden = w_gate.shape
    hp = _round_up(hidden, 512)
    wg = jnp.pad(w_gate, ((0, 0), (0, hp - hidden))).astype(BF16)
    wu = jnp.pad(w_up, ((0, 0), (0, hp - hidden))).astype(BF16)
    wd = jnp.pad(w_down, ((0, hp - hidden), (0, 0))).astype(BF16)
    hmid = _ffn_up(xb, wg, wu)
    return _mm_res_ln(hmid, wd, x, g, b, alpha, want_bf16=want_bf16)


def _attn_conv_layer(x, xb, w_in, conv_w, w_out, g, b, alpha):
    s, d = x.shape
    a_width = d // 2
    w_in_b = w_in.astype(BF16)
    tables = _rope_tables(s, HEAD_DIM)
    qk = _matmul(xb, w_in_b, 0, 2 * a_width, BF16, rope_tables=tables,
                 scale=HEAD_DIM ** -0.5, scaled_cols=a_width)
    v = _matmul(xb, w_in_b, 2 * a_width, a_width, BF16)
    gates = _matmul(xb, w_in_b, 3 * a_width, 3 * (d - a_width), F32)
    outs, lses = [], []
    for window, dilation in DILATED_PATTERNS:
        assert window // dilation == DIL_BLOCK and s % window == 0
        o, lse = _dilated_branch(qk, v, dilation)
        outs.append(o)
        lses.append(lse)
    ab = _mix(outs, lses, gates, conv_w, a_width)
    return _mm_res_ln(ab, w_out.astype(BF16), x, g, b, alpha, want_bf16=True)


def _mla_layer(x, xb, w_in, q_norm, kv_norm, w_uq, w_ukv, w_out, g, b, alpha):
    s, d = x.shape
    heads = d // 128
    w_in_p = jnp.pad(w_in, ((0, 0), (0, LANES - QK_ROPE))).astype(BF16)
    cq, ckv, kr = _mla_in(xb, w_in_p, q_norm, kv_norm, _rope_tables(s, QK_ROPE))
    w_uq_p = jnp.pad(w_uq.reshape(Q_LORA, heads, QK_DIM),
                     ((0, 0), (0, 0), (0, Q_PAD - QK_DIM))).reshape(Q_LORA, heads * Q_PAD)
    q = _q_up(cq, w_uq_p.astype(BF16), _rope_tables(s, QK_ROPE), heads, QK_DIM ** -0.5)
    k, v = _kv_up(ckv, w_ukv.astype(BF16), kr, heads)
    o = _mla_attn(q, k, v)
    return _mm_res_ln(o, w_out.astype(BF16), x, g, b, alpha, want_bf16=True)


def kernel(x, w_in_a, conv_w, w_out_a, w_in_c, q_norm, kv_norm, w_uq, w_ukv, w_out_c,
           ln1_g, ln1_b, w_gate, w_up, w_down, ln2_g, ln2_b):
    batch, s, d = x.shape
    depth = ln1_g.shape[0]
    alpha = (2.0 * depth) ** 0.25
    outs = []
    for bi in range(batch):
        xf = x[bi]
        xb = xf.astype(BF16)
        for l in range(depth):
            j = l // 2
            if l % 2 == 0:
                xf, xb = _attn_conv_layer(xf, xb, w_in_a[j], conv_w[j], w_out_a[j],
                                          ln1_g[l], ln1_b[l], alpha)
            else:
                xf, xb = _mla_layer(xf, xb, w_in_c[j], q_norm[j], kv_norm[j], w_uq[j],
                                    w_ukv[j], w_out_c[j], ln1_g[l], ln1_b[l], alpha)
            xf, xb = _ffn_block(xf, xb, w_gate[l], w_up[l], w_down[l], ln2_g[l], ln2_b[l],
                                alpha, want_bf16=l + 1 < depth)
        outs.append(xf)
    return jnp.stack(outs)
```

```python
import functools
import math

import jax
import jax.numpy as jnp
from jax import lax
from jax.experimental import pallas as pl
from jax.experimental.pallas import tpu as pltpu

F32 = jnp.float32
BF16 = jnp.bfloat16

HEAD_DIM = 128
CONV_WIDTH = 3
DILATED_PATTERNS = ((128, 1), (512, 4), (2048, 16))
DIL_BLOCK = 128
Q_LORA = 1536
KV_LORA = 512
QK_NOPE = 128
QK_ROPE = 64
V_DIM = 128
ROPE_THETA = 10000.0
LN_EPS = 1e-5
RMS_EPS = 1e-6
NEG = -1e30

LANES = 128
SUBLANES = 8
VMEM_LIMIT_BYTES = 60 * 1024 * 1024


def _round_up(n, m):
    return -(-n // m) * m


def _tile(n, pref):
    if n <= pref:
        return n
    t = pref
    while n % t:
        t //= 2
    return t


def _params(*sem):
    return pltpu.CompilerParams(dimension_semantics=sem, vmem_limit_bytes=VMEM_LIMIT_BYTES)


def _rope_tables(seq, dim):
    half = dim // 2
    inv = ROPE_THETA ** (-jnp.arange(half, dtype=F32) * 2.0 / dim)
    ang = jnp.arange(seq).astype(F32)[:, None] * inv[None, :]
    cos, sin = jnp.cos(ang), jnp.sin(ang)
    pad = jnp.zeros((seq, LANES - dim), F32)
    return (jnp.concatenate([cos, cos, pad], axis=1),
            jnp.concatenate([-sin, sin, pad], axis=1))


def _rope_full_lanes(t, cos, sin):
    return t * cos + pltpu.roll(t, HEAD_DIM // 2, axis=1) * sin


def _rope_low_lanes(y, cos, sin):
    half = QK_ROPE // 2
    lane = lax.broadcasted_iota(jnp.int32, y.shape, 1)
    partner = jnp.where(lane < half, pltpu.roll(y, LANES - half, axis=1),
                        pltpu.roll(y, half, axis=1))
    return y * cos + partner * sin


def _mm_kernel(a_ref, w_ref, *rest, rope, scale, n_scaled_tiles):
    if rope:
        cos_ref, sin_ref, o_ref = rest
    else:
        (o_ref,) = rest
    acc = jnp.dot(a_ref[...], w_ref[...], preferred_element_type=F32)
    if not rope:
        o_ref[...] = acc.astype(o_ref.dtype)
        return
    cos = cos_ref[...]
    sin = sin_ref[...]
    mult = jnp.where(pl.program_id(1) < n_scaled_tiles, scale, 1.0).astype(F32)
    for c in range(acc.shape[1] // HEAD_DIM):
        sl = slice(c * HEAD_DIM, (c + 1) * HEAD_DIM)
        o_ref[:, sl] = (_rope_full_lanes(acc[:, sl], cos, sin) * mult).astype(o_ref.dtype)


def _matmul(a, w, col0, ncols, out_dtype, *, rope_tables=None, scale=1.0, scaled_cols=0,
            tm_pref=1024, tn_pref=512):
    m, k = a.shape
    tm = _tile(m, tm_pref)
    tn = _tile(math.gcd(ncols, col0, scaled_cols), tn_pref)
    off = col0 // tn
    in_specs = [pl.BlockSpec((tm, k), lambda i, j: (i, 0)),
                pl.BlockSpec((k, tn), lambda i, j: (0, j + off))]
    args = [a, w]
    if rope_tables is not None:
        in_specs += [pl.BlockSpec((tm, LANES), lambda i, j: (i, 0))] * 2
        args += list(rope_tables)
    return pl.pallas_call(
        functools.partial(_mm_kernel, rope=rope_tables is not None, scale=scale,
                          n_scaled_tiles=scaled_cols // tn),
        out_shape=jax.ShapeDtypeStruct((m, ncols), out_dtype),
        grid=(m // tm, ncols // tn),
        in_specs=in_specs,
        out_specs=pl.BlockSpec((tm, tn), lambda i, j: (i, j)),
        compiler_params=_params("parallel", "arbitrary"),
    )(*args)


def _dilated_kernel(q_ref, kp_ref, kc_ref, vp_ref, vc_ref, o_ref, lse_ref, *, heads):
    n = pl.program_id(1)
    blk = DIL_BLOCK
    qi = lax.broadcasted_iota(jnp.int32, (blk, 2 * blk), 0)
    ki = lax.broadcasted_iota(jnp.int32, (blk, 2 * blk), 1)
    dist = qi + blk - ki
    valid = (dist >= 0) & (dist <= blk) & ((ki >= blk) | (n > 0))
    lane = lax.broadcasted_iota(jnp.int32, (blk, LANES), 1)
    lse = jnp.zeros((blk, LANES), F32)
    for h in range(heads):
        sl = slice(h * HEAD_DIM, (h + 1) * HEAD_DIM)
        q = q_ref[:, sl]
        kk = jnp.concatenate([kp_ref[:, sl], kc_ref[:, sl]], axis=0)
        vv = jnp.concatenate([vp_ref[:, sl], vc_ref[:, sl]], axis=0)
        s = lax.dot_general(q, kk, (((1,), (1,)), ((), ())), preferred_element_type=F32)
        s = jnp.where(valid, s, NEG)
        m = jnp.max(s, axis=-1, keepdims=True)
        p = jnp.exp(s - m)
        den = jnp.sum(p, axis=-1, keepdims=True)
        o = jnp.dot(p.astype(BF16), vv, preferred_element_type=F32)
        o_ref[:, sl] = o / den
        lse = jnp.where(lane == h, m + jnp.log(den), lse)
    lse_ref[...] = lse


def _dilated_branch(qk, v, dilation):
    s, a2 = qk.shape
    a = a2 // 2
    heads = a // HEAD_DIM
    d = dilation
    rows = s // d
    nb = rows // DIL_BLOCK
    qk_r = qk.reshape(rows, d * a2)
    v_r = v.reshape(rows, d * a)
    blk = DIL_BLOCK
    cur = lambda r, n: n
    prev = lambda r, n: jnp.maximum(n - 1, 0)
    o, lse = pl.pallas_call(
        functools.partial(_dilated_kernel, heads=heads),
        out_shape=(jax.ShapeDtypeStruct((rows, d * a), F32),
                   jax.ShapeDtypeStruct((rows, d * LANES), F32)),
        grid=(d, nb),
        in_specs=[
            pl.BlockSpec((blk, a), lambda r, n: (cur(r, n), 2 * r)),
            pl.BlockSpec((blk, a), lambda r, n: (prev(r, n), 2 * r + 1)),
            pl.BlockSpec((blk, a), lambda r, n: (cur(r, n), 2 * r + 1)),
            pl.BlockSpec((blk, a), lambda r, n: (prev(r, n), r)),
            pl.BlockSpec((blk, a), lambda r, n: (cur(r, n), r)),
        ],
        out_specs=(pl.BlockSpec((blk, a), lambda r, n: (n, r)),
                   pl.BlockSpec((blk, LANES), lambda r, n: (n, r))),
        compiler_params=_params("parallel", "arbitrary"),
    )(qk_r, qk_r, qk_r, v_r, v_r)
    return o.reshape(s, a), lse.reshape(s, LANES)


def _mix_kernel(o1_ref, o2_ref, o3_ref, l1_ref, l2_ref, l3_ref,
                gb_ref, gc_ref, hin_ref, gch_ref, hinh_ref, cw_ref, out_ref, *, heads, a_width):
    i = pl.program_id(0)
    l1, l2, l3 = l1_ref[...], l2_ref[...], l3_ref[...]
    m = jnp.maximum(jnp.maximum(l1, l2), l3)
    e1, e2, e3 = jnp.exp(l1 - m), jnp.exp(l2 - m), jnp.exp(l3 - m)
    inv = 1.0 / (e1 + e2 + e3)
    w1, w2, w3 = e1 * inv, e2 * inv, e3 * inv
    for h in range(heads):
        sl = slice(h * HEAD_DIM, (h + 1) * HEAD_DIM)
        hs = slice(h, h + 1)
        mixed = (w1[:, hs] * o1_ref[:, sl] + w2[:, hs] * o2_ref[:, sl]
                 + w3[:, hs] * o3_ref[:, sl])
        out_ref[:, sl] = mixed.astype(out_ref.dtype)

    u = gc_ref[...] * hin_ref[...]
    halo = gch_ref[...] * hinh_ref[...]
    halo = halo * jnp.where(i > 0, 1.0, 0.0).astype(F32)
    row = lax.broadcasted_iota(jnp.int32, u.shape, 0)
    hm1 = halo[SUBLANES - 1:SUBLANES, :]
    hm2 = halo[SUBLANES - 2:SUBLANES - 1, :]
    u1 = jnp.where(row == 0, hm1, pltpu.roll(u, 1, axis=0))
    u2 = jnp.where(row == 0, hm2, jnp.where(row == 1, hm1, pltpu.roll(u, 2, axis=0)))
    cw = cw_ref[...]
    y = cw[0:1, :] * u2 + cw[1:2, :] * u1 + cw[2:3, :] * u
    out_ref[:, a_width:] = (gb_ref[...] * y).astype(out_ref.dtype)


def _mix(o_list, lse_list, gates, conv_w, a_width, tm_pref=256):
    s = gates.shape[0]
    b_width = gates.shape[1] // 3
    assert b_width == a_width
    heads = a_width // HEAD_DIM
    tm = _tile(s, tm_pref)
    hb = tm // SUBLANES
    row_blk = lambda c: pl.BlockSpec((tm, a_width), lambda i: (i, c))
    halo_blk = lambda c: pl.BlockSpec((SUBLANES, b_width),
                                      lambda i: (jnp.maximum(i * hb - 1, 0), c))
    lse_blk = pl.BlockSpec((tm, LANES), lambda i: (i, 0))
    return pl.pallas_call(
        functools.partial(_mix_kernel, heads=heads, a_width=a_width),
        out_shape=jax.ShapeDtypeStruct((s, a_width + b_width), BF16),
        grid=(s // tm,),
        in_specs=[row_blk(0)] * 3 + [lse_blk] * 3
        + [row_blk(0), row_blk(1), row_blk(2), halo_blk(1), halo_blk(2),
           pl.BlockSpec((CONV_WIDTH, b_width), lambda i: (0, 0))],
        out_specs=pl.BlockSpec((tm, a_width + b_width), lambda i: (i, 0)),
        compiler_params=_params("parallel"),
    )(*o_list, *lse_list, gates, gates, gates, gates, gates, conv_w)


LN_COL_CHUNK = 1024
LN_ROW_CHUNK = 8


def _mm_res_ln_kernel(a_ref, w_ref, x_ref, g_ref, b_ref, o_ref, *maybe_ob, alpha):
    k = pl.program_id(1)
    tm, n = o_ref.shape

    @pl.when(k == 0)
    def _():
        o_ref[...] = jnp.zeros(o_ref.shape, F32)

    a = a_ref[...]
    nc = _tile(n, LN_COL_CHUNK)
    for c in range(n // nc):
        sl = slice(c * nc, (c + 1) * nc)
        o_ref[:, sl] += jnp.dot(a, w_ref[:, sl], preferred_element_type=F32)

    @pl.when(k == pl.num_programs(1) - 1)
    def _():
        rc = _tile(tm, LN_ROW_CHUNK)

        for r in range(tm // rc):
            rows = slice(r * rc, (r + 1) * rc)
            z = alpha * x_ref[rows, :] + o_ref[rows, :]
            mu = jnp.mean(z, axis=-1, keepdims=True)
            zc = z - mu
            var = jnp.mean(zc * zc, axis=-1, keepdims=True)
            y = zc * lax.rsqrt(var + LN_EPS) * g_ref[...] + b_ref[...]
            o_ref[rows, :] = y
            if maybe_ob:
                maybe_ob[0][rows, :] = y.astype(BF16)


def _mm_res_ln(a, w, xres, g, b, alpha, *, want_bf16, tm_pref=512, tk_pref=512):
    m, k = a.shape
    n = w.shape[1]
    tm = _tile(m, tm_pref)
    tk = _tile(k, tk_pref)
    out_shape = [jax.ShapeDtypeStruct((m, n), F32)]
    out_specs = [pl.BlockSpec((tm, n), lambda i, kk: (i, 0))]
    if want_bf16:
        out_shape.append(jax.ShapeDtypeStruct((m, n), BF16))
        out_specs.append(pl.BlockSpec((tm, n), lambda i, kk: (i, 0)))
    res = pl.pallas_call(
        functools.partial(_mm_res_ln_kernel, alpha=alpha),
        out_shape=tuple(out_shape),
        grid=(m // tm, k // tk),
        in_specs=[
            pl.BlockSpec((tm, tk), lambda i, kk: (i, kk)),
            pl.BlockSpec((tk, n), lambda i, kk: (kk, 0)),
            pl.BlockSpec((tm, n), lambda i, kk: (i, 0)),
            pl.BlockSpec((1, n), lambda i, kk: (0, 0)),
            pl.BlockSpec((1, n), lambda i, kk: (0, 0)),
        ],
        out_specs=tuple(out_specs),
        compiler_params=_params("parallel", "arbitrary"),
    )(a, w, xres, g.reshape(1, n), b.reshape(1, n))
    return res if want_bf16 else (res[0], None)


def _ffn_up_kernel(a_ref, wg_ref, wu_ref, o_ref):
    a = a_ref[...]
    g = jnp.dot(a, wg_ref[...], preferred_element_type=F32)
    u = jnp.dot(a, wu_ref[...], preferred_element_type=F32)
    o_ref[...] = (g * jax.nn.sigmoid(g) * u).astype(o_ref.dtype)


def _ffn_up(a, wg, wu, tm_pref=1024, tn_pref=512):
    m, k = a.shape
    n = wg.shape[1]
    tm = _tile(m, tm_pref)
    tn = _tile(n, tn_pref)
    return pl.pallas_call(
        _ffn_up_kernel,
        out_shape=jax.ShapeDtypeStruct((m, n), BF16),
        grid=(m // tm, n // tn),
        in_specs=[pl.BlockSpec((tm, k), lambda i, j: (i, 0)),
                  pl.BlockSpec((k, tn), lambda i, j: (0, j)),
                  pl.BlockSpec((k, tn), lambda i, j: (0, j))],
        out_specs=pl.BlockSpec((tm, tn), lambda i, j: (i, j)),
        compiler_params=_params("parallel", "arbitrary"),
    )(a, wg, wu)


def _mla_in_kernel(a_ref, w_ref, qn_ref, kvn_ref, cos_ref, sin_ref,
                   cq_ref, ckv_ref, kr_ref, acc_ref):
    k = pl.program_id(1)
    part = jnp.dot(a_ref[...], w_ref[...], preferred_element_type=F32)

    @pl.when(k == 0)
    def _():
        acc_ref[...] = part

    @pl.when(k > 0)
    def _():
        acc_ref[...] += part

    @pl.when(k == pl.num_programs(1) - 1)
    def _():
        def rms(t, gain):
            r = lax.rsqrt(jnp.mean(t * t, axis=-1, keepdims=True) + RMS_EPS)
            return t * r * gain

        cq_ref[...] = rms(acc_ref[:, :Q_LORA], qn_ref[...]).astype(BF16)
        ckv_ref[...] = rms(acc_ref[:, Q_LORA:Q_LORA + KV_LORA], kvn_ref[...]).astype(BF16)
        y = acc_ref[:, Q_LORA + KV_LORA:]
        kr_ref[...] = _rope_low_lanes(y, cos_ref[...], sin_ref[...]).astype(BF16)


def _mla_in(a, w, q_norm, kv_norm, tables, tm_pref=1024, tk_pref=1024):
    m, k = a.shape
    n = w.shape[1]
    tm = _tile(m, tm_pref)
    tk = _tile(k, tk_pref)
    row = lambda width: pl.BlockSpec((tm, width), lambda i, kk: (i, 0))
    return pl.pallas_call(
        _mla_in_kernel,
        out_shape=(jax.ShapeDtypeStruct((m, Q_LORA), BF16),
                   jax.ShapeDtypeStruct((m, KV_LORA), BF16),
                   jax.ShapeDtypeStruct((m, LANES), BF16)),
        grid=(m // tm, k // tk),
        in_specs=[pl.BlockSpec((tm, tk), lambda i, kk: (i, kk)),
                  pl.BlockSpec((tk, n), lambda i, kk: (kk, 0)),
                  pl.BlockSpec((1, Q_LORA), lambda i, kk: (0, 0)),
                  pl.BlockSpec((1, KV_LORA), lambda i, kk: (0, 0)),
                  row(LANES), row(LANES)],
        out_specs=(row(Q_LORA), row(KV_LORA), row(LANES)),
        scratch_shapes=[pltpu.VMEM((tm, n), F32)],
        compiler_params=_params("parallel", "arbitrary"),
    )(a, w, q_norm.reshape(1, Q_LORA), kv_norm.reshape(1, KV_LORA), *tables)


QK_DIM = QK_NOPE + QK_ROPE
Q_PAD = 2 * LANES


def _q_up_kernel(a_ref, w_ref, cos_ref, sin_ref, q_ref, *, heads_per_step, scale):
    acc = jnp.dot(a_ref[...], w_ref[...], preferred_element_type=F32)
    cos, sin = cos_ref[...], sin_ref[...]
    for h in range(heads_per_step):
        base = h * Q_PAD
        q_ref[h, :, :QK_NOPE] = (acc[:, base:base + QK_NOPE] * scale).astype(BF16)
        roped = _rope_low_lanes(acc[:, base + QK_NOPE:base + Q_PAD], cos, sin) * scale
        q_ref[h, :, QK_NOPE:] = roped[:, :QK_ROPE].astype(BF16)


def _q_up(cq, w_uq_p, tables, heads, scale, tm_pref=1024, hps_pref=4):
    m, k = cq.shape
    tm = _tile(m, tm_pref)
    hps = _tile(heads, hps_pref)
    return pl.pallas_call(
        functools.partial(_q_up_kernel, heads_per_step=hps, scale=scale),
        out_shape=jax.ShapeDtypeStruct((heads, m, QK_DIM), BF16),
        grid=(m // tm, heads // hps),
        in_specs=[pl.BlockSpec((tm, k), lambda i, j: (i, 0)),
                  pl.BlockSpec((k, hps * Q_PAD), lambda i, j: (0, j)),
                  pl.BlockSpec((tm, LANES), lambda i, j: (i, 0)),
                  pl.BlockSpec((tm, LANES), lambda i, j: (i, 0))],
        out_specs=pl.BlockSpec((hps, tm, QK_DIM), lambda i, j: (j, i, 0)),
        compiler_params=_params("parallel", "arbitrary"),
    )(cq, w_uq_p, *tables)


def _kv_up_kernel(a_ref, w_ref, kr_ref, k_ref, v_ref, *, heads_per_step):
    acc = jnp.dot(a_ref[...], w_ref[...], preferred_element_type=F32)
    kr = kr_ref[:, :QK_ROPE]
    for h in range(heads_per_step):
        base = h * (QK_NOPE + V_DIM)
        k_ref[h, :, :QK_NOPE] = acc[:, base:base + QK_NOPE].astype(BF16)
        k_ref[h, :, QK_NOPE:] = kr
        v_ref[h] = acc[:, base + QK_NOPE:base + QK_NOPE + V_DIM].astype(BF16)


def _kv_up(ckv, w_ukv, kr, heads, tm_pref=1024, hps_pref=4):
    m, k = ckv.shape
    tm = _tile(m, tm_pref)
    hps = _tile(heads, hps_pref)
    width = QK_NOPE + V_DIM
    return pl.pallas_call(
        functools.partial(_kv_up_kernel, heads_per_step=hps),
        out_shape=(jax.ShapeDtypeStruct((heads, m, QK_DIM), BF16),
                   jax.ShapeDtypeStruct((heads, m, V_DIM), BF16)),
        grid=(m // tm, heads // hps),
        in_specs=[pl.BlockSpec((tm, k), lambda i, j: (i, 0)),
                  pl.BlockSpec((k, hps * width), lambda i, j: (0, j)),
                  pl.BlockSpec((tm, LANES), lambda i, j: (i, 0))],
        out_specs=(pl.BlockSpec((hps, tm, QK_DIM), lambda i, j: (j, i, 0)),
                   pl.BlockSpec((hps, tm, V_DIM), lambda i, j: (j, i, 0))),
        compiler_params=_params("parallel", "arbitrary"),
    )(ckv, w_ukv, kr)


def _mla_attn_kernel(q_ref, k_ref, v_ref, o_ref, m_ref, acc_ref, s0_ref, s1_ref, *, blk, nsub):
    qi = pl.program_id(1)
    m_ref[...] = jnp.full(m_ref.shape, NEG, F32)
    acc_ref[...] = jnp.zeros(acc_ref.shape, F32)
    ones = jnp.ones((blk, V_DIM), BF16)
    slots = (s0_ref, s1_ref)

    def scores(sub, j, slot):
        start = pl.multiple_of(j * blk, blk)
        kk = k_ref[0, pl.ds(start, blk), :]
        q = q_ref[0, sub * blk:(sub + 1) * blk, :]
        slot[sub] = lax.dot_general(q, kk, (((1,), (1,)), ((), ())),
                                    preferred_element_type=F32)

    def softmax_pv(sub, j, slot, masked):
        start = pl.multiple_of(j * blk, blk)
        vv = jnp.concatenate([v_ref[0, pl.ds(start, blk), :], ones], axis=1)
        s = slot[sub]
        if masked:
            row = lax.broadcasted_iota(jnp.int32, s.shape, 0)
            col = lax.broadcasted_iota(jnp.int32, s.shape, 1)
            s = jnp.where(col <= row, s, NEG)
        chunks = [s[:, c * LANES:(c + 1) * LANES] for c in range(blk // LANES)]
        mx = functools.reduce(jnp.maximum, chunks)
        m_prev = m_ref[sub]
        m_new = jnp.maximum(m_prev, jnp.max(mx, axis=-1, keepdims=True))
        corr = jnp.exp(m_prev - m_new)
        p = jnp.concatenate([jnp.exp(c - m_new) for c in chunks], axis=1).astype(BF16)
        pv = jnp.dot(p, vv, preferred_element_type=F32)
        acc_ref[sub] = jnp.concatenate([corr, corr], axis=1) * acc_ref[sub] + pv
        m_ref[sub] = m_new

    for sub in range(nsub):
        scores(sub, 0, slots[0])

    def body(t, carry):
        for c in range(nsub):
            j = t * nsub + c
            for sub in range(nsub):
                scores(sub, j + 1, slots[(c + 1) % 2])
                softmax_pv(sub, j, slots[c % 2], masked=False)
        return carry

    lax.fori_loop(0, qi, body, 0)
    base = qi * nsub
    for c in range(nsub):
        for sub in range(c, nsub):
            if sub > c:
                scores(sub, base + c + 1, slots[(c + 1) % 2])
            softmax_pv(sub, base + c, slots[c % 2], masked=(sub == c))
    for sub in range(nsub):
        acc = acc_ref[sub]
        o_ref[sub * blk:(sub + 1) * blk, :] = (acc[:, :V_DIM] / acc[:, V_DIM:]).astype(o_ref.dtype)


def _mla_attn(q, k, v, blk_pref=512, nsub_pref=2):
    heads, s, _ = q.shape
    blk = _tile(s, blk_pref)
    nsub = _tile(s // blk, nsub_pref)
    assert nsub % 2 == 0, "score slots alternate with key-block parity"
    tq = blk * nsub
    return pl.pallas_call(
        functools.partial(_mla_attn_kernel, blk=blk, nsub=nsub),
        out_shape=jax.ShapeDtypeStruct((s, heads * V_DIM), BF16),
        grid=(heads, s // tq),
        in_specs=[pl.BlockSpec((1, tq, QK_DIM), lambda h, i: (h, i, 0)),
                  pl.BlockSpec((1, s, QK_DIM), lambda h, i: (h, 0, 0)),
                  pl.BlockSpec((1, s, V_DIM), lambda h, i: (h, 0, 0))],
        out_specs=pl.BlockSpec((tq, V_DIM), lambda h, i: (i, h)),
        scratch_shapes=[pltpu.VMEM((nsub, blk, LANES), F32),
                        pltpu.VMEM((nsub, blk, 2 * V_DIM), F32),
                        pltpu.VMEM((nsub, blk, blk), F32),
                        pltpu.VMEM((nsub, blk, blk), F32)],
        compiler_params=_params("parallel", "arbitrary"),
    )(q, k, v)


def _ffn_block(x, xb, w_gate, w_up, w_down, g, b, alpha, want_bf16):
    d, hidden = w_gate.shape
    hp = _round_up(hidden, 512)
    wg = jnp.pad(w_gate.astype(BF16), ((0, 0), (0, hp - hidden)))
    wu = jnp.pad(w_up.astype(BF16), ((0, 0), (0, hp - hidden)))
    wd = jnp.pad(w_down.astype(BF16), ((0, hp - hidden), (0, 0)))
    hmid = _ffn_up(xb, wg, wu)
    return _mm_res_ln(hmid, wd, x, g, b, alpha, want_bf16=want_bf16)


def _attn_conv_layer(x, xb, w_in, conv_w, w_out, g, b, alpha):
    s, d = x.shape
    a_width = d // 2
    w_in_b = w_in.astype(BF16)
    tables = _rope_tables(s, HEAD_DIM)
    qk = _matmul(xb, w_in_b, 0, 2 * a_width, BF16, rope_tables=tables,
                 scale=HEAD_DIM ** -0.5, scaled_cols=a_width)
    v = _matmul(xb, w_in_b, 2 * a_width, a_width, BF16)
    gates = _matmul(xb, w_in_b, 3 * a_width, 3 * (d - a_width), F32)
    outs, lses = [], []
    for window, dilation in DILATED_PATTERNS:
        assert window // dilation == DIL_BLOCK and s % window == 0
        o, lse = _dilated_branch(qk, v, dilation)
        outs.append(o)
        lses.append(lse)
    ab = _mix(outs, lses, gates, conv_w, a_width)
    return _mm_res_ln(ab, w_out.astype(BF16), x, g, b, alpha, want_bf16=True)


def _mla_layer(x, xb, w_in, q_norm, kv_norm, w_uq, w_ukv, w_out, g, b, alpha):
    s, d = x.shape
    heads = d // 128
    w_in_p = jnp.pad(w_in.astype(BF16), ((0, 0), (0, LANES - QK_ROPE)))
    cq, ckv, kr = _mla_in(xb, w_in_p, q_norm, kv_norm, _rope_tables(s, QK_ROPE))
    w_uq_p = jnp.pad(w_uq.astype(BF16).reshape(Q_LORA, heads, QK_DIM),
                     ((0, 0), (0, 0), (0, Q_PAD - QK_DIM))).reshape(Q_LORA, heads * Q_PAD)
    q = _q_up(cq, w_uq_p, _rope_tables(s, QK_ROPE), heads, QK_DIM ** -0.5)
    k, v = _kv_up(ckv, w_ukv.astype(BF16), kr, heads)
    o = _mla_attn(q, k, v)
    return _mm_res_ln(o, w_out.astype(BF16), x, g, b, alpha, want_bf16=True)


def kernel(x, w_in_a, conv_w, w_out_a, w_in_c, q_norm, kv_norm, w_uq, w_ukv, w_out_c,
           ln1_g, ln1_b, w_gate, w_up, w_down, ln2_g, ln2_b):
    batch, s, d = x.shape
    depth = ln1_g.shape[0]
    alpha = (2.0 * depth) ** 0.25
    outs = []
    for bi in range(batch):
        xf = x[bi]
        xb = xf.astype(BF16)
        for l in range(depth):
            j = l // 2
            if l % 2 == 0:
                xf, xb = _attn_conv_layer(xf, xb, w_in_a[j], conv_w[j], w_out_a[j],
                                          ln1_g[l], ln1_b[l], alpha)
            else:
                xf, xb = _mla_layer(xf, xb, w_in_c[j], q_norm[j], kv_norm[j], w_uq[j],
                                    w_ukv[j], w_out_c[j], ln1_g[l], ln1_b[l], alpha)
            xf, xb = _ffn_block(xf, xb, w_gate[l], w_up[l], w_down[l], ln2_g[l], ln2_b[l],
                                alpha, want_bf16=l + 1 < depth)
        outs.append(xf)
    return jnp.stack(outs)
```

```python
import functools
import math

import jax
import jax.numpy as jnp
from jax import lax
from jax.experimental import pallas as pl
from jax.experimental.pallas import tpu as pltpu

F32 = jnp.float32
BF16 = jnp.bfloat16

HEAD_DIM = 128
CONV_WIDTH = 3
DILATED_PATTERNS = ((128, 1), (512, 4), (2048, 16))
DIL_BLOCK = 128
Q_LORA = 1536
KV_LORA = 512
QK_NOPE = 128
QK_ROPE = 64
V_DIM = 128
ROPE_THETA = 10000.0
LN_EPS = 1e-5
RMS_EPS = 1e-6
NEG = -1e30

LANES = 128
SUBLANES = 8
VMEM_LIMIT_BYTES = 60 * 1024 * 1024


def _round_up(n, m):
    return -(-n // m) * m


def _tile(n, pref):
    if n <= pref:
        return n
    t = pref
    while n % t:
        t //= 2
    return t


def _params(*sem):
    return pltpu.CompilerParams(dimension_semantics=sem, vmem_limit_bytes=VMEM_LIMIT_BYTES)


def _cast_pad_kernel(w_ref, o_ref, *, rows, cols, rows_p):
    tr, cols_p = o_ref.shape
    val = w_ref[...].astype(BF16)
    if rows_p > rows:
        row = pl.program_id(0) * tr + lax.broadcasted_iota(jnp.int32, val.shape, 0)
        val = jnp.where(row < rows, val, jnp.zeros_like(val))
    o_ref[:, :cols] = val
    if cols_p > cols:
        o_ref[:, cols:] = jnp.zeros((tr, cols_p - cols), BF16)


def _cast_pad(w, rows_p=None, cols_p=None, tr_pref=256):
    rows, cols = w.shape
    rows_p = rows_p or rows
    cols_p = cols_p or cols
    tr = _tile(rows_p, tr_pref)
    last = (rows - 1) // tr
    return pl.pallas_call(
        functools.partial(_cast_pad_kernel, rows=rows, cols=cols, rows_p=rows_p),
        out_shape=jax.ShapeDtypeStruct((rows_p, cols_p), BF16),
        grid=(rows_p // tr,),
        in_specs=[pl.BlockSpec((tr, cols), lambda i: (jnp.minimum(i, last), 0))],
        out_specs=pl.BlockSpec((tr, cols_p), lambda i: (i, 0)),
        compiler_params=_params("parallel"),
    )(w)


def _rope_tables(seq, dim):
    half = dim // 2
    inv = ROPE_THETA ** (-jnp.arange(half, dtype=F32) * 2.0 / dim)
    ang = jnp.arange(seq).astype(F32)[:, None] * inv[None, :]
    cos, sin = jnp.cos(ang), jnp.sin(ang)
    pad = jnp.zeros((seq, LANES - dim), F32)
    return (jnp.concatenate([cos, cos, pad], axis=1),
            jnp.concatenate([-sin, sin, pad], axis=1))


def _rope_full_lanes(t, cos, sin):
    return t * cos + pltpu.roll(t, HEAD_DIM // 2, axis=1) * sin


def _rope_low_lanes(y, cos, sin):
    half = QK_ROPE // 2
    lane = lax.broadcasted_iota(jnp.int32, y.shape, 1)
    partner = jnp.where(lane < half, pltpu.roll(y, LANES - half, axis=1),
                        pltpu.roll(y, half, axis=1))
    return y * cos + partner * sin


def _in_proj_kernel(a_ref, w_ref, cos_ref, sin_ref, o_ref, *, scale, n_scaled_tiles,
                    n_rope_tiles):
    j = pl.program_id(1)
    acc = jnp.dot(a_ref[...], w_ref[...], preferred_element_type=F32)

    @pl.when(j >= n_rope_tiles)
    def _():
        o_ref[...] = acc

    @pl.when(j < n_rope_tiles)
    def _():
        cos = cos_ref[...]
        sin = sin_ref[...]
        mult = jnp.where(j < n_scaled_tiles, scale, 1.0).astype(F32)
        for c in range(acc.shape[1] // HEAD_DIM):
            sl = slice(c * HEAD_DIM, (c + 1) * HEAD_DIM)
            o_ref[:, sl] = _rope_full_lanes(acc[:, sl], cos, sin) * mult


def _in_proj(a, w, rope_tables, *, rope_cols, scale, scaled_cols, tm_pref=1024, tn_pref=512):
    m, k = a.shape
    n = w.shape[1]
    tm = _tile(m, tm_pref)
    tn = _tile(math.gcd(n, rope_cols, scaled_cols), tn_pref)
    return pl.pallas_call(
        functools.partial(_in_proj_kernel, scale=scale, n_scaled_tiles=scaled_cols // tn,
                          n_rope_tiles=rope_cols // tn),
        out_shape=jax.ShapeDtypeStruct((m, n), F32),
        grid=(m // tm, n // tn),
        in_specs=[pl.BlockSpec((tm, k), lambda i, j: (i, 0)),
                  pl.BlockSpec((k, tn), lambda i, j: (0, j)),
                  pl.BlockSpec((tm, LANES), lambda i, j: (i, 0)),
                  pl.BlockSpec((tm, LANES), lambda i, j: (i, 0))],
        out_specs=pl.BlockSpec((tm, tn), lambda i, j: (i, j)),
        compiler_params=_params("parallel", "arbitrary"),
    )(a, w, *rope_tables)


DIL_ROWS = 2048


def _dilated_kernel(q_ref, kp_ref, kc_ref, vp_ref, vc_ref, o_ref, lse_ref, *, d, groups):
    n = pl.program_id(0)
    h = pl.program_id(1)
    blk = DIL_BLOCK
    span = blk * d
    qi = lax.broadcasted_iota(jnp.int32, (blk, 2 * blk), 0)
    ki = lax.broadcasted_iota(jnp.int32, (blk, 2 * blk), 1)
    dist = qi + blk - ki
    band = (dist >= 0) & (dist <= blk)
    band_first = band & ((ki >= blk) | (n > 0))
    lane = lax.broadcasted_iota(jnp.int32, (blk, LANES), 1)

    @pl.when(h == 0)
    def _():
        lse_ref[...] = jnp.zeros(lse_ref.shape, F32)

    def sel(start):
        return pl.ds(start, blk, stride=d) if d > 1 else pl.ds(start, blk)

    for g in range(groups):
        for r in range(d):
            cur = sel(g * span + r)
            q = q_ref[cur, :].astype(BF16)
            if g == 0:
                k_prev, v_prev = kp_ref[sel(r), :], vp_ref[sel(r), :]
            else:
                prev = sel((g - 1) * span + r)
                k_prev, v_prev = kc_ref[prev, :], vc_ref[prev, :]
            kk = jnp.concatenate([k_prev, kc_ref[cur, :]], axis=0).astype(BF16)
            vv = jnp.concatenate([v_prev, vc_ref[cur, :]], axis=0).astype(BF16)
            s = lax.dot_general(q, kk, (((1,), (1,)), ((), ())), preferred_element_type=F32)
            s = jnp.where(band_first if g == 0 else band, s, NEG)
            m = jnp.max(s, axis=-1, keepdims=True)
            p = jnp.exp(s - m)
            den = jnp.sum(p, axis=-1, keepdims=True)
            o = jnp.dot(p.astype(BF16), vv, preferred_element_type=F32)
            o_ref[cur, :] = o / den
            lse_ref[cur, :] = jnp.where(lane == h, m + jnp.log(den), lse_ref[cur, :])


def _dilated_branch(hproj, a_width, dilation):
    s = hproj.shape[0]
    heads = a_width // HEAD_DIM
    d = dilation
    span = DIL_BLOCK * d
    rows = _tile(s, DIL_ROWS)
    groups = rows // span
    assert groups >= 1 and rows % span == 0
    cur = lambda col0: pl.BlockSpec((rows, HEAD_DIM), lambda n, h: (n, col0 + h))
    prev = lambda col0: pl.BlockSpec(
        (span, HEAD_DIM), lambda n, h: (jnp.maximum(n * groups - 1, 0), col0 + h))
    return pl.pallas_call(
        functools.partial(_dilated_kernel, d=d, groups=groups),
        out_shape=(jax.ShapeDtypeStruct((s, a_width), F32),
                   jax.ShapeDtypeStruct((s, LANES), F32)),
        grid=(s // rows, heads),
        in_specs=[cur(0), prev(heads), cur(heads), prev(2 * heads), cur(2 * heads)],
        out_specs=(pl.BlockSpec((rows, HEAD_DIM), lambda n, h: (n, h)),
                   pl.BlockSpec((rows, LANES), lambda n, h: (n, 0))),
        compiler_params=_params("parallel", "arbitrary"),
    )(hproj, hproj, hproj, hproj, hproj)


def _mix_kernel(o1_ref, o2_ref, o3_ref, l1_ref, l2_ref, l3_ref,
                gb_ref, gc_ref, hin_ref, gch_ref, hinh_ref, cw_ref, out_ref, *, heads, a_width):
    i = pl.program_id(0)
    l1, l2, l3 = l1_ref[...], l2_ref[...], l3_ref[...]
    m = jnp.maximum(jnp.maximum(l1, l2), l3)
    e1, e2, e3 = jnp.exp(l1 - m), jnp.exp(l2 - m), jnp.exp(l3 - m)
    inv = 1.0 / (e1 + e2 + e3)
    w1, w2, w3 = e1 * inv, e2 * inv, e3 * inv
    for h in range(heads):
        sl = slice(h * HEAD_DIM, (h + 1) * HEAD_DIM)
        hs = slice(h, h + 1)
        mixed = (w1[:, hs] * o1_ref[:, sl] + w2[:, hs] * o2_ref[:, sl]
                 + w3[:, hs] * o3_ref[:, sl])
        out_ref[:, sl] = mixed.astype(out_ref.dtype)

    u = gc_ref[...] * hin_ref[...]
    halo = gch_ref[...] * hinh_ref[...]
    halo = halo * jnp.where(i > 0, 1.0, 0.0).astype(F32)
    row = lax.broadcasted_iota(jnp.int32, u.shape, 0)
    hm1 = halo[SUBLANES - 1:SUBLANES, :]
    hm2 = halo[SUBLANES - 2:SUBLANES - 1, :]
    u1 = jnp.where(row == 0, hm1, pltpu.roll(u, 1, axis=0))
    u2 = jnp.where(row == 0, hm2, jnp.where(row == 1, hm1, pltpu.roll(u, 2, axis=0)))
    cw = cw_ref[...]
    y = cw[0:1, :] * u2 + cw[1:2, :] * u1 + cw[2:3, :] * u
    out_ref[:, a_width:] = (gb_ref[...] * y).astype(out_ref.dtype)


def _mix(o_list, lse_list, hproj, conv_w, a_width, tm_pref=256):
    s = hproj.shape[0]
    b_width = (hproj.shape[1] - 3 * a_width) // 3
    assert b_width == a_width, "gate blocks are addressed in units of the attention width"
    gb_blk, gc_blk, hin_blk = 3, 4, 5
    heads = a_width // HEAD_DIM
    tm = _tile(s, tm_pref)
    hb = tm // SUBLANES
    row_blk = lambda c: pl.BlockSpec((tm, a_width), lambda i: (i, c))
    halo_blk = lambda c: pl.BlockSpec((SUBLANES, b_width),
                                      lambda i: (jnp.maximum(i * hb - 1, 0), c))
    lse_blk = pl.BlockSpec((tm, LANES), lambda i: (i, 0))
    return pl.pallas_call(
        functools.partial(_mix_kernel, heads=heads, a_width=a_width),
        out_shape=jax.ShapeDtypeStruct((s, a_width + b_width), BF16),
        grid=(s // tm,),
        in_specs=[row_blk(0)] * 3 + [lse_blk] * 3
        + [row_blk(gb_blk), row_blk(gc_blk), row_blk(hin_blk), halo_blk(gc_blk),
           halo_blk(hin_blk), pl.BlockSpec((CONV_WIDTH, b_width), lambda i: (0, 0))],
        out_specs=pl.BlockSpec((tm, a_width + b_width), lambda i: (i, 0)),
        compiler_params=_params("parallel"),
    )(*o_list, *lse_list, hproj, hproj, hproj, hproj, hproj, conv_w)


LN_COL_CHUNK = 1024
LN_ROW_CHUNK = 8


def _mm_res_ln_kernel(a_ref, w_ref, x_ref, g_ref, b_ref, o_ref, *maybe_ob, alpha):
    k = pl.program_id(1)
    tm, n = o_ref.shape

    @pl.when(k == 0)
    def _():
        o_ref[...] = jnp.zeros(o_ref.shape, F32)

    a = a_ref[...]
    nc = _tile(n, LN_COL_CHUNK)
    for c in range(n // nc):
        sl = slice(c * nc, (c + 1) * nc)
        o_ref[:, sl] += jnp.dot(a, w_ref[:, sl], preferred_element_type=F32)

    @pl.when(k == pl.num_programs(1) - 1)
    def _():
        rc = _tile(tm, LN_ROW_CHUNK)

        for r in range(tm // rc):
            rows = slice(r * rc, (r + 1) * rc)
            z = alpha * x_ref[rows, :] + o_ref[rows, :]
            mu = jnp.mean(z, axis=-1, keepdims=True)
            zc = z - mu
            var = jnp.mean(zc * zc, axis=-1, keepdims=True)
            y = zc * lax.rsqrt(var + LN_EPS) * g_ref[...] + b_ref[...]
            o_ref[rows, :] = y
            if maybe_ob:
                maybe_ob[0][rows, :] = y.astype(BF16)


def _mm_res_ln(a, w, xres, g, b, alpha, *, want_bf16, tm_pref=512, tk_pref=512):
    m, k = a.shape
    n = w.shape[1]
    tm = _tile(m, tm_pref)
    tk = _tile(k, tk_pref)
    out_shape = [jax.ShapeDtypeStruct((m, n), F32)]
    out_specs = [pl.BlockSpec((tm, n), lambda i, kk: (i, 0))]
    if want_bf16:
        out_shape.append(jax.ShapeDtypeStruct((m, n), BF16))
        out_specs.append(pl.BlockSpec((tm, n), lambda i, kk: (i, 0)))
    res = pl.pallas_call(
        functools.partial(_mm_res_ln_kernel, alpha=alpha),
        out_shape=tuple(out_shape),
        grid=(m // tm, k // tk),
        in_specs=[
            pl.BlockSpec((tm, tk), lambda i, kk: (i, kk)),
            pl.BlockSpec((tk, n), lambda i, kk: (kk, 0)),
            pl.BlockSpec((tm, n), lambda i, kk: (i, 0)),
            pl.BlockSpec((1, n), lambda i, kk: (0, 0)),
            pl.BlockSpec((1, n), lambda i, kk: (0, 0)),
        ],
        out_specs=tuple(out_specs),
        compiler_params=_params("parallel", "arbitrary"),
    )(a, w, xres, g.reshape(1, n), b.reshape(1, n))
    return res if want_bf16 else (res[0], None)


def _ffn_up_kernel(a_ref, wg_ref, wu_ref, o_ref):
    a = a_ref[...]
    g = jnp.dot(a, wg_ref[...], preferred_element_type=F32)
    u = jnp.dot(a, wu_ref[...], preferred_element_type=F32)
    o_ref[...] = (g * jax.nn.sigmoid(g) * u).astype(o_ref.dtype)


def _ffn_up(a, wg, wu, tm_pref=1024, tn_pref=512):
    m, k = a.shape
    n = wg.shape[1]
    tm = _tile(m, tm_pref)
    tn = _tile(n, tn_pref)
    return pl.pallas_call(
        _ffn_up_kernel,
        out_shape=jax.ShapeDtypeStruct((m, n), BF16),
        grid=(m // tm, n // tn),
        in_specs=[pl.BlockSpec((tm, k), lambda i, j: (i, 0)),
                  pl.BlockSpec((k, tn), lambda i, j: (0, j)),
                  pl.BlockSpec((k, tn), lambda i, j: (0, j))],
        out_specs=pl.BlockSpec((tm, tn), lambda i, j: (i, j)),
        compiler_params=_params("parallel", "arbitrary"),
    )(a, wg, wu)


def _mla_in_kernel(a_ref, w_ref, qn_ref, kvn_ref, cos_ref, sin_ref,
                   cq_ref, ckv_ref, kr_ref, acc_ref):
    k = pl.program_id(1)
    part = jnp.dot(a_ref[...], w_ref[...], preferred_element_type=F32)

    @pl.when(k == 0)
    def _():
        acc_ref[...] = part

    @pl.when(k > 0)
    def _():
        acc_ref[...] += part

    @pl.when(k == pl.num_programs(1) - 1)
    def _():
        def rms(t, gain):
            r = lax.rsqrt(jnp.mean(t * t, axis=-1, keepdims=True) + RMS_EPS)
            return t * r * gain

        cq_ref[...] = rms(acc_ref[:, :Q_LORA], qn_ref[...]).astype(BF16)
        ckv_ref[...] = rms(acc_ref[:, Q_LORA:Q_LORA + KV_LORA], kvn_ref[...]).astype(BF16)
        y = acc_ref[:, Q_LORA + KV_LORA:]
        kr_ref[...] = _rope_low_lanes(y, cos_ref[...], sin_ref[...]).astype(BF16)


def _mla_in(a, w, q_norm, kv_norm, tables, tm_pref=1024, tk_pref=1024):
    m, k = a.shape
    n = w.shape[1]
    tm = _tile(m, tm_pref)
    tk = _tile(k, tk_pref)
    row = lambda width: pl.BlockSpec((tm, width), lambda i, kk: (i, 0))
    return pl.pallas_call(
        _mla_in_kernel,
        out_shape=(jax.ShapeDtypeStruct((m, Q_LORA), BF16),
                   jax.ShapeDtypeStruct((m, KV_LORA), BF16),
                   jax.ShapeDtypeStruct((m, LANES), BF16)),
        grid=(m // tm, k // tk),
        in_specs=[pl.BlockSpec((tm, tk), lambda i, kk: (i, kk)),
                  pl.BlockSpec((tk, n), lambda i, kk: (kk, 0)),
                  pl.BlockSpec((1, Q_LORA), lambda i, kk: (0, 0)),
                  pl.BlockSpec((1, KV_LORA), lambda i, kk: (0, 0)),
                  row(LANES), row(LANES)],
        out_specs=(row(Q_LORA), row(KV_LORA), row(LANES)),
        scratch_shapes=[pltpu.VMEM((tm, n), F32)],
        compiler_params=_params("parallel", "arbitrary"),
    )(a, w, q_norm.reshape(1, Q_LORA), kv_norm.reshape(1, KV_LORA), *tables)


QK_DIM = QK_NOPE + QK_ROPE
Q_PAD = 2 * LANES


def _q_up_kernel(a_ref, w_ref, cos_ref, sin_ref, q_ref, *, heads_per_step, scale):
    acc = jnp.dot(a_ref[...], w_ref[...], preferred_element_type=F32)
    cos, sin = cos_ref[...], sin_ref[...]
    for h in range(heads_per_step):
        base = h * Q_PAD
        q_ref[h, :, :QK_NOPE] = (acc[:, base:base + QK_NOPE] * scale).astype(BF16)
        roped = _rope_low_lanes(acc[:, base + QK_NOPE:base + Q_PAD], cos, sin) * scale
        q_ref[h, :, QK_NOPE:] = roped[:, :QK_ROPE].astype(BF16)


def _q_up(cq, w_uq_p, tables, heads, scale, tm_pref=1024, hps_pref=4):
    m, k = cq.shape
    tm = _tile(m, tm_pref)
    hps = _tile(heads, hps_pref)
    return pl.pallas_call(
        functools.partial(_q_up_kernel, heads_per_step=hps, scale=scale),
        out_shape=jax.ShapeDtypeStruct((heads, m, QK_DIM), BF16),
        grid=(m // tm, heads // hps),
        in_specs=[pl.BlockSpec((tm, k), lambda i, j: (i, 0)),
                  pl.BlockSpec((k, hps * Q_PAD), lambda i, j: (0, j)),
                  pl.BlockSpec((tm, LANES), lambda i, j: (i, 0)),
                  pl.BlockSpec((tm, LANES), lambda i, j: (i, 0))],
        out_specs=pl.BlockSpec((hps, tm, QK_DIM), lambda i, j: (j, i, 0)),
        compiler_params=_params("parallel", "arbitrary"),
    )(cq, w_uq_p, *tables)


def _kv_up_kernel(a_ref, w_ref, kr_ref, k_ref, v_ref, *, heads_per_step):
    acc = jnp.dot(a_ref[...], w_ref[...], preferred_element_type=F32)
    kr = kr_ref[:, :QK_ROPE]
    for h in range(heads_per_step):
        base = h * (QK_NOPE + V_DIM)
        k_ref[h, :, :QK_NOPE] = acc[:, base:base + QK_NOPE].astype(BF16)
        k_ref[h, :, QK_NOPE:] = kr
        v_ref[h] = acc[:, base + QK_NOPE:base + QK_NOPE + V_DIM].astype(BF16)


def _kv_up(ckv, w_ukv, kr, heads, tm_pref=1024, hps_pref=4):
    m, k = ckv.shape
    tm = _tile(m, tm_pref)
    hps = _tile(heads, hps_pref)
    width = QK_NOPE + V_DIM
    return pl.pallas_call(
        functools.partial(_kv_up_kernel, heads_per_step=hps),
        out_shape=(jax.ShapeDtypeStruct((heads, m, QK_DIM), BF16),
                   jax.ShapeDtypeStruct((heads, m, V_DIM), BF16)),
        grid=(m // tm, heads // hps),
        in_specs=[pl.BlockSpec((tm, k), lambda i, j: (i, 0)),
                  pl.BlockSpec((k, hps * width), lambda i, j: (0, j)),
                  pl.BlockSpec((tm, LANES), lambda i, j: (i, 0))],
        out_specs=(pl.BlockSpec((hps, tm, QK_DIM), lambda i, j: (j, i, 0)),
                   pl.BlockSpec((hps, tm, V_DIM), lambda i, j: (j, i, 0))),
        compiler_params=_params("parallel", "arbitrary"),
    )(ckv, w_ukv, kr)


def _mla_attn_kernel(q_ref, k_ref, v_ref, o_ref, m_ref, acc_ref, s0_ref, s1_ref, *, blk, nsub):
    qi = pl.program_id(1)
    m_ref[...] = jnp.full(m_ref.shape, NEG, F32)
    acc_ref[...] = jnp.zeros(acc_ref.shape, F32)
    ones = jnp.ones((blk, V_DIM), BF16)
    slots = (s0_ref, s1_ref)

    def scores(sub, j, slot):
        start = pl.multiple_of(j * blk, blk)
        kk = k_ref[0, pl.ds(start, blk), :]
        q = q_ref[0, sub * blk:(sub + 1) * blk, :]
        slot[sub] = lax.dot_general(q, kk, (((1,), (1,)), ((), ())),
                                    preferred_element_type=F32)

    def softmax_pv(sub, j, slot, masked):
        start = pl.multiple_of(j * blk, blk)
        vv = jnp.concatenate([v_ref[0, pl.ds(start, blk), :], ones], axis=1)
        s = slot[sub]
        if masked:
            row = lax.broadcasted_iota(jnp.int32, s.shape, 0)
            col = lax.broadcasted_iota(jnp.int32, s.shape, 1)
            s = jnp.where(col <= row, s, NEG)
        chunks = [s[:, c * LANES:(c + 1) * LANES] for c in range(blk // LANES)]
        mx = functools.reduce(jnp.maximum, chunks)
        m_prev = m_ref[sub]
        m_new = jnp.maximum(m_prev, jnp.max(mx, axis=-1, keepdims=True))
        corr = jnp.exp(m_prev - m_new)
        p = jnp.concatenate([jnp.exp(c - m_new) for c in chunks], axis=1).astype(BF16)
        pv = jnp.dot(p, vv, preferred_element_type=F32)
        acc_ref[sub] = jnp.concatenate([corr, corr], axis=1) * acc_ref[sub] + pv
        m_ref[sub] = m_new

    for sub in range(nsub):
        scores(sub, 0, slots[0])

    def full_blocks(t):
        for c in range(nsub):
            j = t * nsub + c
            for sub in range(nsub):
                scores(sub, j + 1, slots[(c + 1) % 2])
                softmax_pv(sub, j, slots[c % 2], masked=False)

    def body(u, carry):
        full_blocks(2 * u)
        full_blocks(2 * u + 1)
        return carry

    lax.fori_loop(0, qi // 2, body, 0)

    @pl.when(qi % 2 == 1)
    def _():
        full_blocks(qi - 1)

    base = qi * nsub
    for c in range(nsub):
        for sub in range(c, nsub):
            if sub > c:
                scores(sub, base + c + 1, slots[(c + 1) % 2])
            softmax_pv(sub, base + c, slots[c % 2], masked=(sub == c))
    for sub in range(nsub):
        acc = acc_ref[sub]
        o_ref[sub * blk:(sub + 1) * blk, :] = (acc[:, :V_DIM] / acc[:, V_DIM:]).astype(o_ref.dtype)


def _mla_attn(q, k, v, blk_pref=512, nsub_pref=2):
    heads, s, _ = q.shape
    blk = _tile(s, blk_pref)
    nsub = _tile(s // blk, nsub_pref)
    assert nsub % 2 == 0, "score slots alternate with key-block parity"
    tq = blk * nsub
    return pl.pallas_call(
        functools.partial(_mla_attn_kernel, blk=blk, nsub=nsub),
        out_shape=jax.ShapeDtypeStruct((s, heads * V_DIM), BF16),
        grid=(heads, s // tq),
        in_specs=[pl.BlockSpec((1, tq, QK_DIM), lambda h, i: (h, i, 0)),
                  pl.BlockSpec((1, s, QK_DIM), lambda h, i: (h, 0, 0)),
                  pl.BlockSpec((1, s, V_DIM), lambda h, i: (h, 0, 0))],
        out_specs=pl.BlockSpec((tq, V_DIM), lambda h, i: (i, h)),
        scratch_shapes=[pltpu.VMEM((nsub, blk, LANES), F32),
                        pltpu.VMEM((nsub, blk, 2 * V_DIM), F32),
                        pltpu.VMEM((nsub, blk, blk), F32),
                        pltpu.VMEM((nsub, blk, blk), F32)],
        compiler_params=_params("parallel", "arbitrary"),
    )(q, k, v)


def _ffn_block(x, xb, w_gate, w_up, w_down, g, b, alpha, want_bf16):
    d, hidden = w_gate.shape
    hp = _round_up(hidden, 512)
    wg = _cast_pad(w_gate, cols_p=hp)
    wu = _cast_pad(w_up, cols_p=hp)
    wd = _cast_pad(w_down, rows_p=hp)
    hmid = _ffn_up(xb, wg, wu)
    return _mm_res_ln(hmid, wd, x, g, b, alpha, want_bf16=want_bf16)


def _attn_conv_layer(x, xb, w_in, conv_w, w_out, g, b, alpha):
    s, d = x.shape
    a_width = d // 2
    hproj = _in_proj(xb, _cast_pad(w_in), _rope_tables(s, HEAD_DIM), rope_cols=2 * a_width,
                     scale=HEAD_DIM ** -0.5, scaled_cols=a_width)
    outs, lses = [], []
    for window, dilation in DILATED_PATTERNS:
        assert window // dilation == DIL_BLOCK and s % window == 0
        o, lse = _dilated_branch(hproj, a_width, dilation)
        outs.append(o)
        lses.append(lse)
    ab = _mix(outs, lses, hproj, conv_w, a_width)
    return _mm_res_ln(ab, _cast_pad(w_out), x, g, b, alpha, want_bf16=True)


def _mla_layer(x, xb, w_in, q_norm, kv_norm, w_uq, w_ukv, w_out, g, b, alpha):
    s, d = x.shape
    heads = d // 128
    w_in_p = _cast_pad(w_in, cols_p=w_in.shape[1] + LANES - QK_ROPE)
    cq, ckv, kr = _mla_in(xb, w_in_p, q_norm, kv_norm, _rope_tables(s, QK_ROPE))
    w_uq_p = jnp.pad(w_uq.astype(BF16).reshape(Q_LORA, heads, QK_DIM),
                     ((0, 0), (0, 0), (0, Q_PAD - QK_DIM))).reshape(Q_LORA, heads * Q_PAD)
    q = _q_up(cq, w_uq_p, _rope_tables(s, QK_ROPE), heads, QK_DIM ** -0.5)
    k, v = _kv_up(ckv, _cast_pad(w_ukv), kr, heads)
    o = _mla_attn(q, k, v)
    return _mm_res_ln(o, _cast_pad(w_out), x, g, b, alpha, want_bf16=True)


def kernel(x, w_in_a, conv_w, w_out_a, w_in_c, q_norm, kv_norm, w_uq, w_ukv, w_out_c,
           ln1_g, ln1_b, w_gate, w_up, w_down, ln2_g, ln2_b):
    batch, s, d = x.shape
    depth = ln1_g.shape[0]
    alpha = (2.0 * depth) ** 0.25
    outs = []
    for bi in range(batch):
        xf = x[bi]
        xb = xf.astype(BF16)
        for l in range(depth):
            j = l // 2
            if l % 2 == 0:
                xf, xb = _attn_conv_layer(xf, xb, w_in_a[j], conv_w[j], w_out_a[j],
                                          ln1_g[l], ln1_b[l], alpha)
            else:
                xf, xb = _mla_layer(xf, xb, w_in_c[j], q_norm[j], kv_norm[j], w_uq[j],
                                    w_ukv[j], w_out_c[j], ln1_g[l], ln1_b[l], alpha)
            xf, xb = _ffn_block(xf, xb, w_gate[l], w_up[l], w_down[l], ln2_g[l], ln2_b[l],
                                alpha, want_bf16=l + 1 < depth)
        outs.append(xf)
    return jnp.stack(outs)
```

```python
import functools
import math

import jax
import jax.numpy as jnp
from jax import lax
from jax.experimental import pallas as pl
from jax.experimental.pallas import tpu as pltpu

F32 = jnp.float32
BF16 = jnp.bfloat16

HEAD_DIM = 128
CONV_WIDTH = 3
DILATED_PATTERNS = ((128, 1), (512, 4), (2048, 16))
DIL_BLOCK = 128
Q_LORA = 1536
KV_LORA = 512
QK_NOPE = 128
QK_ROPE = 64
V_DIM = 128
ROPE_THETA = 10000.0
LN_EPS = 1e-5
RMS_EPS = 1e-6
NEG = -1e30

LANES = 128
SUBLANES = 8
VMEM_LIMIT_BYTES = 60 * 1024 * 1024


def _round_up(n, m):
    return -(-n // m) * m


def _tile(n, pref):
    if n <= pref:
        return n
    t = pref
    while n % t:
        t //= 2
    return t


def _params(*sem):
    return pltpu.CompilerParams(dimension_semantics=sem, vmem_limit_bytes=VMEM_LIMIT_BYTES)


def _cast_pad_kernel(w_ref, o_ref, *, rows, cols, rows_p):
    tr, cols_p = o_ref.shape
    val = w_ref[...].astype(BF16)
    if rows_p > rows:
        row = pl.program_id(0) * tr + lax.broadcasted_iota(jnp.int32, val.shape, 0)
        val = jnp.where(row < rows, val, jnp.zeros_like(val))
    o_ref[:, :cols] = val
    if cols_p > cols:
        o_ref[:, cols:] = jnp.zeros((tr, cols_p - cols), BF16)


def _cast_pad(w_stack, layer, rows_p=None, cols_p=None, tr_pref=256):
    _, rows, cols = w_stack.shape
    rows_p = rows_p or rows
    cols_p = cols_p or cols
    tr = _tile(rows_p, tr_pref)
    last = (rows - 1) // tr
    return pl.pallas_call(
        functools.partial(_cast_pad_kernel, rows=rows, cols=cols, rows_p=rows_p),
        out_shape=jax.ShapeDtypeStruct((rows_p, cols_p), BF16),
        grid=(rows_p // tr,),
        in_specs=[pl.BlockSpec((None, tr, cols), lambda i: (layer, jnp.minimum(i, last), 0))],
        out_specs=pl.BlockSpec((tr, cols_p), lambda i: (i, 0)),
        compiler_params=_params("parallel"),
    )(w_stack)


def _rope_tables(seq, dim):
    half = dim // 2
    inv = ROPE_THETA ** (-jnp.arange(half, dtype=F32) * 2.0 / dim)
    ang = jnp.arange(seq).astype(F32)[:, None] * inv[None, :]
    cos, sin = jnp.cos(ang), jnp.sin(ang)
    pad = jnp.zeros((seq, LANES - dim), F32)
    return (jnp.concatenate([cos, cos, pad], axis=1),
            jnp.concatenate([-sin, sin, pad], axis=1))


def _rope_full_lanes(t, cos, sin):
    return t * cos + pltpu.roll(t, HEAD_DIM // 2, axis=1) * sin


def _rope_low_lanes(y, cos, sin):
    half = QK_ROPE // 2
    lane = lax.broadcasted_iota(jnp.int32, y.shape, 1)
    partner = jnp.where(lane < half, pltpu.roll(y, LANES - half, axis=1),
                        pltpu.roll(y, half, axis=1))
    return y * cos + partner * sin


def _in_proj_kernel(a_ref, w_ref, cos_ref, sin_ref, o_ref, *, scale, n_scaled_tiles,
                    n_rope_tiles):
    j = pl.program_id(1)
    acc = jnp.dot(a_ref[...], w_ref[...], preferred_element_type=F32)

    @pl.when(j >= n_rope_tiles)
    def _():
        o_ref[...] = acc

    @pl.when(j < n_rope_tiles)
    def _():
        cos = cos_ref[...]
        sin = sin_ref[...]
        mult = jnp.where(j < n_scaled_tiles, scale, 1.0).astype(F32)
        for c in range(acc.shape[1] // HEAD_DIM):
            sl = slice(c * HEAD_DIM, (c + 1) * HEAD_DIM)
            o_ref[:, sl] = _rope_full_lanes(acc[:, sl], cos, sin) * mult


def _in_proj(a, w, rope_tables, *, rope_cols, scale, scaled_cols, tm_pref=1024, tn_pref=1024):
    m, k = a.shape
    n = w.shape[1]
    tm = _tile(m, tm_pref)
    tn = _tile(math.gcd(n, rope_cols, scaled_cols), tn_pref)
    return pl.pallas_call(
        functools.partial(_in_proj_kernel, scale=scale, n_scaled_tiles=scaled_cols // tn,
                          n_rope_tiles=rope_cols // tn),
        out_shape=jax.ShapeDtypeStruct((m, n), F32),
        grid=(m // tm, n // tn),
        in_specs=[pl.BlockSpec((tm, k), lambda i, j: (i, 0)),
                  pl.BlockSpec((k, tn), lambda i, j: (0, j)),
                  pl.BlockSpec((tm, LANES), lambda i, j: (i, 0)),
                  pl.BlockSpec((tm, LANES), lambda i, j: (i, 0))],
        out_specs=pl.BlockSpec((tm, tn), lambda i, j: (i, j)),
        compiler_params=_params("parallel", "arbitrary"),
    )(a, w, *rope_tables)


DIL_ROWS = 2048


def _dilated_kernel(q_ref, kp_ref, kc_ref, vp_ref, vc_ref, o_ref, lse_ref, *, d, groups):
    n = pl.program_id(0)
    h = pl.program_id(1)
    blk = DIL_BLOCK
    span = blk * d
    qi = lax.broadcasted_iota(jnp.int32, (blk, 2 * blk), 0)
    ki = lax.broadcasted_iota(jnp.int32, (blk, 2 * blk), 1)
    dist = qi + blk - ki
    band = (dist >= 0) & (dist <= blk)
    band_first = band & ((ki >= blk) | (n > 0))
    lane = lax.broadcasted_iota(jnp.int32, (blk, LANES), 1)

    @pl.when(h == 0)
    def _():
        lse_ref[...] = jnp.zeros(lse_ref.shape, F32)

    def sel(start):
        return pl.ds(start, blk, stride=d) if d > 1 else pl.ds(start, blk)

    for g in range(groups):
        for r in range(d):
            cur = sel(g * span + r)
            q = q_ref[cur, :].astype(BF16)
            if g == 0:
                k_prev, v_prev = kp_ref[sel(r), :], vp_ref[sel(r), :]
            else:
                prev = sel((g - 1) * span + r)
                k_prev, v_prev = kc_ref[prev, :], vc_ref[prev, :]
            kk = jnp.concatenate([k_prev, kc_ref[cur, :]], axis=0).astype(BF16)
            vv = jnp.concatenate([v_prev, vc_ref[cur, :]], axis=0).astype(BF16)
            s = lax.dot_general(q, kk, (((1,), (1,)), ((), ())), preferred_element_type=F32)
            s = jnp.where(band_first if g == 0 else band, s, NEG)
            m = jnp.max(s, axis=-1, keepdims=True)
            p = jnp.exp(s - m)
            den = jnp.sum(p, axis=-1, keepdims=True)
            o = jnp.dot(p.astype(BF16), vv, preferred_element_type=F32)
            o_ref[cur, :] = o / den
            lse_ref[cur, :] = jnp.where(lane == h, m + jnp.log(den), lse_ref[cur, :])


def _dilated_branch(hproj, a_width, dilation):
    s = hproj.shape[0]
    heads = a_width // HEAD_DIM
    d = dilation
    span = DIL_BLOCK * d
    rows = _tile(s, DIL_ROWS)
    groups = rows // span
    assert groups >= 1 and rows % span == 0
    cur = lambda col0: pl.BlockSpec((rows, HEAD_DIM), lambda n, h: (n, col0 + h))
    prev = lambda col0: pl.BlockSpec(
        (span, HEAD_DIM), lambda n, h: (jnp.maximum(n * groups - 1, 0), col0 + h))
    return pl.pallas_call(
        functools.partial(_dilated_kernel, d=d, groups=groups),
        out_shape=(jax.ShapeDtypeStruct((s, a_width), F32),
                   jax.ShapeDtypeStruct((s, LANES), F32)),
        grid=(s // rows, heads),
        in_specs=[cur(0), prev(heads), cur(heads), prev(2 * heads), cur(2 * heads)],
        out_specs=(pl.BlockSpec((rows, HEAD_DIM), lambda n, h: (n, h)),
                   pl.BlockSpec((rows, LANES), lambda n, h: (n, 0))),
        compiler_params=_params("parallel", "arbitrary"),
    )(hproj, hproj, hproj, hproj, hproj)


def _mix_kernel(o1_ref, o2_ref, o3_ref, l1_ref, l2_ref, l3_ref,
                gb_ref, gc_ref, hin_ref, gch_ref, hinh_ref, cw_ref, out_ref, *, heads, a_width):
    i = pl.program_id(0)
    l1, l2, l3 = l1_ref[...], l2_ref[...], l3_ref[...]
    m = jnp.maximum(jnp.maximum(l1, l2), l3)
    e1, e2, e3 = jnp.exp(l1 - m), jnp.exp(l2 - m), jnp.exp(l3 - m)
    inv = 1.0 / (e1 + e2 + e3)
    w1, w2, w3 = e1 * inv, e2 * inv, e3 * inv
    for h in range(heads):
        sl = slice(h * HEAD_DIM, (h + 1) * HEAD_DIM)
        hs = slice(h, h + 1)
        mixed = (w1[:, hs] * o1_ref[:, sl] + w2[:, hs] * o2_ref[:, sl]
                 + w3[:, hs] * o3_ref[:, sl])
        out_ref[:, sl] = mixed.astype(out_ref.dtype)

    u = gc_ref[...] * hin_ref[...]
    halo = gch_ref[...] * hinh_ref[...]
    halo = halo * jnp.where(i > 0, 1.0, 0.0).astype(F32)
    row = lax.broadcasted_iota(jnp.int32, u.shape, 0)
    hm1 = halo[SUBLANES - 1:SUBLANES, :]
    hm2 = halo[SUBLANES - 2:SUBLANES - 1, :]
    u1 = jnp.where(row == 0, hm1, pltpu.roll(u, 1, axis=0))
    u2 = jnp.where(row == 0, hm2, jnp.where(row == 1, hm1, pltpu.roll(u, 2, axis=0)))
    cw = cw_ref[...]
    y = cw[0:1, :] * u2 + cw[1:2, :] * u1 + cw[2:3, :] * u
    out_ref[:, a_width:] = (gb_ref[...] * y).astype(out_ref.dtype)


def _mix(o_list, lse_list, hproj, conv_w, a_width, tm_pref=256):
    s = hproj.shape[0]
    b_width = (hproj.shape[1] - 3 * a_width) // 3
    assert b_width == a_width, "gate blocks are addressed in units of the attention width"
    gb_blk, gc_blk, hin_blk = 3, 4, 5
    heads = a_width // HEAD_DIM
    tm = _tile(s, tm_pref)
    hb = tm // SUBLANES
    row_blk = lambda c: pl.BlockSpec((tm, a_width), lambda i: (i, c))
    halo_blk = lambda c: pl.BlockSpec((SUBLANES, b_width),
                                      lambda i: (jnp.maximum(i * hb - 1, 0), c))
    lse_blk = pl.BlockSpec((tm, LANES), lambda i: (i, 0))
    return pl.pallas_call(
        functools.partial(_mix_kernel, heads=heads, a_width=a_width),
        out_shape=jax.ShapeDtypeStruct((s, a_width + b_width), BF16),
        grid=(s // tm,),
        in_specs=[row_blk(0)] * 3 + [lse_blk] * 3
        + [row_blk(gb_blk), row_blk(gc_blk), row_blk(hin_blk), halo_blk(gc_blk),
           halo_blk(hin_blk), pl.BlockSpec((CONV_WIDTH, b_width), lambda i: (0, 0))],
        out_specs=pl.BlockSpec((tm, a_width + b_width), lambda i: (i, 0)),
        compiler_params=_params("parallel"),
    )(*o_list, *lse_list, hproj, hproj, hproj, hproj, hproj, conv_w)


LN_COL_CHUNK = 1024
LN_ROW_CHUNK = 8


def _residual_ln(x_ref, g_ref, b_ref, o_ref, ob_ref, alpha):
    tm = o_ref.shape[0]
    rc = _tile(tm, LN_ROW_CHUNK)
    for r in range(tm // rc):
        rows = slice(r * rc, (r + 1) * rc)
        z = alpha * x_ref[rows, :] + o_ref[rows, :]
        mu = jnp.mean(z, axis=-1, keepdims=True)
        zc = z - mu
        var = jnp.mean(zc * zc, axis=-1, keepdims=True)
        y = zc * lax.rsqrt(var + LN_EPS) * g_ref[...] + b_ref[...]
        o_ref[rows, :] = y
        if ob_ref is not None:
            ob_ref[rows, :] = y.astype(BF16)


def _mm_res_ln_kernel(a_ref, w_ref, x_ref, g_ref, b_ref, o_ref, *maybe_ob, alpha):
    k = pl.program_id(1)
    n = o_ref.shape[1]

    @pl.when(k == 0)
    def _():
        o_ref[...] = jnp.zeros(o_ref.shape, F32)

    a = a_ref[...]
    nc = _tile(n, LN_COL_CHUNK)
    for c in range(n // nc):
        sl = slice(c * nc, (c + 1) * nc)
        o_ref[:, sl] += jnp.dot(a, w_ref[:, sl], preferred_element_type=F32)

    @pl.when(k == pl.num_programs(1) - 1)
    def _():
        _residual_ln(x_ref, g_ref, b_ref, o_ref, maybe_ob[0] if maybe_ob else None, alpha)


def _mm_res_ln(a, w, xres, g, b, alpha, *, want_bf16, tm_pref=512, tk_pref=512):
    m, k = a.shape
    n = w.shape[1]
    tm = _tile(m, tm_pref)
    tk = _tile(k, tk_pref)
    out_shape = [jax.ShapeDtypeStruct((m, n), F32)]
    out_specs = [pl.BlockSpec((tm, n), lambda i, kk: (i, 0))]
    if want_bf16:
        out_shape.append(jax.ShapeDtypeStruct((m, n), BF16))
        out_specs.append(pl.BlockSpec((tm, n), lambda i, kk: (i, 0)))
    res = pl.pallas_call(
        functools.partial(_mm_res_ln_kernel, alpha=alpha),
        out_shape=tuple(out_shape),
        grid=(m // tm, k // tk),
        in_specs=[
            pl.BlockSpec((tm, tk), lambda i, kk: (i, kk)),
            pl.BlockSpec((tk, n), lambda i, kk: (kk, 0)),
            pl.BlockSpec((tm, n), lambda i, kk: (i, 0)),
            pl.BlockSpec((1, n), lambda i, kk: (0, 0)),
            pl.BlockSpec((1, n), lambda i, kk: (0, 0)),
        ],
        out_specs=tuple(out_specs),
        compiler_params=_params("parallel", "arbitrary"),
    )(a, w, xres, g.reshape(1, n), b.reshape(1, n))
    return res if want_bf16 else (res[0], None)


def _proj_res_ln_kernel(a_ref, w_ref, x_ref, g_ref, b_ref, o_ref, ob_ref, *, alpha):
    n = o_ref.shape[1]
    a = a_ref[...]
    nc = _tile(n, LN_COL_CHUNK)
    for c in range(n // nc):
        sl = slice(c * nc, (c + 1) * nc)
        o_ref[:, sl] = jnp.dot(a, w_ref[:, sl], preferred_element_type=F32)
    _residual_ln(x_ref, g_ref, b_ref, o_ref, ob_ref, alpha)


def _proj_res_ln(a, w, xres, g, b, alpha, tm_pref=128):
    m, k = a.shape
    n = w.shape[1]
    tm = _tile(m, tm_pref)
    row = lambda width: pl.BlockSpec((tm, width), lambda i: (i, 0))
    vec = pl.BlockSpec((1, n), lambda i: (0, 0))
    return pl.pallas_call(
        functools.partial(_proj_res_ln_kernel, alpha=alpha),
        out_shape=(jax.ShapeDtypeStruct((m, n), F32), jax.ShapeDtypeStruct((m, n), BF16)),
        grid=(m // tm,),
        in_specs=[row(k),
                  pl.BlockSpec((k, n), lambda i: (0, 0), pipeline_mode=pl.Buffered(1)),
                  row(n), vec, vec],
        out_specs=(row(n), row(n)),
        compiler_params=_params("arbitrary"),
    )(a, w, xres, g.reshape(1, n), b.reshape(1, n))


def _ffn_up_kernel(a_ref, wg_ref, wu_ref, o_ref):
    a = a_ref[...]
    g = jnp.dot(a, wg_ref[...], preferred_element_type=F32)
    u = jnp.dot(a, wu_ref[...], preferred_element_type=F32)
    o_ref[...] = (g * jax.nn.sigmoid(g) * u).astype(o_ref.dtype)


def _ffn_up(a, wg, wu, tm_pref=1024, tn_pref=512):
    m, k = a.shape
    n = wg.shape[1]
    tm = _tile(m, tm_pref)
    tn = _tile(n, tn_pref)
    return pl.pallas_call(
        _ffn_up_kernel,
        out_shape=jax.ShapeDtypeStruct((m, n), BF16),
        grid=(m // tm, n // tn),
        in_specs=[pl.BlockSpec((tm, k), lambda i, j: (i, 0)),
                  pl.BlockSpec((k, tn), lambda i, j: (0, j)),
                  pl.BlockSpec((k, tn), lambda i, j: (0, j))],
        out_specs=pl.BlockSpec((tm, tn), lambda i, j: (i, j)),
        compiler_params=_params("parallel", "arbitrary"),
    )(a, wg, wu)


def _mla_in_kernel(a_ref, w_ref, qn_ref, kvn_ref, cos_ref, sin_ref,
                   cq_ref, ckv_ref, kr_ref, acc_ref):
    k = pl.program_id(1)
    part = jnp.dot(a_ref[...], w_ref[...], preferred_element_type=F32)

    @pl.when(k == 0)
    def _():
        acc_ref[...] = part

    @pl.when(k > 0)
    def _():
        acc_ref[...] += part

    @pl.when(k == pl.num_programs(1) - 1)
    def _():
        def rms(t, gain):
            r = lax.rsqrt(jnp.mean(t * t, axis=-1, keepdims=True) + RMS_EPS)
            return t * r * gain

        cq_ref[...] = rms(acc_ref[:, :Q_LORA], qn_ref[...]).astype(BF16)
        ckv_ref[...] = rms(acc_ref[:, Q_LORA:Q_LORA + KV_LORA], kvn_ref[...]).astype(BF16)
        y = acc_ref[:, Q_LORA + KV_LORA:]
        kr_ref[...] = _rope_low_lanes(y, cos_ref[...], sin_ref[...]).astype(BF16)


def _mla_in(a, w, q_norm, kv_norm, tables, tm_pref=1024, tk_pref=1024):
    m, k = a.shape
    n = w.shape[1]
    tm = _tile(m, tm_pref)
    tk = _tile(k, tk_pref)
    row = lambda width: pl.BlockSpec((tm, width), lambda i, kk: (i, 0))
    return pl.pallas_call(
        _mla_in_kernel,
        out_shape=(jax.ShapeDtypeStruct((m, Q_LORA), BF16),
                   jax.ShapeDtypeStruct((m, KV_LORA), BF16),
                   jax.ShapeDtypeStruct((m, LANES), BF16)),
        grid=(m // tm, k // tk),
        in_specs=[pl.BlockSpec((tm, tk), lambda i, kk: (i, kk)),
                  pl.BlockSpec((tk, n), lambda i, kk: (kk, 0)),
                  pl.BlockSpec((1, Q_LORA), lambda i, kk: (0, 0)),
                  pl.BlockSpec((1, KV_LORA), lambda i, kk: (0, 0)),
                  row(LANES), row(LANES)],
        out_specs=(row(Q_LORA), row(KV_LORA), row(LANES)),
        scratch_shapes=[pltpu.VMEM((tm, n), F32)],
        compiler_params=_params("parallel", "arbitrary"),
    )(a, w, q_norm.reshape(1, Q_LORA), kv_norm.reshape(1, KV_LORA), *tables)


QK_DIM = QK_NOPE + QK_ROPE
Q_PAD = 2 * LANES


def _q_up_kernel(a_ref, w_ref, cos_ref, sin_ref, q_ref, *, heads_per_step, scale):
    acc = jnp.dot(a_ref[...], w_ref[...], preferred_element_type=F32)
    cos, sin = cos_ref[...], sin_ref[...]
    for h in range(heads_per_step):
        base = h * Q_PAD
        q_ref[h, :, :QK_NOPE] = (acc[:, base:base + QK_NOPE] * scale).astype(BF16)
        roped = _rope_low_lanes(acc[:, base + QK_NOPE:base + Q_PAD], cos, sin) * scale
        q_ref[h, :, QK_NOPE:] = roped[:, :QK_ROPE].astype(BF16)


def _q_up(cq, w_uq_p, tables, heads, scale, tm_pref=1024, hps_pref=4):
    m, k = cq.shape
    tm = _tile(m, tm_pref)
    hps = _tile(heads, hps_pref)
    return pl.pallas_call(
        functools.partial(_q_up_kernel, heads_per_step=hps, scale=scale),
        out_shape=jax.ShapeDtypeStruct((heads, m, QK_DIM), BF16),
        grid=(m // tm, heads // hps),
        in_specs=[pl.BlockSpec((tm, k), lambda i, j: (i, 0)),
                  pl.BlockSpec((k, hps * Q_PAD), lambda i, j: (0, j)),
                  pl.BlockSpec((tm, LANES), lambda i, j: (i, 0)),
                  pl.BlockSpec((tm, LANES), lambda i, j: (i, 0))],
        out_specs=pl.BlockSpec((hps, tm, QK_DIM), lambda i, j: (j, i, 0)),
        compiler_params=_params("parallel", "arbitrary"),
    )(cq, w_uq_p, *tables)


def _kv_up_kernel(a_ref, w_ref, kr_ref, k_ref, v_ref, *, heads_per_step):
    acc = jnp.dot(a_ref[...], w_ref[...], preferred_element_type=F32)
    kr = kr_ref[:, :QK_ROPE]
    for h in range(heads_per_step):
        base = h * (QK_NOPE + V_DIM)
        k_ref[h, :, :QK_NOPE] = acc[:, base:base + QK_NOPE].astype(BF16)
        k_ref[h, :, QK_NOPE:] = kr
        v_ref[h] = acc[:, base + QK_NOPE:base + QK_NOPE + V_DIM].astype(BF16)


def _kv_up(ckv, w_ukv, kr, heads, tm_pref=1024, hps_pref=4):
    m, k = ckv.shape
    tm = _tile(m, tm_pref)
    hps = _tile(heads, hps_pref)
    width = QK_NOPE + V_DIM
    return pl.pallas_call(
        functools.partial(_kv_up_kernel, heads_per_step=hps),
        out_shape=(jax.ShapeDtypeStruct((heads, m, QK_DIM), BF16),
                   jax.ShapeDtypeStruct((heads, m, V_DIM), BF16)),
        grid=(m // tm, heads // hps),
        in_specs=[pl.BlockSpec((tm, k), lambda i, j: (i, 0)),
                  pl.BlockSpec((k, hps * width), lambda i, j: (0, j)),
                  pl.BlockSpec((tm, LANES), lambda i, j: (i, 0))],
        out_specs=(pl.BlockSpec((hps, tm, QK_DIM), lambda i, j: (j, i, 0)),
                   pl.BlockSpec((hps, tm, V_DIM), lambda i, j: (j, i, 0))),
        compiler_params=_params("parallel", "arbitrary"),
    )(ckv, w_ukv, kr)


def _mla_attn_kernel(q_ref, k_ref, v_ref, o_ref, m_ref, acc_ref, s0_ref, s1_ref, *, blk, nsub):
    qi = pl.program_id(1)
    m_ref[...] = jnp.full(m_ref.shape, NEG, F32)
    acc_ref[...] = jnp.zeros(acc_ref.shape, F32)
    ones = jnp.ones((blk, V_DIM), BF16)
    slots = (s0_ref, s1_ref)

    def scores(sub, j, slot):
        start = pl.multiple_of(j * blk, blk)
        kk = k_ref[0, pl.ds(start, blk), :]
        q = q_ref[0, sub * blk:(sub + 1) * blk, :]
        slot[sub] = lax.dot_general(q, kk, (((1,), (1,)), ((), ())),
                                    preferred_element_type=F32)

    def softmax_pv(sub, j, slot, masked):
        start = pl.multiple_of(j * blk, blk)
        vv = jnp.concatenate([v_ref[0, pl.ds(start, blk), :], ones], axis=1)
        s = slot[sub]
        if masked:
            row = lax.broadcasted_iota(jnp.int32, s.shape, 0)
            col = lax.broadcasted_iota(jnp.int32, s.shape, 1)
            s = jnp.where(col <= row, s, NEG)
        chunks = [s[:, c * LANES:(c + 1) * LANES] for c in range(blk // LANES)]
        mx = functools.reduce(jnp.maximum, chunks)
        m_prev = m_ref[sub]
        m_new = jnp.maximum(m_prev, jnp.max(mx, axis=-1, keepdims=True))
        corr = jnp.exp(m_prev - m_new)
        p = jnp.concatenate([jnp.exp(c - m_new) for c in chunks], axis=1).astype(BF16)
        pv = jnp.dot(p, vv, preferred_element_type=F32)
        acc_ref[sub] = jnp.concatenate([corr, corr], axis=1) * acc_ref[sub] + pv
        m_ref[sub] = m_new

    for sub in range(nsub):
        scores(sub, 0, slots[0])

    def full_blocks(t):
        for c in range(nsub):
            j = t * nsub + c
            for sub in range(nsub):
                scores(sub, j + 1, slots[(c + 1) % 2])
                softmax_pv(sub, j, slots[c % 2], masked=False)

    def body(u, carry):
        full_blocks(2 * u)
        full_blocks(2 * u + 1)
        return carry

    lax.fori_loop(0, qi // 2, body, 0)

    @pl.when(qi % 2 == 1)
    def _():
        full_blocks(qi - 1)

    base = qi * nsub
    for c in range(nsub):
        for sub in range(c, nsub):
            if sub > c:
                scores(sub, base + c + 1, slots[(c + 1) % 2])
            softmax_pv(sub, base + c, slots[c % 2], masked=(sub == c))
    for sub in range(nsub):
        acc = acc_ref[sub]
        o_ref[sub * blk:(sub + 1) * blk, :] = (acc[:, :V_DIM] / acc[:, V_DIM:]).astype(o_ref.dtype)


def _mla_attn(q, k, v, blk_pref=512, nsub_pref=2):
    heads, s, _ = q.shape
    blk = _tile(s, blk_pref)
    nsub = _tile(s // blk, nsub_pref)
    assert nsub % 2 == 0, "score slots alternate with key-block parity"
    tq = blk * nsub
    return pl.pallas_call(
        functools.partial(_mla_attn_kernel, blk=blk, nsub=nsub),
        out_shape=jax.ShapeDtypeStruct((s, heads * V_DIM), BF16),
        grid=(heads, s // tq),
        in_specs=[pl.BlockSpec((1, tq, QK_DIM), lambda h, i: (h, i, 0)),
                  pl.BlockSpec((1, s, QK_DIM), lambda h, i: (h, 0, 0)),
                  pl.BlockSpec((1, s, V_DIM), lambda h, i: (h, 0, 0))],
        out_specs=pl.BlockSpec((tq, V_DIM), lambda h, i: (i, h)),
        scratch_shapes=[pltpu.VMEM((nsub, blk, LANES), F32),
                        pltpu.VMEM((nsub, blk, 2 * V_DIM), F32),
                        pltpu.VMEM((nsub, blk, blk), F32),
                        pltpu.VMEM((nsub, blk, blk), F32)],
        compiler_params=_params("parallel", "arbitrary"),
    )(q, k, v)


def _ffn_block(x, xb, w_gate, w_up, w_down, l, g, b, alpha, want_bf16):
    hidden = w_gate.shape[2]
    hp = _round_up(hidden, 512)
    wg = _cast_pad(w_gate, l, cols_p=hp)
    wu = _cast_pad(w_up, l, cols_p=hp)
    wd = _cast_pad(w_down, l, rows_p=hp)
    hmid = _ffn_up(xb, wg, wu)
    return _mm_res_ln(hmid, wd, x, g, b, alpha, want_bf16=want_bf16)


def _attn_conv_layer(x, xb, w_in, conv_w, w_out, j, g, b, alpha):
    s, d = x.shape
    a_width = d // 2
    hproj = _in_proj(xb, _cast_pad(w_in, j), _rope_tables(s, HEAD_DIM), rope_cols=2 * a_width,
                     scale=HEAD_DIM ** -0.5, scaled_cols=a_width)
    outs, lses = [], []
    for window, dilation in DILATED_PATTERNS:
        assert window // dilation == DIL_BLOCK and s % window == 0
        o, lse = _dilated_branch(hproj, a_width, dilation)
        outs.append(o)
        lses.append(lse)
    ab = _mix(outs, lses, hproj, conv_w[j], a_width)
    return _proj_res_ln(ab, _cast_pad(w_out, j), x, g, b, alpha)


def _mla_layer(x, xb, w_in, q_norm, kv_norm, w_uq, w_ukv, w_out, j, g, b, alpha):
    s, d = x.shape
    heads = d // 128
    w_in_p = _cast_pad(w_in, j, cols_p=w_in.shape[2] + LANES - QK_ROPE)
    cq, ckv, kr = _mla_in(xb, w_in_p, q_norm[j], kv_norm[j], _rope_tables(s, QK_ROPE))
    w_uq_p = jnp.pad(w_uq[j].astype(BF16).reshape(Q_LORA, heads, QK_DIM),
                     ((0, 0), (0, 0), (0, Q_PAD - QK_DIM))).reshape(Q_LORA, heads * Q_PAD)
    q = _q_up(cq, w_uq_p, _rope_tables(s, QK_ROPE), heads, QK_DIM ** -0.5)
    k, v = _kv_up(ckv, _cast_pad(w_ukv, j), kr, heads)
    o = _mla_attn(q, k, v)
    return _proj_res_ln(o, _cast_pad(w_out, j), x, g, b, alpha)


def kernel(x, w_in_a, conv_w, w_out_a, w_in_c, q_norm, kv_norm, w_uq, w_ukv, w_out_c,
           ln1_g, ln1_b, w_gate, w_up, w_down, ln2_g, ln2_b):
    batch, s, d = x.shape
    depth = ln1_g.shape[0]
    alpha = (2.0 * depth) ** 0.25
    outs = []
    for bi in range(batch):
        xf = x[bi]
        xb = xf.astype(BF16)
        for l in range(depth):
            j = l // 2
            if l % 2 == 0:
                xf, xb = _attn_conv_layer(xf, xb, w_in_a, conv_w, w_out_a, j,
                                          ln1_g[l], ln1_b[l], alpha)
            else:
                xf, xb = _mla_layer(xf, xb, w_in_c, q_norm, kv_norm, w_uq, w_ukv, w_out_c, j,
                                    ln1_g[l], ln1_b[l], alpha)
            xf, xb = _ffn_block(xf, xb, w_gate, w_up, w_down, l, ln2_g[l], ln2_b[l],
                                alpha, want_bf16=l + 1 < depth)
        outs.append(xf)
    return jnp.stack(outs)
```

```python
import functools
import math

import jax
import jax.numpy as jnp
from jax import lax
from jax.experimental import pallas as pl
from jax.experimental.pallas import tpu as pltpu

F32 = jnp.float32
BF16 = jnp.bfloat16

HEAD_DIM = 128
CONV_WIDTH = 3
DILATED_PATTERNS = ((128, 1), (512, 4), (2048, 16))
DIL_BLOCK = 128
Q_LORA = 1536
KV_LORA = 512
QK_NOPE = 128
QK_ROPE = 64
V_DIM = 128
ROPE_THETA = 10000.0
LN_EPS = 1e-5
RMS_EPS = 1e-6
NEG = -1e30

LANES = 128
SUBLANES = 8
VMEM_LIMIT_BYTES = 60 * 1024 * 1024


def _round_up(n, m):
    return -(-n // m) * m


def _tile(n, pref):
    if n <= pref:
        return n
    t = pref
    while n % t:
        t //= 2
    return t


def _params(*sem):
    return pltpu.CompilerParams(dimension_semantics=sem, vmem_limit_bytes=VMEM_LIMIT_BYTES)


def _cast_pad_kernel(w_ref, o_ref, *, rows, cols, rows_p):
    tr, cols_p = o_ref.shape
    val = w_ref[...].astype(BF16)
    if rows_p > rows:
        row = pl.program_id(0) * tr + lax.broadcasted_iota(jnp.int32, val.shape, 0)
        val = jnp.where(row < rows, val, jnp.zeros_like(val))
    o_ref[:, :cols] = val
    if cols_p > cols:
        o_ref[:, cols:] = jnp.zeros((tr, cols_p - cols), BF16)


def _cast_pad(w_stack, layer, rows_p=None, cols_p=None, tr_pref=256):
    _, rows, cols = w_stack.shape
    rows_p = rows_p or rows
    cols_p = cols_p or cols
    tr = _tile(rows_p, tr_pref)
    last = (rows - 1) // tr
    return pl.pallas_call(
        functools.partial(_cast_pad_kernel, rows=rows, cols=cols, rows_p=rows_p),
        out_shape=jax.ShapeDtypeStruct((rows_p, cols_p), BF16),
        grid=(rows_p // tr,),
        in_specs=[pl.BlockSpec((None, tr, cols), lambda i: (layer, jnp.minimum(i, last), 0))],
        out_specs=pl.BlockSpec((tr, cols_p), lambda i: (i, 0)),
        compiler_params=_params("parallel"),
    )(w_stack)


def _rope_tables(seq, dim):
    half = dim // 2
    inv = ROPE_THETA ** (-jnp.arange(half, dtype=F32) * 2.0 / dim)
    ang = jnp.arange(seq).astype(F32)[:, None] * inv[None, :]
    cos, sin = jnp.cos(ang), jnp.sin(ang)
    reps = LANES // dim
    return (jnp.concatenate([cos, cos] * reps, axis=1),
            jnp.concatenate([-sin, sin] * reps, axis=1))


def _rope_full_lanes(t, cos, sin):
    return t * cos + pltpu.roll(t, HEAD_DIM // 2, axis=1) * sin


def _rope_half_lanes(y, cos, sin):
    half = QK_ROPE // 2
    lane = lax.broadcasted_iota(jnp.int32, y.shape, 1)
    partner = jnp.where(lane % QK_ROPE < half, pltpu.roll(y, LANES - half, axis=1),
                        pltpu.roll(y, half, axis=1))
    return y * cos + partner * sin


def _in_proj_kernel(a_ref, w_ref, cos_ref, sin_ref, o_ref, *, scale, n_scaled_tiles,
                    n_rope_tiles):
    j = pl.program_id(1)
    acc = jnp.dot(a_ref[...], w_ref[...], preferred_element_type=F32)

    @pl.when(j >= n_rope_tiles)
    def _():
        o_ref[...] = acc

    @pl.when(j < n_rope_tiles)
    def _():
        cos = cos_ref[...]
        sin = sin_ref[...]
        mult = jnp.where(j < n_scaled_tiles, scale, 1.0).astype(F32)
        for c in range(acc.shape[1] // HEAD_DIM):
            sl = slice(c * HEAD_DIM, (c + 1) * HEAD_DIM)
            o_ref[:, sl] = _rope_full_lanes(acc[:, sl], cos, sin) * mult


def _in_proj(a, w, rope_tables, *, rope_cols, scale, scaled_cols, tm_pref=1024, tn_pref=1024):
    m, k = a.shape
    n = w.shape[1]
    tm = _tile(m, tm_pref)
    tn = _tile(math.gcd(n, rope_cols, scaled_cols), tn_pref)
    return pl.pallas_call(
        functools.partial(_in_proj_kernel, scale=scale, n_scaled_tiles=scaled_cols // tn,
                          n_rope_tiles=rope_cols // tn),
        out_shape=jax.ShapeDtypeStruct((m, n), F32),
        grid=(m // tm, n // tn),
        in_specs=[pl.BlockSpec((tm, k), lambda i, j: (i, 0)),
                  pl.BlockSpec((k, tn), lambda i, j: (0, j)),
                  pl.BlockSpec((tm, LANES), lambda i, j: (i, 0)),
                  pl.BlockSpec((tm, LANES), lambda i, j: (i, 0))],
        out_specs=pl.BlockSpec((tm, tn), lambda i, j: (i, j)),
        compiler_params=_params("parallel", "arbitrary"),
    )(a, w, *rope_tables)


DIL_ROWS = 2048


def _dilated_kernel(q_ref, kp_ref, kc_ref, vp_ref, vc_ref, o_ref, lse_ref, *, d, groups):
    n = pl.program_id(0)
    h = pl.program_id(1)
    blk = DIL_BLOCK
    span = blk * d
    qi = lax.broadcasted_iota(jnp.int32, (blk, 2 * blk), 0)
    ki = lax.broadcasted_iota(jnp.int32, (blk, 2 * blk), 1)
    dist = qi + blk - ki
    band = (dist >= 0) & (dist <= blk)
    band_first = band & ((ki >= blk) | (n > 0))
    lane = lax.broadcasted_iota(jnp.int32, (blk, LANES), 1)

    @pl.when(h == 0)
    def _():
        lse_ref[...] = jnp.zeros(lse_ref.shape, F32)

    def sel(start):
        return pl.ds(start, blk, stride=d) if d > 1 else pl.ds(start, blk)

    for g in range(groups):
        for r in range(d):
            cur = sel(g * span + r)
            q = q_ref[cur, :].astype(BF16)
            if g == 0:
                k_prev, v_prev = kp_ref[sel(r), :], vp_ref[sel(r), :]
            else:
                prev = sel((g - 1) * span + r)
                k_prev, v_prev = kc_ref[prev, :], vc_ref[prev, :]
            kk = jnp.concatenate([k_prev, kc_ref[cur, :]], axis=0).astype(BF16)
            vv = jnp.concatenate([v_prev, vc_ref[cur, :]], axis=0).astype(BF16)
            s = lax.dot_general(q, kk, (((1,), (1,)), ((), ())), preferred_element_type=F32)
            s = jnp.where(band_first if g == 0 else band, s, NEG)
            m = jnp.max(s, axis=-1, keepdims=True)
            p = jnp.exp(s - m)
            den = jnp.sum(p, axis=-1, keepdims=True)
            o = jnp.dot(p.astype(BF16), vv, preferred_element_type=F32)
            o_ref[cur, :] = o / den
            lse_ref[cur, :] = jnp.where(lane == h, m + jnp.log(den), lse_ref[cur, :])


def _dilated_branch(hproj, a_width, dilation):
    s = hproj.shape[0]
    heads = a_width // HEAD_DIM
    d = dilation
    span = DIL_BLOCK * d
    rows = _tile(s, DIL_ROWS)
    groups = rows // span
    assert groups >= 1 and rows % span == 0
    cur = lambda col0: pl.BlockSpec((rows, HEAD_DIM), lambda n, h: (n, col0 + h))
    prev = lambda col0: pl.BlockSpec(
        (span, HEAD_DIM), lambda n, h: (jnp.maximum(n * groups - 1, 0), col0 + h))
    return pl.pallas_call(
        functools.partial(_dilated_kernel, d=d, groups=groups),
        out_shape=(jax.ShapeDtypeStruct((s, a_width), F32),
                   jax.ShapeDtypeStruct((s, LANES), F32)),
        grid=(s // rows, heads),
        in_specs=[cur(0), prev(heads), cur(heads), prev(2 * heads), cur(2 * heads)],
        out_specs=(pl.BlockSpec((rows, HEAD_DIM), lambda n, h: (n, h)),
                   pl.BlockSpec((rows, LANES), lambda n, h: (n, 0))),
        compiler_params=_params("parallel", "arbitrary"),
    )(hproj, hproj, hproj, hproj, hproj)


def _mix_kernel(o1_ref, o2_ref, o3_ref, l1_ref, l2_ref, l3_ref,
                gb_ref, gc_ref, hin_ref, gch_ref, hinh_ref, cw_ref, out_ref, *, heads, a_width):
    i = pl.program_id(0)
    l1, l2, l3 = l1_ref[...], l2_ref[...], l3_ref[...]
    m = jnp.maximum(jnp.maximum(l1, l2), l3)
    e1, e2, e3 = jnp.exp(l1 - m), jnp.exp(l2 - m), jnp.exp(l3 - m)
    inv = 1.0 / (e1 + e2 + e3)
    w1, w2, w3 = e1 * inv, e2 * inv, e3 * inv
    for h in range(heads):
        sl = slice(h * HEAD_DIM, (h + 1) * HEAD_DIM)
        hs = slice(h, h + 1)
        mixed = (w1[:, hs] * o1_ref[:, sl] + w2[:, hs] * o2_ref[:, sl]
                 + w3[:, hs] * o3_ref[:, sl])
        out_ref[:, sl] = mixed.astype(out_ref.dtype)

    u = gc_ref[...] * hin_ref[...]
    halo = gch_ref[...] * hinh_ref[...]
    halo = halo * jnp.where(i > 0, 1.0, 0.0).astype(F32)
    row = lax.broadcasted_iota(jnp.int32, u.shape, 0)
    hm1 = halo[SUBLANES - 1:SUBLANES, :]
    hm2 = halo[SUBLANES - 2:SUBLANES - 1, :]
    u1 = jnp.where(row == 0, hm1, pltpu.roll(u, 1, axis=0))
    u2 = jnp.where(row == 0, hm2, jnp.where(row == 1, hm1, pltpu.roll(u, 2, axis=0)))
    cw = cw_ref[...]
    y = cw[0:1, :] * u2 + cw[1:2, :] * u1 + cw[2:3, :] * u
    out_ref[:, a_width:] = (gb_ref[...] * y).astype(out_ref.dtype)


def _mix(o_list, lse_list, hproj, conv_w, a_width, tm_pref=256):
    s = hproj.shape[0]
    b_width = (hproj.shape[1] - 3 * a_width) // 3
    assert b_width == a_width, "gate blocks are addressed in units of the attention width"
    gb_blk, gc_blk, hin_blk = 3, 4, 5
    heads = a_width // HEAD_DIM
    tm = _tile(s, tm_pref)
    hb = tm // SUBLANES
    row_blk = lambda c: pl.BlockSpec((tm, a_width), lambda i: (i, c))
    halo_blk = lambda c: pl.BlockSpec((SUBLANES, b_width),
                                      lambda i: (jnp.maximum(i * hb - 1, 0), c))
    lse_blk = pl.BlockSpec((tm, LANES), lambda i: (i, 0))
    return pl.pallas_call(
        functools.partial(_mix_kernel, heads=heads, a_width=a_width),
        out_shape=jax.ShapeDtypeStruct((s, a_width + b_width), BF16),
        grid=(s // tm,),
        in_specs=[row_blk(0)] * 3 + [lse_blk] * 3
        + [row_blk(gb_blk), row_blk(gc_blk), row_blk(hin_blk), halo_blk(gc_blk),
           halo_blk(hin_blk), pl.BlockSpec((CONV_WIDTH, b_width), lambda i: (0, 0))],
        out_specs=pl.BlockSpec((tm, a_width + b_width), lambda i: (i, 0)),
        compiler_params=_params("parallel"),
    )(*o_list, *lse_list, hproj, hproj, hproj, hproj, hproj, conv_w)


LN_COL_CHUNK = 1024
LN_ROW_CHUNK = 8


def _residual_ln(x_ref, g_ref, b_ref, o_ref, ob_ref, alpha):
    tm = o_ref.shape[0]
    rc = _tile(tm, LN_ROW_CHUNK)
    for r in range(tm // rc):
        rows = slice(r * rc, (r + 1) * rc)
        z = alpha * x_ref[rows, :] + o_ref[rows, :]
        mu = jnp.mean(z, axis=-1, keepdims=True)
        zc = z - mu
        var = jnp.mean(zc * zc, axis=-1, keepdims=True)
        y = zc * lax.rsqrt(var + LN_EPS) * g_ref[...] + b_ref[...]
        o_ref[rows, :] = y
        if ob_ref is not None:
            ob_ref[rows, :] = y.astype(BF16)


def _mm_res_ln_kernel(a_ref, w_ref, x_ref, g_ref, b_ref, o_ref, *maybe_ob, alpha):
    k = pl.program_id(1)
    n = o_ref.shape[1]

    @pl.when(k == 0)
    def _():
        o_ref[...] = jnp.zeros(o_ref.shape, F32)

    a = a_ref[...]
    nc = _tile(n, LN_COL_CHUNK)
    for c in range(n // nc):
        sl = slice(c * nc, (c + 1) * nc)
        o_ref[:, sl] += jnp.dot(a, w_ref[:, sl], preferred_element_type=F32)

    @pl.when(k == pl.num_programs(1) - 1)
    def _():
        _residual_ln(x_ref, g_ref, b_ref, o_ref, maybe_ob[0] if maybe_ob else None, alpha)


def _mm_res_ln(a, w, xres, g, b, alpha, *, want_bf16, tm_pref=512, tk_pref=1024):
    m, k = a.shape
    n = w.shape[1]
    tm = _tile(m, tm_pref)
    tk = _tile(k, tk_pref)
    out_shape = [jax.ShapeDtypeStruct((m, n), F32)]
    out_specs = [pl.BlockSpec((tm, n), lambda i, kk: (i, 0))]
    if want_bf16:
        out_shape.append(jax.ShapeDtypeStruct((m, n), BF16))
        out_specs.append(pl.BlockSpec((tm, n), lambda i, kk: (i, 0)))
    res = pl.pallas_call(
        functools.partial(_mm_res_ln_kernel, alpha=alpha),
        out_shape=tuple(out_shape),
        grid=(m // tm, k // tk),
        in_specs=[
            pl.BlockSpec((tm, tk), lambda i, kk: (i, kk)),
            pl.BlockSpec((tk, n), lambda i, kk: (kk, 0)),
            pl.BlockSpec((tm, n), lambda i, kk: (i, 0), pipeline_mode=pl.Buffered(1)),
            pl.BlockSpec((1, n), lambda i, kk: (0, 0)),
            pl.BlockSpec((1, n), lambda i, kk: (0, 0)),
        ],
        out_specs=tuple(out_specs),
        compiler_params=_params("parallel", "arbitrary"),
    )(a, w, xres, g.reshape(1, n), b.reshape(1, n))
    return res if want_bf16 else (res[0], None)


def _proj_res_ln_kernel(a_ref, w_ref, x_ref, g_ref, b_ref, o_ref, ob_ref, *, alpha):
    n = o_ref.shape[1]
    a = a_ref[...]
    nc = _tile(n, LN_COL_CHUNK)
    for c in range(n // nc):
        sl = slice(c * nc, (c + 1) * nc)
        o_ref[:, sl] = jnp.dot(a, w_ref[:, sl], preferred_element_type=F32)
    _residual_ln(x_ref, g_ref, b_ref, o_ref, ob_ref, alpha)


def _proj_res_ln(a, w, xres, g, b, alpha, tm_pref=128):
    m, k = a.shape
    n = w.shape[1]
    tm = _tile(m, tm_pref)
    row = lambda width: pl.BlockSpec((tm, width), lambda i: (i, 0))
    vec = pl.BlockSpec((1, n), lambda i: (0, 0))
    return pl.pallas_call(
        functools.partial(_proj_res_ln_kernel, alpha=alpha),
        out_shape=(jax.ShapeDtypeStruct((m, n), F32), jax.ShapeDtypeStruct((m, n), BF16)),
        grid=(m // tm,),
        in_specs=[row(k),
                  pl.BlockSpec((k, n), lambda i: (0, 0), pipeline_mode=pl.Buffered(1)),
                  row(n), vec, vec],
        out_specs=(row(n), row(n)),
        compiler_params=_params("arbitrary"),
    )(a, w, xres, g.reshape(1, n), b.reshape(1, n))


def _ffn_up_kernel(a_ref, wg_ref, wu_ref, o_ref):
    a = a_ref[...]
    g = jnp.dot(a, wg_ref[...], preferred_element_type=F32)
    u = jnp.dot(a, wu_ref[...], preferred_element_type=F32)
    o_ref[...] = (g * jax.nn.sigmoid(g) * u).astype(o_ref.dtype)


def _ffn_up(a, wg, wu, tm_pref=1024, tn_pref=512):
    m, k = a.shape
    n = wg.shape[1]
    tm = _tile(m, tm_pref)
    tn = _tile(n, tn_pref)
    return pl.pallas_call(
        _ffn_up_kernel,
        out_shape=jax.ShapeDtypeStruct((m, n), BF16),
        grid=(m // tm, n // tn),
        in_specs=[pl.BlockSpec((tm, k), lambda i, j: (i, 0)),
                  pl.BlockSpec((k, tn), lambda i, j: (0, j)),
                  pl.BlockSpec((k, tn), lambda i, j: (0, j))],
        out_specs=pl.BlockSpec((tm, tn), lambda i, j: (i, j)),
        compiler_params=_params("parallel", "arbitrary"),
    )(a, wg, wu)


def _mla_in_kernel(a_ref, w_ref, qn_ref, kvn_ref, cos_ref, sin_ref,
                   cq_ref, ckv_ref, kr_ref, acc_ref):
    k = pl.program_id(1)
    part = jnp.dot(a_ref[...], w_ref[...], preferred_element_type=F32)

    @pl.when(k == 0)
    def _():
        acc_ref[...] = part

    @pl.when(k > 0)
    def _():
        acc_ref[...] += part

    @pl.when(k == pl.num_programs(1) - 1)
    def _():
        def rms(t, gain):
            r = lax.rsqrt(jnp.mean(t * t, axis=-1, keepdims=True) + RMS_EPS)
            return t * r * gain

        cq_ref[...] = rms(acc_ref[:, :Q_LORA], qn_ref[...]).astype(BF16)
        ckv_ref[...] = rms(acc_ref[:, Q_LORA:Q_LORA + KV_LORA], kvn_ref[...]).astype(BF16)
        y = acc_ref[:, Q_LORA + KV_LORA:]
        kr_ref[...] = _rope_half_lanes(y, cos_ref[...], sin_ref[...]).astype(BF16)


def _mla_in(a, w, q_norm, kv_norm, tables, tm_pref=1024, tk_pref=1024):
    m, k = a.shape
    n = w.shape[1]
    tm = _tile(m, tm_pref)
    tk = _tile(k, tk_pref)
    row = lambda width: pl.BlockSpec((tm, width), lambda i, kk: (i, 0))
    return pl.pallas_call(
        _mla_in_kernel,
        out_shape=(jax.ShapeDtypeStruct((m, Q_LORA), BF16),
                   jax.ShapeDtypeStruct((m, KV_LORA), BF16),
                   jax.ShapeDtypeStruct((m, LANES), BF16)),
        grid=(m // tm, k // tk),
        in_specs=[pl.BlockSpec((tm, tk), lambda i, kk: (i, kk)),
                  pl.BlockSpec((tk, n), lambda i, kk: (kk, 0)),
                  pl.BlockSpec((1, Q_LORA), lambda i, kk: (0, 0)),
                  pl.BlockSpec((1, KV_LORA), lambda i, kk: (0, 0)),
                  row(LANES), row(LANES)],
        out_specs=(row(Q_LORA), row(KV_LORA), row(LANES)),
        scratch_shapes=[pltpu.VMEM((tm, n), F32)],
        compiler_params=_params("parallel", "arbitrary"),
    )(a, w, q_norm.reshape(1, Q_LORA), kv_norm.reshape(1, KV_LORA), *tables)


QK_DIM = QK_NOPE + QK_ROPE
UP_HEADS_PER_DOT = 4


def _q_up_kernel(a_ref, wn_ref, wr_ref, cos_ref, sin_ref, q_ref, *, heads_per_step, scale):
    a = a_ref[...]
    cos, sin = cos_ref[...], sin_ref[...]
    hpd = UP_HEADS_PER_DOT
    for c in range(heads_per_step // hpd):
        nope = jnp.dot(a, wn_ref[:, c * hpd * QK_NOPE:(c + 1) * hpd * QK_NOPE],
                       preferred_element_type=F32)
        rope = jnp.dot(a, wr_ref[:, c * hpd * QK_ROPE:(c + 1) * hpd * QK_ROPE],
                       preferred_element_type=F32)
        roped = [_rope_half_lanes(rope[:, t * LANES:(t + 1) * LANES], cos, sin) * scale
                 for t in range(hpd * QK_ROPE // LANES)]
        for h in range(hpd):
            tile, half = divmod(h * QK_ROPE, LANES)
            q_ref[c * hpd + h, :, :QK_NOPE] = (
                nope[:, h * QK_NOPE:(h + 1) * QK_NOPE] * scale).astype(BF16)
            q_ref[c * hpd + h, :, QK_NOPE:] = roped[tile][:, half:half + QK_ROPE].astype(BF16)


def _q_up(cq, w_uq_g, tables, heads, scale, tm_pref=1024, hps_pref=8):
    m, k = cq.shape
    tm = _tile(m, tm_pref)
    hps = _tile(heads, hps_pref)
    assert hps % UP_HEADS_PER_DOT == 0 and (UP_HEADS_PER_DOT * QK_ROPE) % LANES == 0
    rope_blk0 = heads * QK_NOPE // (hps * QK_ROPE)
    return pl.pallas_call(
        functools.partial(_q_up_kernel, heads_per_step=hps, scale=scale),
        out_shape=jax.ShapeDtypeStruct((heads, m, QK_DIM), BF16),
        grid=(m // tm, heads // hps),
        in_specs=[pl.BlockSpec((tm, k), lambda i, j: (i, 0)),
                  pl.BlockSpec((k, hps * QK_NOPE), lambda i, j: (0, j)),
                  pl.BlockSpec((k, hps * QK_ROPE), lambda i, j: (0, rope_blk0 + j)),
                  pl.BlockSpec((tm, LANES), lambda i, j: (i, 0)),
                  pl.BlockSpec((tm, LANES), lambda i, j: (i, 0))],
        out_specs=pl.BlockSpec((hps, tm, QK_DIM), lambda i, j: (j, i, 0)),
        compiler_params=_params("parallel", "arbitrary"),
    )(cq, w_uq_g, w_uq_g, *tables)


def _kv_up_kernel(a_ref, w_ref, kr_ref, k_ref, v_ref, *, heads_per_step):
    a = a_ref[...]
    kr = kr_ref[:, :QK_ROPE]
    width = QK_NOPE + V_DIM
    hpd = UP_HEADS_PER_DOT
    for c in range(heads_per_step // hpd):
        acc = jnp.dot(a, w_ref[:, c * hpd * width:(c + 1) * hpd * width],
                      preferred_element_type=F32)
        for h in range(hpd):
            base = h * width
            k_ref[c * hpd + h, :, :QK_NOPE] = acc[:, base:base + QK_NOPE].astype(BF16)
            k_ref[c * hpd + h, :, QK_NOPE:] = kr
            v_ref[c * hpd + h] = acc[:, base + QK_NOPE:base + width].astype(BF16)


def _kv_up(ckv, w_ukv, kr, heads, tm_pref=1024, hps_pref=16):
    m, k = ckv.shape
    tm = _tile(m, tm_pref)
    hps = _tile(heads, hps_pref)
    assert hps % UP_HEADS_PER_DOT == 0
    width = QK_NOPE + V_DIM
    return pl.pallas_call(
        functools.partial(_kv_up_kernel, heads_per_step=hps),
        out_shape=(jax.ShapeDtypeStruct((heads, m, QK_DIM), BF16),
                   jax.ShapeDtypeStruct((heads, m, V_DIM), BF16)),
        grid=(m // tm, heads // hps),
        in_specs=[pl.BlockSpec((tm, k), lambda i, j: (i, 0)),
                  pl.BlockSpec((k, hps * width), lambda i, j: (0, j)),
                  pl.BlockSpec((tm, LANES), lambda i, j: (i, 0))],
        out_specs=(pl.BlockSpec((hps, tm, QK_DIM), lambda i, j: (j, i, 0)),
                   pl.BlockSpec((hps, tm, V_DIM), lambda i, j: (j, i, 0))),
        compiler_params=_params("parallel", "arbitrary"),
    )(ckv, w_ukv, kr)


def _mla_attn_kernel(q_ref, k_ref, v_ref, o_ref, m_ref, acc_ref, s0_ref, s1_ref, *, blk, nsub):
    qi = pl.program_id(1)
    m_ref[...] = jnp.full(m_ref.shape, NEG, F32)
    acc_ref[...] = jnp.zeros(acc_ref.shape, F32)
    ones = jnp.ones((blk, V_DIM), BF16)
    slots = (s0_ref, s1_ref)

    def scores(sub, j, slot):
        start = pl.multiple_of(j * blk, blk)
        kk = k_ref[0, pl.ds(start, blk), :]
        q = q_ref[0, sub * blk:(sub + 1) * blk, :]
        slot[sub] = lax.dot_general(q, kk, (((1,), (1,)), ((), ())),
                                    preferred_element_type=F32)

    def softmax_pv(sub, j, slot, masked):
        start = pl.multiple_of(j * blk, blk)
        vv = jnp.concatenate([v_ref[0, pl.ds(start, blk), :], ones], axis=1)
        s = slot[sub]
        if masked:
            row = lax.broadcasted_iota(jnp.int32, s.shape, 0)
            col = lax.broadcasted_iota(jnp.int32, s.shape, 1)
            s = jnp.where(col <= row, s, NEG)
        chunks = [s[:, c * LANES:(c + 1) * LANES] for c in range(blk // LANES)]
        mx = functools.reduce(jnp.maximum, chunks)
        m_prev = m_ref[sub]
        m_new = jnp.maximum(m_prev, jnp.max(mx, axis=-1, keepdims=True))
        corr = jnp.exp(m_prev - m_new)
        p = jnp.concatenate([jnp.exp(c - m_new) for c in chunks], axis=1).astype(BF16)
        pv = jnp.dot(p, vv, preferred_element_type=F32)
        acc_ref[sub] = jnp.concatenate([corr, corr], axis=1) * acc_ref[sub] + pv
        m_ref[sub] = m_new

    for sub in range(nsub):
        scores(sub, 0, slots[0])

    def full_blocks(t):
        for c in range(nsub):
            j = t * nsub + c
            for sub in range(nsub):
                scores(sub, j + 1, slots[(c + 1) % 2])
                softmax_pv(sub, j, slots[c % 2], masked=False)

    def body(u, carry):
        full_blocks(2 * u)
        full_blocks(2 * u + 1)
        return carry

    lax.fori_loop(0, qi // 2, body, 0)

    @pl.when(qi % 2 == 1)
    def _():
        full_blocks(qi - 1)

    base = qi * nsub
    for c in range(nsub):
        for sub in range(c, nsub):
            if sub > c:
                scores(sub, base + c + 1, slots[(c + 1) % 2])
            softmax_pv(sub, base + c, slots[c % 2], masked=(sub == c))
    for sub in range(nsub):
        acc = acc_ref[sub]
        o_ref[sub * blk:(sub + 1) * blk, :] = (acc[:, :V_DIM] / acc[:, V_DIM:]).astype(o_ref.dtype)


def _mla_attn(q, k, v, blk_pref=512, nsub_pref=2):
    heads, s, _ = q.shape
    blk = _tile(s, blk_pref)
    nsub = _tile(s // blk, nsub_pref)
    assert nsub % 2 == 0, "score slots alternate with key-block parity"
    tq = blk * nsub
    return pl.pallas_call(
        functools.partial(_mla_attn_kernel, blk=blk, nsub=nsub),
        out_shape=jax.ShapeDtypeStruct((s, heads * V_DIM), BF16),
        grid=(heads, s // tq),
        in_specs=[pl.BlockSpec((1, tq, QK_DIM), lambda h, i: (h, i, 0)),
                  pl.BlockSpec((1, s, QK_DIM), lambda h, i: (h, 0, 0)),
                  pl.BlockSpec((1, s, V_DIM), lambda h, i: (h, 0, 0))],
        out_specs=pl.BlockSpec((tq, V_DIM), lambda h, i: (i, h)),
        scratch_shapes=[pltpu.VMEM((nsub, blk, LANES), F32),
                        pltpu.VMEM((nsub, blk, 2 * V_DIM), F32),
                        pltpu.VMEM((nsub, blk, blk), F32),
                        pltpu.VMEM((nsub, blk, blk), F32)],
        compiler_params=_params("parallel", "arbitrary"),
    )(q, k, v)


def _ffn_block(x, xb, w_gate, w_up, w_down, l, g, b, alpha, want_bf16):
    hidden = w_gate.shape[2]
    hp = _round_up(hidden, 512)
    wg = _cast_pad(w_gate, l, cols_p=hp)
    wu = _cast_pad(w_up, l, cols_p=hp)
    wd = _cast_pad(w_down, l, rows_p=hp)
    hmid = _ffn_up(xb, wg, wu)
    return _mm_res_ln(hmid, wd, x, g, b, alpha, want_bf16=want_bf16)


def _attn_conv_layer(x, xb, w_in, conv_w, w_out, j, g, b, alpha):
    s, d = x.shape
    a_width = d // 2
    hproj = _in_proj(xb, _cast_pad(w_in, j), _rope_tables(s, HEAD_DIM), rope_cols=2 * a_width,
                     scale=HEAD_DIM ** -0.5, scaled_cols=a_width)
    outs, lses = [], []
    for window, dilation in DILATED_PATTERNS:
        assert window // dilation == DIL_BLOCK and s % window == 0
        o, lse = _dilated_branch(hproj, a_width, dilation)
        outs.append(o)
        lses.append(lse)
    ab = _mix(outs, lses, hproj, conv_w[j], a_width)
    return _proj_res_ln(ab, _cast_pad(w_out, j), x, g, b, alpha)


def _mla_layer(x, xb, w_in, q_norm, kv_norm, w_uq, w_ukv, w_out, j, g, b, alpha):
    s, d = x.shape
    heads = d // 128
    w_in_p = _cast_pad(w_in, j, cols_p=w_in.shape[2] + LANES - QK_ROPE)
    cq, ckv, kr = _mla_in(xb, w_in_p, q_norm[j], kv_norm[j], _rope_tables(s, QK_ROPE))
    w_uq_h = w_uq[j].astype(BF16).reshape(Q_LORA, heads, QK_DIM)
    w_uq_g = jnp.concatenate([w_uq_h[:, :, :QK_NOPE].reshape(Q_LORA, heads * QK_NOPE),
                              w_uq_h[:, :, QK_NOPE:].reshape(Q_LORA, heads * QK_ROPE)], axis=1)
    q = _q_up(cq, w_uq_g, _rope_tables(s, QK_ROPE), heads, QK_DIM ** -0.5)
    k, v = _kv_up(ckv, _cast_pad(w_ukv, j), kr, heads)
    o = _mla_attn(q, k, v)
    return _proj_res_ln(o, _cast_pad(w_out, j), x, g, b, alpha)


def kernel(x, w_in_a, conv_w, w_out_a, w_in_c, q_norm, kv_norm, w_uq, w_ukv, w_out_c,
           ln1_g, ln1_b, w_gate, w_up, w_down, ln2_g, ln2_b):
    batch, s, d = x.shape
    depth = ln1_g.shape[0]
    alpha = (2.0 * depth) ** 0.25
    outs = []
    for bi in range(batch):
        xf = x[bi]
        xb = xf.astype(BF16)
        for l in range(depth):
            j = l // 2
            if l % 2 == 0:
                xf, xb = _attn_conv_layer(xf, xb, w_in_a, conv_w, w_out_a, j,
                                          ln1_g[l], ln1_b[l], alpha)
            else:
                xf, xb = _mla_layer(xf, xb, w_in_c, q_norm, kv_norm, w_uq, w_ukv, w_out_c, j,
                                    ln1_g[l], ln1_b[l], alpha)
            xf, xb = _ffn_block(xf, xb, w_gate, w_up, w_down, l, ln2_g[l], ln2_b[l],
                                alpha, want_bf16=l + 1 < depth)
        outs.append(xf)
    return jnp.stack(outs)
```

```python
import functools
import math

import jax
import jax.numpy as jnp
from jax import lax
from jax.experimental import pallas as pl
from jax.experimental.pallas import tpu as pltpu

F32 = jnp.float32
BF16 = jnp.bfloat16

HEAD_DIM = 128
CONV_WIDTH = 3
DILATED_PATTERNS = ((128, 1), (512, 4), (2048, 16))
DIL_BLOCK = 128
Q_LORA = 1536
KV_LORA = 512
QK_NOPE = 128
QK_ROPE = 64
V_DIM = 128
ROPE_THETA = 10000.0
LN_EPS = 1e-5
RMS_EPS = 1e-6
NEG = -1e30

LANES = 128
SUBLANES = 8
VMEM_LIMIT_BYTES = 60 * 1024 * 1024


def _round_up(n, m):
    return -(-n // m) * m


def _tile(n, pref):
    if n <= pref:
        return n
    t = pref
    while n % t:
        t //= 2
    return t


def _params(*sem):
    return pltpu.CompilerParams(dimension_semantics=sem, vmem_limit_bytes=VMEM_LIMIT_BYTES)


def _cast_pad_kernel(w_ref, o_ref, *, rows, cols, rows_p):
    tr, cols_p = o_ref.shape
    val = w_ref[...].astype(BF16)
    if rows_p > rows:
        row = pl.program_id(0) * tr + lax.broadcasted_iota(jnp.int32, val.shape, 0)
        val = jnp.where(row < rows, val, jnp.zeros_like(val))
    o_ref[:, :cols] = val
    if cols_p > cols:
        o_ref[:, cols:] = jnp.zeros((tr, cols_p - cols), BF16)


def _cast_pad(w_stack, layer, rows_p=None, cols_p=None, tr_pref=256):
    _, rows, cols = w_stack.shape
    rows_p = rows_p or rows
    cols_p = cols_p or cols
    tr = _tile(rows_p, tr_pref)
    last = (rows - 1) // tr
    return pl.pallas_call(
        functools.partial(_cast_pad_kernel, rows=rows, cols=cols, rows_p=rows_p),
        out_shape=jax.ShapeDtypeStruct((rows_p, cols_p), BF16),
        grid=(rows_p // tr,),
        in_specs=[pl.BlockSpec((None, tr, cols), lambda i: (layer, jnp.minimum(i, last), 0))],
        out_specs=pl.BlockSpec((tr, cols_p), lambda i: (i, 0)),
        compiler_params=_params("parallel"),
    )(w_stack)


def _rope_tables(seq, dim):
    half = dim // 2
    inv = ROPE_THETA ** (-jnp.arange(half, dtype=F32) * 2.0 / dim)
    ang = jnp.arange(seq).astype(F32)[:, None] * inv[None, :]
    cos, sin = jnp.cos(ang), jnp.sin(ang)
    reps = LANES // dim
    return (jnp.concatenate([cos, cos] * reps, axis=1),
            jnp.concatenate([-sin, sin] * reps, axis=1))


def _rope_full_lanes(t, cos, sin):
    return t * cos + pltpu.roll(t, HEAD_DIM // 2, axis=1) * sin


def _rope_half_lanes(y, cos, sin):
    half = QK_ROPE // 2
    lane = lax.broadcasted_iota(jnp.int32, y.shape, 1)
    partner = jnp.where(lane % QK_ROPE < half, pltpu.roll(y, LANES - half, axis=1),
                        pltpu.roll(y, half, axis=1))
    return y * cos + partner * sin


def _in_proj_kernel(a_ref, w_ref, cos_ref, sin_ref, o_ref, *, scale, n_scaled_tiles,
                    n_rope_tiles):
    j = pl.program_id(1)
    acc = jnp.dot(a_ref[...], w_ref[...], preferred_element_type=F32)

    @pl.when(j >= n_rope_tiles)
    def _():
        o_ref[...] = acc

    @pl.when(j < n_rope_tiles)
    def _():
        cos = cos_ref[...]
        sin = sin_ref[...]
        mult = jnp.where(j < n_scaled_tiles, scale, 1.0).astype(F32)
        for c in range(acc.shape[1] // HEAD_DIM):
            sl = slice(c * HEAD_DIM, (c + 1) * HEAD_DIM)
            o_ref[:, sl] = _rope_full_lanes(acc[:, sl], cos, sin) * mult


def _in_proj(a, w, rope_tables, *, rope_cols, scale, scaled_cols, tm_pref=1024, tn_pref=1024):
    m, k = a.shape
    n = w.shape[1]
    tm = _tile(m, tm_pref)
    tn = _tile(math.gcd(n, rope_cols, scaled_cols), tn_pref)
    return pl.pallas_call(
        functools.partial(_in_proj_kernel, scale=scale, n_scaled_tiles=scaled_cols // tn,
                          n_rope_tiles=rope_cols // tn),
        out_shape=jax.ShapeDtypeStruct((m, n), F32),
        grid=(m // tm, n // tn),
        in_specs=[pl.BlockSpec((tm, k), lambda i, j: (i, 0)),
                  pl.BlockSpec((k, tn), lambda i, j: (0, j)),
                  pl.BlockSpec((tm, LANES), lambda i, j: (i, 0)),
                  pl.BlockSpec((tm, LANES), lambda i, j: (i, 0))],
        out_specs=pl.BlockSpec((tm, tn), lambda i, j: (i, j)),
        compiler_params=_params("parallel", "arbitrary"),
    )(a, w, *rope_tables)


DIL_ROWS = 2048


def _dilated_kernel(q_ref, kp_ref, kc_ref, vp_ref, vc_ref, o_ref, lse_ref, *, d, groups):
    n = pl.program_id(0)
    h = pl.program_id(1)
    blk = DIL_BLOCK
    span = blk * d
    qi = lax.broadcasted_iota(jnp.int32, (blk, 2 * blk), 0)
    ki = lax.broadcasted_iota(jnp.int32, (blk, 2 * blk), 1)
    dist = qi + blk - ki
    band = (dist >= 0) & (dist <= blk)
    band_first = band & ((ki >= blk) | (n > 0))
    lane = lax.broadcasted_iota(jnp.int32, (blk, LANES), 1)

    @pl.when(h == 0)
    def _():
        lse_ref[...] = jnp.zeros(lse_ref.shape, F32)

    def sel(start):
        return pl.ds(start, blk, stride=d) if d > 1 else pl.ds(start, blk)

    for g in range(groups):
        for r in range(d):
            cur = sel(g * span + r)
            q = q_ref[cur, :].astype(BF16)
            if g == 0:
                k_prev, v_prev = kp_ref[sel(r), :], vp_ref[sel(r), :]
            else:
                prev = sel((g - 1) * span + r)
                k_prev, v_prev = kc_ref[prev, :], vc_ref[prev, :]
            kk = jnp.concatenate([k_prev, kc_ref[cur, :]], axis=0).astype(BF16)
            vv = jnp.concatenate([v_prev, vc_ref[cur, :]], axis=0).astype(BF16)
            s = lax.dot_general(q, kk, (((1,), (1,)), ((), ())), preferred_element_type=F32)
            s = jnp.where(band_first if g == 0 else band, s, NEG)
            m = jnp.max(s, axis=-1, keepdims=True)
            p = jnp.exp(s - m)
            den = jnp.sum(p, axis=-1, keepdims=True)
            o = jnp.dot(p.astype(BF16), vv, preferred_element_type=F32)
            o_ref[cur, :] = o / den
            lse_ref[cur, :] = jnp.where(lane == h, m + jnp.log(den), lse_ref[cur, :])


def _dilated_branch(hproj, a_width, dilation):
    s = hproj.shape[0]
    heads = a_width // HEAD_DIM
    d = dilation
    span = DIL_BLOCK * d
    rows = _tile(s, DIL_ROWS)
    groups = rows // span
    assert groups >= 1 and rows % span == 0
    cur = lambda col0: pl.BlockSpec((rows, HEAD_DIM), lambda n, h: (n, col0 + h))
    prev = lambda col0: pl.BlockSpec(
        (span, HEAD_DIM), lambda n, h: (jnp.maximum(n * groups - 1, 0), col0 + h))
    return pl.pallas_call(
        functools.partial(_dilated_kernel, d=d, groups=groups),
        out_shape=(jax.ShapeDtypeStruct((s, a_width), F32),
                   jax.ShapeDtypeStruct((s, LANES), F32)),
        grid=(s // rows, heads),
        in_specs=[cur(0), prev(heads), cur(heads), prev(2 * heads), cur(2 * heads)],
        out_specs=(pl.BlockSpec((rows, HEAD_DIM), lambda n, h: (n, h)),
                   pl.BlockSpec((rows, LANES), lambda n, h: (n, 0))),
        compiler_params=_params("parallel", "arbitrary"),
    )(hproj, hproj, hproj, hproj, hproj)


def _mix_kernel(o1_ref, o2_ref, o3_ref, l1_ref, l2_ref, l3_ref,
                gb_ref, gc_ref, hin_ref, gch_ref, hinh_ref, cw_ref, out_ref, *, heads, a_width):
    i = pl.program_id(0)
    l1, l2, l3 = l1_ref[...], l2_ref[...], l3_ref[...]
    m = jnp.maximum(jnp.maximum(l1, l2), l3)
    e1, e2, e3 = jnp.exp(l1 - m), jnp.exp(l2 - m), jnp.exp(l3 - m)
    inv = 1.0 / (e1 + e2 + e3)
    w1, w2, w3 = e1 * inv, e2 * inv, e3 * inv
    for h in range(heads):
        sl = slice(h * HEAD_DIM, (h + 1) * HEAD_DIM)
        hs = slice(h, h + 1)
        mixed = (w1[:, hs] * o1_ref[:, sl] + w2[:, hs] * o2_ref[:, sl]
                 + w3[:, hs] * o3_ref[:, sl])
        out_ref[:, sl] = mixed.astype(out_ref.dtype)

    u = gc_ref[...] * hin_ref[...]
    halo = gch_ref[...] * hinh_ref[...]
    halo = halo * jnp.where(i > 0, 1.0, 0.0).astype(F32)
    row = lax.broadcasted_iota(jnp.int32, u.shape, 0)
    hm1 = halo[SUBLANES - 1:SUBLANES, :]
    hm2 = halo[SUBLANES - 2:SUBLANES - 1, :]
    u1 = jnp.where(row == 0, hm1, pltpu.roll(u, 1, axis=0))
    u2 = jnp.where(row == 0, hm2, jnp.where(row == 1, hm1, pltpu.roll(u, 2, axis=0)))
    cw = cw_ref[...]
    y = cw[0:1, :] * u2 + cw[1:2, :] * u1 + cw[2:3, :] * u
    out_ref[:, a_width:] = (gb_ref[...] * y).astype(out_ref.dtype)


def _mix(o_list, lse_list, hproj, conv_w, a_width, tm_pref=256):
    s = hproj.shape[0]
    b_width = (hproj.shape[1] - 3 * a_width) // 3
    assert b_width == a_width, "gate blocks are addressed in units of the attention width"
    gb_blk, gc_blk, hin_blk = 3, 4, 5
    heads = a_width // HEAD_DIM
    tm = _tile(s, tm_pref)
    hb = tm // SUBLANES
    row_blk = lambda c: pl.BlockSpec((tm, a_width), lambda i: (i, c))
    halo_blk = lambda c: pl.BlockSpec((SUBLANES, b_width),
                                      lambda i: (jnp.maximum(i * hb - 1, 0), c))
    lse_blk = pl.BlockSpec((tm, LANES), lambda i: (i, 0))
    return pl.pallas_call(
        functools.partial(_mix_kernel, heads=heads, a_width=a_width),
        out_shape=jax.ShapeDtypeStruct((s, a_width + b_width), BF16),
        grid=(s // tm,),
        in_specs=[row_blk(0)] * 3 + [lse_blk] * 3
        + [row_blk(gb_blk), row_blk(gc_blk), row_blk(hin_blk), halo_blk(gc_blk),
           halo_blk(hin_blk), pl.BlockSpec((CONV_WIDTH, b_width), lambda i: (0, 0))],
        out_specs=pl.BlockSpec((tm, a_width + b_width), lambda i: (i, 0)),
        compiler_params=_params("parallel"),
    )(*o_list, *lse_list, hproj, hproj, hproj, hproj, hproj, conv_w)


LN_COL_CHUNK = 1024
LN_ROW_CHUNK = 8


def _residual_ln(x_ref, g_ref, b_ref, o_ref, ob_ref, alpha):
    tm = o_ref.shape[0]
    rc = _tile(tm, LN_ROW_CHUNK)
    for r in range(tm // rc):
        rows = slice(r * rc, (r + 1) * rc)
        z = alpha * x_ref[rows, :] + o_ref[rows, :]
        mu = jnp.mean(z, axis=-1, keepdims=True)
        zc = z - mu
        var = jnp.mean(zc * zc, axis=-1, keepdims=True)
        y = zc * lax.rsqrt(var + LN_EPS) * g_ref[...] + b_ref[...]
        o_ref[rows, :] = y
        if ob_ref is not None:
            ob_ref[rows, :] = y.astype(BF16)


def _mm_res_ln_kernel(a_ref, w_ref, x_ref, g_ref, b_ref, o_ref, *maybe_ob, alpha):
    k = pl.program_id(1)
    n = o_ref.shape[1]

    @pl.when(k == 0)
    def _():
        o_ref[...] = jnp.zeros(o_ref.shape, F32)

    a = a_ref[...]
    nc = _tile(n, LN_COL_CHUNK)
    for c in range(n // nc):
        sl = slice(c * nc, (c + 1) * nc)
        o_ref[:, sl] += jnp.dot(a, w_ref[:, sl], preferred_element_type=F32)

    @pl.when(k == pl.num_programs(1) - 1)
    def _():
        _residual_ln(x_ref, g_ref, b_ref, o_ref, maybe_ob[0] if maybe_ob else None, alpha)


def _mm_res_ln(a, w, xres, g, b, alpha, *, want_bf16, tm_pref=512, tk_pref=512):
    m, k = a.shape
    n = w.shape[1]
    tm = _tile(m, tm_pref)
    tk = _tile(k, tk_pref)
    out_shape = [jax.ShapeDtypeStruct((m, n), F32)]
    out_specs = [pl.BlockSpec((tm, n), lambda i, kk: (i, 0))]
    if want_bf16:
        out_shape.append(jax.ShapeDtypeStruct((m, n), BF16))
        out_specs.append(pl.BlockSpec((tm, n), lambda i, kk: (i, 0)))
    res = pl.pallas_call(
        functools.partial(_mm_res_ln_kernel, alpha=alpha),
        out_shape=tuple(out_shape),
        grid=(m // tm, k // tk),
        in_specs=[
            pl.BlockSpec((tm, tk), lambda i, kk: (i, kk)),
            pl.BlockSpec((tk, n), lambda i, kk: (kk, 0)),
            pl.BlockSpec((tm, n), lambda i, kk: (i, 0)),
            pl.BlockSpec((1, n), lambda i, kk: (0, 0)),
            pl.BlockSpec((1, n), lambda i, kk: (0, 0)),
        ],
        out_specs=tuple(out_specs),
        compiler_params=_params("parallel", "arbitrary"),
    )(a, w, xres, g.reshape(1, n), b.reshape(1, n))
    return res if want_bf16 else (res[0], None)


def _proj_res_ln_kernel(a_ref, w_ref, x_ref, g_ref, b_ref, o_ref, ob_ref, *, alpha):
    n = o_ref.shape[1]
    a = a_ref[...]
    nc = _tile(n, LN_COL_CHUNK)
    for c in range(n // nc):
        sl = slice(c * nc, (c + 1) * nc)
        o_ref[:, sl] = jnp.dot(a, w_ref[:, sl], preferred_element_type=F32)
    _residual_ln(x_ref, g_ref, b_ref, o_ref, ob_ref, alpha)


def _proj_res_ln(a, w, xres, g, b, alpha, tm_pref=128):
    m, k = a.shape
    n = w.shape[1]
    tm = _tile(m, tm_pref)
    row = lambda width: pl.BlockSpec((tm, width), lambda i: (i, 0))
    vec = pl.BlockSpec((1, n), lambda i: (0, 0))
    return pl.pallas_call(
        functools.partial(_proj_res_ln_kernel, alpha=alpha),
        out_shape=(jax.ShapeDtypeStruct((m, n), F32), jax.ShapeDtypeStruct((m, n), BF16)),
        grid=(m // tm,),
        in_specs=[row(k),
                  pl.BlockSpec((k, n), lambda i: (0, 0), pipeline_mode=pl.Buffered(1)),
                  row(n), vec, vec],
        out_specs=(row(n), row(n)),
        compiler_params=_params("arbitrary"),
    )(a, w, xres, g.reshape(1, n), b.reshape(1, n))


def _ffn_up_kernel(a_ref, wg_ref, wu_ref, o_ref):
    a = a_ref[...]
    g = jnp.dot(a, wg_ref[...], preferred_element_type=F32)
    u = jnp.dot(a, wu_ref[...], preferred_element_type=F32)
    o_ref[...] = (g * jax.nn.sigmoid(g) * u).astype(o_ref.dtype)


def _ffn_up(a, wg, wu, tm_pref=1024, tn_pref=512):
    m, k = a.shape
    n = wg.shape[1]
    tm = _tile(m, tm_pref)
    tn = _tile(n, tn_pref)
    return pl.pallas_call(
        _ffn_up_kernel,
        out_shape=jax.ShapeDtypeStruct((m, n), BF16),
        grid=(m // tm, n // tn),
        in_specs=[pl.BlockSpec((tm, k), lambda i, j: (i, 0)),
                  pl.BlockSpec((k, tn), lambda i, j: (0, j)),
                  pl.BlockSpec((k, tn), lambda i, j: (0, j))],
        out_specs=pl.BlockSpec((tm, tn), lambda i, j: (i, j)),
        compiler_params=_params("parallel", "arbitrary"),
    )(a, wg, wu)


def _mla_in_kernel(a_ref, w_ref, qn_ref, kvn_ref, cos_ref, sin_ref, cq_ref, ckv_ref, kr_ref):
    def rms(t, gain):
        r = lax.rsqrt(jnp.mean(t * t, axis=-1, keepdims=True) + RMS_EPS)
        return t * r * gain

    a = a_ref[...]
    kv0 = Q_LORA + KV_LORA
    cq = jnp.dot(a, w_ref[:, :Q_LORA], preferred_element_type=F32)
    cq_ref[...] = rms(cq, qn_ref[...]).astype(BF16)
    ckv = jnp.dot(a, w_ref[:, Q_LORA:kv0], preferred_element_type=F32)
    ckv_ref[...] = rms(ckv, kvn_ref[...]).astype(BF16)
    y = jnp.dot(a, w_ref[:, kv0:], preferred_element_type=F32)
    kr_ref[...] = _rope_half_lanes(y, cos_ref[...], sin_ref[...]).astype(BF16)


def _mla_in(a, w, q_norm, kv_norm, tables, tm_pref=512):
    m, k = a.shape
    n = w.shape[1]
    tm = _tile(m, tm_pref)
    row = lambda width: pl.BlockSpec((tm, width), lambda i: (i, 0))
    return pl.pallas_call(
        _mla_in_kernel,
        out_shape=(jax.ShapeDtypeStruct((m, Q_LORA), BF16),
                   jax.ShapeDtypeStruct((m, KV_LORA), BF16),
                   jax.ShapeDtypeStruct((m, LANES), BF16)),
        grid=(m // tm,),
        in_specs=[row(k),
                  pl.BlockSpec((k, n), lambda i: (0, 0), pipeline_mode=pl.Buffered(1)),
                  pl.BlockSpec((1, Q_LORA), lambda i: (0, 0)),
                  pl.BlockSpec((1, KV_LORA), lambda i: (0, 0)),
                  row(LANES), row(LANES)],
        out_specs=(row(Q_LORA), row(KV_LORA), row(LANES)),
        compiler_params=_params("arbitrary"),
    )(a, w, q_norm.reshape(1, Q_LORA), kv_norm.reshape(1, KV_LORA), *tables)


QK_DIM = QK_NOPE + QK_ROPE
UP_HEADS_PER_DOT = 4


def _q_up_kernel(a_ref, wn_ref, wr_ref, cos_ref, sin_ref, q_ref, *, heads_per_step, scale):
    a = a_ref[...]
    cos, sin = cos_ref[...], sin_ref[...]
    hpd = UP_HEADS_PER_DOT
    for c in range(heads_per_step // hpd):
        nope = jnp.dot(a, wn_ref[:, c * hpd * QK_NOPE:(c + 1) * hpd * QK_NOPE],
                       preferred_element_type=F32)
        rope = jnp.dot(a, wr_ref[:, c * hpd * QK_ROPE:(c + 1) * hpd * QK_ROPE],
                       preferred_element_type=F32)
        roped = [_rope_half_lanes(rope[:, t * LANES:(t + 1) * LANES], cos, sin) * scale
                 for t in range(hpd * QK_ROPE // LANES)]
        for h in range(hpd):
            tile, half = divmod(h * QK_ROPE, LANES)
            q_ref[c * hpd + h, :, :QK_NOPE] = (
                nope[:, h * QK_NOPE:(h + 1) * QK_NOPE] * scale).astype(BF16)
            q_ref[c * hpd + h, :, QK_NOPE:] = roped[tile][:, half:half + QK_ROPE].astype(BF16)


def _q_up(cq, w_uq_g, tables, heads, scale, tm_pref=1024, hps_pref=8):
    m, k = cq.shape
    tm = _tile(m, tm_pref)
    hps = _tile(heads, hps_pref)
    assert hps % UP_HEADS_PER_DOT == 0 and (UP_HEADS_PER_DOT * QK_ROPE) % LANES == 0
    rope_blk0 = heads * QK_NOPE // (hps * QK_ROPE)
    return pl.pallas_call(
        functools.partial(_q_up_kernel, heads_per_step=hps, scale=scale),
        out_shape=jax.ShapeDtypeStruct((heads, m, QK_DIM), BF16),
        grid=(m // tm, heads // hps),
        in_specs=[pl.BlockSpec((tm, k), lambda i, j: (i, 0)),
                  pl.BlockSpec((k, hps * QK_NOPE), lambda i, j: (0, j)),
                  pl.BlockSpec((k, hps * QK_ROPE), lambda i, j: (0, rope_blk0 + j)),
                  pl.BlockSpec((tm, LANES), lambda i, j: (i, 0)),
                  pl.BlockSpec((tm, LANES), lambda i, j: (i, 0))],
        out_specs=pl.BlockSpec((hps, tm, QK_DIM), lambda i, j: (j, i, 0)),
        compiler_params=_params("parallel", "arbitrary"),
    )(cq, w_uq_g, w_uq_g, *tables)


def _kv_up_kernel(a_ref, w_ref, kr_ref, k_ref, v_ref, *, heads_per_step):
    a = a_ref[...]
    kr = kr_ref[:, :QK_ROPE]
    width = QK_NOPE + V_DIM
    hpd = UP_HEADS_PER_DOT
    for c in range(heads_per_step // hpd):
        acc = jnp.dot(a, w_ref[:, c * hpd * width:(c + 1) * hpd * width],
                      preferred_element_type=F32)
        for h in range(hpd):
            base = h * width
            k_ref[c * hpd + h, :, :QK_NOPE] = acc[:, base:base + QK_NOPE].astype(BF16)
            k_ref[c * hpd + h, :, QK_NOPE:] = kr
            v_ref[c * hpd + h] = acc[:, base + QK_NOPE:base + width].astype(BF16)


def _kv_up(ckv, w_ukv, kr, heads, tm_pref=1024, hps_pref=16):
    m, k = ckv.shape
    tm = _tile(m, tm_pref)
    hps = _tile(heads, hps_pref)
    assert hps % UP_HEADS_PER_DOT == 0
    width = QK_NOPE + V_DIM
    return pl.pallas_call(
        functools.partial(_kv_up_kernel, heads_per_step=hps),
        out_shape=(jax.ShapeDtypeStruct((heads, m, QK_DIM), BF16),
                   jax.ShapeDtypeStruct((heads, m, V_DIM), BF16)),
        grid=(m // tm, heads // hps),
        in_specs=[pl.BlockSpec((tm, k), lambda i, j: (i, 0)),
                  pl.BlockSpec((k, hps * width), lambda i, j: (0, j)),
                  pl.BlockSpec((tm, LANES), lambda i, j: (i, 0))],
        out_specs=(pl.BlockSpec((hps, tm, QK_DIM), lambda i, j: (j, i, 0)),
                   pl.BlockSpec((hps, tm, V_DIM), lambda i, j: (j, i, 0))),
        compiler_params=_params("parallel", "arbitrary"),
    )(ckv, w_ukv, kr)


ATTN_UPDATES_PER_TRIP = 16


def _mla_attn_kernel(q_ref, k_ref, v_ref, o_ref, m_ref, acc_ref, s0_ref, s1_ref, *, blk, nsub):
    qi = pl.program_id(1)
    m_ref[...] = jnp.full(m_ref.shape, NEG, F32)
    acc_ref[...] = jnp.zeros(acc_ref.shape, F32)
    ones = jnp.ones((blk, V_DIM), BF16)
    slots = (s0_ref, s1_ref)

    def scores(sub, j, slot):
        start = pl.multiple_of(j * blk, blk)
        kk = k_ref[0, pl.ds(start, blk), :]
        q = q_ref[0, sub * blk:(sub + 1) * blk, :]
        slot[sub] = lax.dot_general(q, kk, (((1,), (1,)), ((), ())),
                                    preferred_element_type=F32)

    def softmax_pv(sub, j, slot, masked):
        start = pl.multiple_of(j * blk, blk)
        vv = jnp.concatenate([v_ref[0, pl.ds(start, blk), :], ones], axis=1)
        s = slot[sub]
        if masked:
            row = lax.broadcasted_iota(jnp.int32, s.shape, 0)
            col = lax.broadcasted_iota(jnp.int32, s.shape, 1)
            s = jnp.where(col <= row, s, NEG)
        chunks = [s[:, c * LANES:(c + 1) * LANES] for c in range(blk // LANES)]
        mx = functools.reduce(jnp.maximum, chunks)
        m_prev = m_ref[sub]
        m_new = jnp.maximum(m_prev, jnp.max(mx, axis=-1, keepdims=True))
        corr = jnp.exp(m_prev - m_new)
        p = jnp.concatenate([jnp.exp(c - m_new) for c in chunks], axis=1).astype(BF16)
        pv = jnp.dot(p, vv, preferred_element_type=F32)
        acc_ref[sub] = jnp.concatenate([corr, corr], axis=1) * acc_ref[sub] + pv
        m_ref[sub] = m_new

    for sub in range(nsub):
        scores(sub, 0, slots[0])

    def full_blocks(t):
        for c in range(nsub):
            j = t * nsub + c
            for sub in range(nsub):
                scores(sub, j + 1, slots[(c + 1) % 2])
                softmax_pv(sub, j, slots[c % 2], masked=False)

    per_trip = max(1, ATTN_UPDATES_PER_TRIP // (nsub * nsub))

    def body(u, carry):
        for g in range(per_trip):
            full_blocks(per_trip * u + g)
        return carry

    lax.fori_loop(0, qi // per_trip, body, 0)
    for g in range(per_trip - 1):
        @pl.when(qi % per_trip > g)
        def _():
            full_blocks(qi - qi % per_trip + g)

    base = qi * nsub
    for c in range(nsub):
        for sub in range(c, nsub):
            if sub > c:
                scores(sub, base + c + 1, slots[(c + 1) % 2])
            softmax_pv(sub, base + c, slots[c % 2], masked=(sub == c))
    for sub in range(nsub):
        acc = acc_ref[sub]
        o_ref[sub * blk:(sub + 1) * blk, :] = (acc[:, :V_DIM] / acc[:, V_DIM:]).astype(o_ref.dtype)


def _mla_attn(q, k, v, blk_pref=512, nsub_pref=4):
    heads, s, _ = q.shape
    blk = _tile(s, blk_pref)
    nsub = _tile(s // blk, nsub_pref)
    assert nsub % 2 == 0, "score slots alternate with key-block parity"
    tq = blk * nsub
    return pl.pallas_call(
        functools.partial(_mla_attn_kernel, blk=blk, nsub=nsub),
        out_shape=jax.ShapeDtypeStruct((s, heads * V_DIM), BF16),
        grid=(heads, s // tq),
        in_specs=[pl.BlockSpec((1, tq, QK_DIM), lambda h, i: (h, i, 0)),
                  pl.BlockSpec((1, s, QK_DIM), lambda h, i: (h, 0, 0)),
                  pl.BlockSpec((1, s, V_DIM), lambda h, i: (h, 0, 0))],
        out_specs=pl.BlockSpec((tq, V_DIM), lambda h, i: (i, h)),
        scratch_shapes=[pltpu.VMEM((nsub, blk, LANES), F32),
                        pltpu.VMEM((nsub, blk, 2 * V_DIM), F32),
                        pltpu.VMEM((nsub, blk, blk), F32),
                        pltpu.VMEM((nsub, blk, blk), F32)],
        compiler_params=_params("parallel", "arbitrary"),
    )(q, k, v)


def _ffn_block(x, xb, w_gate, w_up, w_down, l, g, b, alpha, want_bf16):
    hidden = w_gate.shape[2]
    hp = _round_up(hidden, 512)
    wg = _cast_pad(w_gate, l, cols_p=hp)
    wu = _cast_pad(w_up, l, cols_p=hp)
    wd = _cast_pad(w_down, l, rows_p=hp)
    hmid = _ffn_up(xb, wg, wu)
    return _mm_res_ln(hmid, wd, x, g, b, alpha, want_bf16=want_bf16)


def _attn_conv_layer(x, xb, w_in, conv_w, w_out, j, g, b, alpha):
    s, d = x.shape
    a_width = d // 2
    hproj = _in_proj(xb, _cast_pad(w_in, j), _rope_tables(s, HEAD_DIM), rope_cols=2 * a_width,
                     scale=HEAD_DIM ** -0.5, scaled_cols=a_width)
    outs, lses = [], []
    for window, dilation in DILATED_PATTERNS:
        assert window // dilation == DIL_BLOCK and s % window == 0
        o, lse = _dilated_branch(hproj, a_width, dilation)
        outs.append(o)
        lses.append(lse)
    ab = _mix(outs, lses, hproj, conv_w[j], a_width)
    return _proj_res_ln(ab, _cast_pad(w_out, j), x, g, b, alpha)


def _mla_layer(x, xb, w_in, q_norm, kv_norm, w_uq, w_ukv, w_out, j, g, b, alpha):
    s, d = x.shape
    heads = d // 128
    w_in_p = _cast_pad(w_in, j, cols_p=w_in.shape[2] + LANES - QK_ROPE)
    cq, ckv, kr = _mla_in(xb, w_in_p, q_norm[j], kv_norm[j], _rope_tables(s, QK_ROPE))
    w_uq_h = w_uq[j].astype(BF16).reshape(Q_LORA, heads, QK_DIM)
    w_uq_g = jnp.concatenate([w_uq_h[:, :, :QK_NOPE].reshape(Q_LORA, heads * QK_NOPE),
                              w_uq_h[:, :, QK_NOPE:].reshape(Q_LORA, heads * QK_ROPE)], axis=1)
    q = _q_up(cq, w_uq_g, _rope_tables(s, QK_ROPE), heads, QK_DIM ** -0.5)
    k, v = _kv_up(ckv, _cast_pad(w_ukv, j), kr, heads)
    o = _mla_attn(q, k, v)
    return _proj_res_ln(o, _cast_pad(w_out, j), x, g, b, alpha)


def kernel(x, w_in_a, conv_w, w_out_a, w_in_c, q_norm, kv_norm, w_uq, w_ukv, w_out_c,
           ln1_g, ln1_b, w_gate, w_up, w_down, ln2_g, ln2_b):
    batch, s, d = x.shape
    depth = ln1_g.shape[0]
    alpha = (2.0 * depth) ** 0.25
    outs = []
    for bi in range(batch):
        xf = x[bi]
        xb = xf.astype(BF16)
        for l in range(depth):
            j = l // 2
            if l % 2 == 0:
                xf, xb = _attn_conv_layer(xf, xb, w_in_a, conv_w, w_out_a, j,
                                          ln1_g[l], ln1_b[l], alpha)
            else:
                xf, xb = _mla_layer(xf, xb, w_in_c, q_norm, kv_norm, w_uq, w_ukv, w_out_c, j,
                                    ln1_g[l], ln1_b[l], alpha)
            xf, xb = _ffn_block(xf, xb, w_gate, w_up, w_down, l, ln2_g[l], ln2_b[l],
                                alpha, want_bf16=l + 1 < depth)
        outs.append(xf)
    return jnp.stack(outs)
```

```python
import functools
import math

import jax
import jax.numpy as jnp
from jax import lax
from jax.experimental import pallas as pl
from jax.experimental.pallas import tpu as pltpu

F32 = jnp.float32
BF16 = jnp.bfloat16

HEAD_DIM = 128
CONV_WIDTH = 3
DILATED_PATTERNS = ((128, 1), (512, 4), (2048, 16))
DIL_BLOCK = 128
Q_LORA = 1536
KV_LORA = 512
QK_NOPE = 128
QK_ROPE = 64
V_DIM = 128
ROPE_THETA = 10000.0
LN_EPS = 1e-5
RMS_EPS = 1e-6
NEG = -1e30

LANES = 128
SUBLANES = 8
VMEM_LIMIT_BYTES = 60 * 1024 * 1024


def _round_up(n, m):
    return -(-n // m) * m


def _tile(n, pref):
    if n <= pref:
        return n
    t = pref
    while n % t:
        t //= 2
    return t


def _params(*sem):
    return pltpu.CompilerParams(dimension_semantics=sem, vmem_limit_bytes=VMEM_LIMIT_BYTES)


def _cast_pad_kernel(w_ref, o_ref, *, rows, cols, rows_p):
    tr, cols_p = o_ref.shape
    val = w_ref[...].astype(BF16)
    if rows_p > rows:
        row = pl.program_id(0) * tr + lax.broadcasted_iota(jnp.int32, val.shape, 0)
        val = jnp.where(row < rows, val, jnp.zeros_like(val))
    o_ref[:, :cols] = val
    if cols_p > cols:
        o_ref[:, cols:] = jnp.zeros((tr, cols_p - cols), BF16)


def _cast_pad(w_stack, layer, rows_p=None, cols_p=None, tr_pref=256):
    _, rows, cols = w_stack.shape
    rows_p = rows_p or rows
    cols_p = cols_p or cols
    tr = _tile(rows_p, tr_pref)
    last = (rows - 1) // tr
    return pl.pallas_call(
        functools.partial(_cast_pad_kernel, rows=rows, cols=cols, rows_p=rows_p),
        out_shape=jax.ShapeDtypeStruct((rows_p, cols_p), BF16),
        grid=(rows_p // tr,),
        in_specs=[pl.BlockSpec((None, tr, cols), lambda i: (layer, jnp.minimum(i, last), 0))],
        out_specs=pl.BlockSpec((tr, cols_p), lambda i: (i, 0)),
        compiler_params=_params("parallel"),
    )(w_stack)


def _rope_tables(seq, dim):
    half = dim // 2
    inv = ROPE_THETA ** (-jnp.arange(half, dtype=F32) * 2.0 / dim)
    ang = jnp.arange(seq).astype(F32)[:, None] * inv[None, :]
    cos, sin = jnp.cos(ang), jnp.sin(ang)
    reps = LANES // dim
    return (jnp.concatenate([cos, cos] * reps, axis=1),
            jnp.concatenate([-sin, sin] * reps, axis=1))


def _rope_full_lanes(t, cos, sin):
    return t * cos + pltpu.roll(t, HEAD_DIM // 2, axis=1) * sin


def _rope_half_lanes(y, cos, sin):
    half = QK_ROPE // 2
    lane = lax.broadcasted_iota(jnp.int32, y.shape, 1)
    partner = jnp.where(lane % QK_ROPE < half, pltpu.roll(y, LANES - half, axis=1),
                        pltpu.roll(y, half, axis=1))
    return y * cos + partner * sin


def _in_proj_kernel(a_ref, w_ref, cos_ref, sin_ref, o_ref, *, scale, n_scaled_tiles,
                    n_rope_tiles):
    j = pl.program_id(1)
    acc = jnp.dot(a_ref[...], w_ref[...], preferred_element_type=F32)

    @pl.when(j >= n_rope_tiles)
    def _():
        o_ref[...] = acc

    @pl.when(j < n_rope_tiles)
    def _():
        cos = cos_ref[...]
        sin = sin_ref[...]
        mult = jnp.where(j < n_scaled_tiles, scale, 1.0).astype(F32)
        for c in range(acc.shape[1] // HEAD_DIM):
            sl = slice(c * HEAD_DIM, (c + 1) * HEAD_DIM)
            o_ref[:, sl] = _rope_full_lanes(acc[:, sl], cos, sin) * mult


def _in_proj(a, w, rope_tables, *, rope_cols, scale, scaled_cols, tm_pref=1024, tn_pref=1024):
    m, k = a.shape
    n = w.shape[1]
    tm = _tile(m, tm_pref)
    tn = _tile(math.gcd(n, rope_cols, scaled_cols), tn_pref)
    return pl.pallas_call(
        functools.partial(_in_proj_kernel, scale=scale, n_scaled_tiles=scaled_cols // tn,
                          n_rope_tiles=rope_cols // tn),
        out_shape=jax.ShapeDtypeStruct((m, n), F32),
        grid=(m // tm, n // tn),
        in_specs=[pl.BlockSpec((tm, k), lambda i, j: (i, 0)),
                  pl.BlockSpec((k, tn), lambda i, j: (0, j)),
                  pl.BlockSpec((tm, LANES), lambda i, j: (i, 0)),
                  pl.BlockSpec((tm, LANES), lambda i, j: (i, 0))],
        out_specs=pl.BlockSpec((tm, tn), lambda i, j: (i, j)),
        compiler_params=_params("parallel", "arbitrary"),
    )(a, w, *rope_tables)


DIL_ROWS = 2048


def _dilated_kernel(q_ref, kp_ref, kc_ref, vp_ref, vc_ref, o_ref, lse_ref, *, d, groups):
    n = pl.program_id(0)
    h = pl.program_id(1)
    blk = DIL_BLOCK
    span = blk * d
    qi = lax.broadcasted_iota(jnp.int32, (blk, 2 * blk), 0)
    ki = lax.broadcasted_iota(jnp.int32, (blk, 2 * blk), 1)
    dist = qi + blk - ki
    band = (dist >= 0) & (dist <= blk)
    band_first = band & ((ki >= blk) | (n > 0))
    lane = lax.broadcasted_iota(jnp.int32, (blk, LANES), 1)

    @pl.when(h == 0)
    def _():
        lse_ref[...] = jnp.zeros(lse_ref.shape, F32)

    def sel(start):
        return pl.ds(start, blk, stride=d) if d > 1 else pl.ds(start, blk)

    for g in range(groups):
        for r in range(d):
            cur = sel(g * span + r)
            q = q_ref[cur, :].astype(BF16)
            if g == 0:
                k_prev, v_prev = kp_ref[sel(r), :], vp_ref[sel(r), :]
            else:
                prev = sel((g - 1) * span + r)
                k_prev, v_prev = kc_ref[prev, :], vc_ref[prev, :]
            kk = jnp.concatenate([k_prev, kc_ref[cur, :]], axis=0).astype(BF16)
            vv = jnp.concatenate([v_prev, vc_ref[cur, :]], axis=0).astype(BF16)
            s = lax.dot_general(q, kk, (((1,), (1,)), ((), ())), preferred_element_type=F32)
            s = jnp.where(band_first if g == 0 else band, s, NEG)
            m = jnp.max(s, axis=-1, keepdims=True)
            p = jnp.exp(s - m)
            den = jnp.sum(p, axis=-1, keepdims=True)
            o = jnp.dot(p.astype(BF16), vv, preferred_element_type=F32)
            o_ref[cur, :] = o / den
            lse_ref[cur, :] = jnp.where(lane == h, m + jnp.log(den), lse_ref[cur, :])


def _dilated_branch(hproj, a_width, dilation):
    s = hproj.shape[0]
    heads = a_width // HEAD_DIM
    d = dilation
    span = DIL_BLOCK * d
    rows = _tile(s, DIL_ROWS)
    groups = rows // span
    assert groups >= 1 and rows % span == 0
    cur = lambda col0: pl.BlockSpec((rows, HEAD_DIM), lambda n, h: (n, col0 + h))
    prev = lambda col0: pl.BlockSpec(
        (span, HEAD_DIM), lambda n, h: (jnp.maximum(n * groups - 1, 0), col0 + h))
    return pl.pallas_call(
        functools.partial(_dilated_kernel, d=d, groups=groups),
        out_shape=(jax.ShapeDtypeStruct((s, a_width), F32),
                   jax.ShapeDtypeStruct((s, LANES), F32)),
        grid=(s // rows, heads),
        in_specs=[cur(0), prev(heads), cur(heads), prev(2 * heads), cur(2 * heads)],
        out_specs=(pl.BlockSpec((rows, HEAD_DIM), lambda n, h: (n, h)),
                   pl.BlockSpec((rows, LANES), lambda n, h: (n, 0))),
        compiler_params=_params("parallel", "arbitrary"),
    )(hproj, hproj, hproj, hproj, hproj)


def _mix_kernel(o1_ref, o2_ref, o3_ref, l1_ref, l2_ref, l3_ref,
                gb_ref, gc_ref, hin_ref, gch_ref, hinh_ref, cw_ref, out_ref, *, heads, a_width):
    i = pl.program_id(0)
    l1, l2, l3 = l1_ref[...], l2_ref[...], l3_ref[...]
    m = jnp.maximum(jnp.maximum(l1, l2), l3)
    e1, e2, e3 = jnp.exp(l1 - m), jnp.exp(l2 - m), jnp.exp(l3 - m)
    inv = 1.0 / (e1 + e2 + e3)
    w1, w2, w3 = e1 * inv, e2 * inv, e3 * inv
    for h in range(heads):
        sl = slice(h * HEAD_DIM, (h + 1) * HEAD_DIM)
        hs = slice(h, h + 1)
        mixed = (w1[:, hs] * o1_ref[:, sl] + w2[:, hs] * o2_ref[:, sl]
                 + w3[:, hs] * o3_ref[:, sl])
        out_ref[:, sl] = mixed.astype(out_ref.dtype)

    u = gc_ref[...] * hin_ref[...]
    halo = gch_ref[...] * hinh_ref[...]
    halo = halo * jnp.where(i > 0, 1.0, 0.0).astype(F32)
    row = lax.broadcasted_iota(jnp.int32, u.shape, 0)
    hm1 = halo[SUBLANES - 1:SUBLANES, :]
    hm2 = halo[SUBLANES - 2:SUBLANES - 1, :]
    u1 = jnp.where(row == 0, hm1, pltpu.roll(u, 1, axis=0))
    u2 = jnp.where(row == 0, hm2, jnp.where(row == 1, hm1, pltpu.roll(u, 2, axis=0)))
    cw = cw_ref[...]
    y = cw[0:1, :] * u2 + cw[1:2, :] * u1 + cw[2:3, :] * u
    out_ref[:, a_width:] = (gb_ref[...] * y).astype(out_ref.dtype)


def _mix(o_list, lse_list, hproj, conv_w, a_width, tm_pref=256):
    s = hproj.shape[0]
    b_width = (hproj.shape[1] - 3 * a_width) // 3
    assert b_width == a_width, "gate blocks are addressed in units of the attention width"
    gb_blk, gc_blk, hin_blk = 3, 4, 5
    heads = a_width // HEAD_DIM
    tm = _tile(s, tm_pref)
    hb = tm // SUBLANES
    row_blk = lambda c: pl.BlockSpec((tm, a_width), lambda i: (i, c))
    halo_blk = lambda c: pl.BlockSpec((SUBLANES, b_width),
                                      lambda i: (jnp.maximum(i * hb - 1, 0), c))
    lse_blk = pl.BlockSpec((tm, LANES), lambda i: (i, 0))
    return pl.pallas_call(
        functools.partial(_mix_kernel, heads=heads, a_width=a_width),
        out_shape=jax.ShapeDtypeStruct((s, a_width + b_width), BF16),
        grid=(s // tm,),
        in_specs=[row_blk(0)] * 3 + [lse_blk] * 3
        + [row_blk(gb_blk), row_blk(gc_blk), row_blk(hin_blk), halo_blk(gc_blk),
           halo_blk(hin_blk), pl.BlockSpec((CONV_WIDTH, b_width), lambda i: (0, 0))],
        out_specs=pl.BlockSpec((tm, a_width + b_width), lambda i: (i, 0)),
        compiler_params=_params("parallel"),
    )(*o_list, *lse_list, hproj, hproj, hproj, hproj, hproj, conv_w)


LN_COL_CHUNK = 1024
LN_ROW_CHUNK = 8


def _residual_ln(x_ref, g_ref, b_ref, o_ref, ob_ref, alpha):
    tm = o_ref.shape[0]
    rc = _tile(tm, LN_ROW_CHUNK)
    for r in range(tm // rc):
        rows = slice(r * rc, (r + 1) * rc)
        z = alpha * x_ref[rows, :] + o_ref[rows, :]
        mu = jnp.mean(z, axis=-1, keepdims=True)
        zc = z - mu
        var = jnp.mean(zc * zc, axis=-1, keepdims=True)
        y = zc * lax.rsqrt(var + LN_EPS) * g_ref[...] + b_ref[...]
        o_ref[rows, :] = y
        if ob_ref is not None:
            ob_ref[rows, :] = y.astype(BF16)


def _mm_res_ln_kernel(a_ref, w_ref, x_ref, g_ref, b_ref, o_ref, *maybe_ob, alpha):
    k = pl.program_id(1)
    n = o_ref.shape[1]

    @pl.when(k == 0)
    def _():
        o_ref[...] = jnp.zeros(o_ref.shape, F32)

    a = a_ref[...]
    nc = _tile(n, LN_COL_CHUNK)
    for c in range(n // nc):
        sl = slice(c * nc, (c + 1) * nc)
        o_ref[:, sl] += jnp.dot(a, w_ref[:, sl], preferred_element_type=F32)

    @pl.when(k == pl.num_programs(1) - 1)
    def _():
        _residual_ln(x_ref, g_ref, b_ref, o_ref, maybe_ob[0] if maybe_ob else None, alpha)


def _mm_res_ln(a, w, xres, g, b, alpha, *, want_bf16, tm_pref=512, tk_pref=512):
    m, k = a.shape
    n = w.shape[1]
    tm = _tile(m, tm_pref)
    tk = _tile(k, tk_pref)
    out_shape = [jax.ShapeDtypeStruct((m, n), F32)]
    out_specs = [pl.BlockSpec((tm, n), lambda i, kk: (i, 0))]
    if want_bf16:
        out_shape.append(jax.ShapeDtypeStruct((m, n), BF16))
        out_specs.append(pl.BlockSpec((tm, n), lambda i, kk: (i, 0)))
    res = pl.pallas_call(
        functools.partial(_mm_res_ln_kernel, alpha=alpha),
        out_shape=tuple(out_shape),
        grid=(m // tm, k // tk),
        in_specs=[
            pl.BlockSpec((tm, tk), lambda i, kk: (i, kk)),
            pl.BlockSpec((tk, n), lambda i, kk: (kk, 0)),
            pl.BlockSpec((tm, n), lambda i, kk: (i, 0)),
            pl.BlockSpec((1, n), lambda i, kk: (0, 0)),
            pl.BlockSpec((1, n), lambda i, kk: (0, 0)),
        ],
        out_specs=tuple(out_specs),
        compiler_params=_params("parallel", "arbitrary"),
    )(a, w, xres, g.reshape(1, n), b.reshape(1, n))
    return res if want_bf16 else (res[0], None)


def _proj_res_ln_kernel(a_ref, w_ref, x_ref, g_ref, b_ref, o_ref, ob_ref, *, alpha):
    n = o_ref.shape[1]
    a = a_ref[...]
    nc = _tile(n, LN_COL_CHUNK)
    for c in range(n // nc):
        sl = slice(c * nc, (c + 1) * nc)
        o_ref[:, sl] = jnp.dot(a, w_ref[:, sl], preferred_element_type=F32)
    _residual_ln(x_ref, g_ref, b_ref, o_ref, ob_ref, alpha)


def _proj_res_ln(a, w, xres, g, b, alpha, tm_pref=128):
    m, k = a.shape
    n = w.shape[1]
    tm = _tile(m, tm_pref)
    row = lambda width: pl.BlockSpec((tm, width), lambda i: (i, 0))
    vec = pl.BlockSpec((1, n), lambda i: (0, 0))
    return pl.pallas_call(
        functools.partial(_proj_res_ln_kernel, alpha=alpha),
        out_shape=(jax.ShapeDtypeStruct((m, n), F32), jax.ShapeDtypeStruct((m, n), BF16)),
        grid=(m // tm,),
        in_specs=[row(k),
                  pl.BlockSpec((k, n), lambda i: (0, 0), pipeline_mode=pl.Buffered(1)),
                  row(n), vec, vec],
        out_specs=(row(n), row(n)),
        compiler_params=_params("arbitrary"),
    )(a, w, xres, g.reshape(1, n), b.reshape(1, n))


def _ffn_up_kernel(a_ref, wg_ref, wu_ref, o_ref):
    a = a_ref[...]
    g = jnp.dot(a, wg_ref[...], preferred_element_type=F32)
    u = jnp.dot(a, wu_ref[...], preferred_element_type=F32)
    o_ref[...] = (g * jax.nn.sigmoid(g) * u).astype(o_ref.dtype)


def _ffn_up(a, wg, wu, tm_pref=1024, tn_pref=512):
    m, k = a.shape
    n = wg.shape[1]
    tm = _tile(m, tm_pref)
    tn = _tile(n, tn_pref)
    return pl.pallas_call(
        _ffn_up_kernel,
        out_shape=jax.ShapeDtypeStruct((m, n), BF16),
        grid=(m // tm, n // tn),
        in_specs=[pl.BlockSpec((tm, k), lambda i, j: (i, 0)),
                  pl.BlockSpec((k, tn), lambda i, j: (0, j)),
                  pl.BlockSpec((k, tn), lambda i, j: (0, j))],
        out_specs=pl.BlockSpec((tm, tn), lambda i, j: (i, j)),
        compiler_params=_params("parallel", "arbitrary"),
    )(a, wg, wu)


def _mla_in_kernel(a_ref, w_ref, qn_ref, kvn_ref, cos_ref, sin_ref, cq_ref, ckv_ref, kr_ref):
    def rms(t, gain):
        r = lax.rsqrt(jnp.mean(t * t, axis=-1, keepdims=True) + RMS_EPS)
        return t * r * gain

    a = a_ref[...]
    kv0 = Q_LORA + KV_LORA
    cq = jnp.dot(a, w_ref[:, :Q_LORA], preferred_element_type=F32)
    cq_ref[...] = rms(cq, qn_ref[...]).astype(BF16)
    ckv = jnp.dot(a, w_ref[:, Q_LORA:kv0], preferred_element_type=F32)
    ckv_ref[...] = rms(ckv, kvn_ref[...]).astype(BF16)
    y = jnp.dot(a, w_ref[:, kv0:], preferred_element_type=F32)
    kr_ref[...] = _rope_half_lanes(y, cos_ref[...], sin_ref[...]).astype(BF16)


def _mla_in(a, w, q_norm, kv_norm, tables, tm_pref=512):
    m, k = a.shape
    n = w.shape[1]
    tm = _tile(m, tm_pref)
    row = lambda width: pl.BlockSpec((tm, width), lambda i: (i, 0))
    return pl.pallas_call(
        _mla_in_kernel,
        out_shape=(jax.ShapeDtypeStruct((m, Q_LORA), BF16),
                   jax.ShapeDtypeStruct((m, KV_LORA), BF16),
                   jax.ShapeDtypeStruct((m, LANES), BF16)),
        grid=(m // tm,),
        in_specs=[row(k),
                  pl.BlockSpec((k, n), lambda i: (0, 0), pipeline_mode=pl.Buffered(1)),
                  pl.BlockSpec((1, Q_LORA), lambda i: (0, 0)),
                  pl.BlockSpec((1, KV_LORA), lambda i: (0, 0)),
                  row(LANES), row(LANES)],
        out_specs=(row(Q_LORA), row(KV_LORA), row(LANES)),
        compiler_params=_params("arbitrary"),
    )(a, w, q_norm.reshape(1, Q_LORA), kv_norm.reshape(1, KV_LORA), *tables)


QK_DIM = QK_NOPE + QK_ROPE
UP_HEADS_PER_DOT = 4


ATTN_BLOCK = 512
ONES_ROWS = 16


def _rope_tables_t(seq, dim):
    half = dim // 2
    inv = ROPE_THETA ** (-jnp.arange(half, dtype=F32) * 2.0 / dim)
    ang = inv[:, None] * jnp.arange(seq).astype(F32)[None, :]
    return jnp.cos(ang), jnp.sin(ang)


def _q_up_t_kernel(a_ref, wn_ref, wr_ref, cos_ref, sin_ref, q_ref, *, heads_per_step, scale):
    a = a_ref[...]
    cos, sin = cos_ref[...], sin_ref[...]
    hpd = UP_HEADS_PER_DOT
    half = QK_ROPE // 2
    tn = (((0,), (1,)), ((), ()))
    for c in range(heads_per_step // hpd):
        nope = lax.dot_general(wn_ref[:, c * hpd * QK_NOPE:(c + 1) * hpd * QK_NOPE], a, tn,
                               preferred_element_type=F32)
        rope = lax.dot_general(wr_ref[:, c * hpd * QK_ROPE:(c + 1) * hpd * QK_ROPE], a, tn,
                               preferred_element_type=F32)
        for h in range(hpd):
            y1 = rope[h * QK_ROPE:h * QK_ROPE + half, :]
            y2 = rope[h * QK_ROPE + half:(h + 1) * QK_ROPE, :]
            q_ref[c * hpd + h, :QK_NOPE, :] = (
                nope[h * QK_NOPE:(h + 1) * QK_NOPE, :] * scale).astype(BF16)
            q_ref[c * hpd + h, QK_NOPE:QK_NOPE + half, :] = (
                (y1 * cos - y2 * sin) * scale).astype(BF16)
            q_ref[c * hpd + h, QK_NOPE + half:, :] = ((y2 * cos + y1 * sin) * scale).astype(BF16)


def _q_up_t(cq, w_uq_g, tables_t, heads, scale, tm_pref=1024, hps_pref=8):
    m, k = cq.shape
    tm = _tile(m, tm_pref)
    hps = _tile(heads, hps_pref)
    assert hps % UP_HEADS_PER_DOT == 0
    half = QK_ROPE // 2
    rope_blk0 = heads * QK_NOPE // (hps * QK_ROPE)
    return pl.pallas_call(
        functools.partial(_q_up_t_kernel, heads_per_step=hps, scale=scale),
        out_shape=jax.ShapeDtypeStruct((heads, QK_DIM, m), BF16),
        grid=(m // tm, heads // hps),
        in_specs=[pl.BlockSpec((tm, k), lambda i, j: (i, 0)),
                  pl.BlockSpec((k, hps * QK_NOPE), lambda i, j: (0, j)),
                  pl.BlockSpec((k, hps * QK_ROPE), lambda i, j: (0, rope_blk0 + j)),
                  pl.BlockSpec((half, tm), lambda i, j: (0, i)),
                  pl.BlockSpec((half, tm), lambda i, j: (0, i))],
        out_specs=pl.BlockSpec((hps, QK_DIM, tm), lambda i, j: (j, 0, i)),
        compiler_params=_params("parallel", "arbitrary"),
    )(cq, w_uq_g, w_uq_g, *tables_t)


def _kv_up_t_kernel(a_ref, wk_ref, wv_ref, kr_ref, k_ref, v_ref, *, heads_per_step, blk):
    a = a_ref[...]
    kr = kr_ref[:, :QK_ROPE]
    hpd = UP_HEADS_PER_DOT
    tn = (((0,), (1,)), ((), ()))
    for c in range(heads_per_step // hpd):
        kn = jnp.dot(a, wk_ref[:, c * hpd * QK_NOPE:(c + 1) * hpd * QK_NOPE],
                     preferred_element_type=F32)
        vt = lax.dot_general(wv_ref[:, c * hpd * V_DIM:(c + 1) * hpd * V_DIM], a, tn,
                             preferred_element_type=F32)
        for h in range(hpd):
            k_ref[c * hpd + h, :, :QK_NOPE] = kn[:, h * QK_NOPE:(h + 1) * QK_NOPE].astype(BF16)
            k_ref[c * hpd + h, :, QK_NOPE:] = kr
            for b in range(a.shape[0] // blk):
                v_ref[c * hpd + h, b] = vt[h * V_DIM:(h + 1) * V_DIM,
                                           b * blk:(b + 1) * blk].astype(BF16)


def _kv_up_t(ckv, w_ukv_g, kr, heads, blk, tm_pref=1024, hps_pref=16):
    m, k = ckv.shape
    tm = _tile(m, tm_pref)
    hps = _tile(heads, hps_pref)
    assert hps % UP_HEADS_PER_DOT == 0 and tm % blk == 0
    v_blk0 = heads * QK_NOPE // (hps * V_DIM)
    return pl.pallas_call(
        functools.partial(_kv_up_t_kernel, heads_per_step=hps, blk=blk),
        out_shape=(jax.ShapeDtypeStruct((heads, m, QK_DIM), BF16),
                   jax.ShapeDtypeStruct((heads, m // blk, V_DIM, blk), BF16)),
        grid=(m // tm, heads // hps),
        in_specs=[pl.BlockSpec((tm, k), lambda i, j: (i, 0)),
                  pl.BlockSpec((k, hps * QK_NOPE), lambda i, j: (0, j)),
                  pl.BlockSpec((k, hps * V_DIM), lambda i, j: (0, v_blk0 + j)),
                  pl.BlockSpec((tm, LANES), lambda i, j: (i, 0))],
        out_specs=(pl.BlockSpec((hps, tm, QK_DIM), lambda i, j: (j, i, 0)),
                   pl.BlockSpec((hps, tm // blk, V_DIM, blk), lambda i, j: (j, i, 0, 0))),
        compiler_params=_params("parallel", "arbitrary"),
    )(ckv, w_ukv_g, w_ukv_g, kr)


def _mla_attn_t_kernel(q_ref, k_ref, v_ref, o_ref, m_ref, acc_ref, s0_ref, s1_ref, *, blk, nsub):
    qi = pl.program_id(1)
    m_ref[...] = jnp.full(m_ref.shape, NEG, F32)
    acc_ref[...] = jnp.zeros(acc_ref.shape, F32)
    ones = jnp.ones((ONES_ROWS, blk), BF16)
    slots = (s0_ref, s1_ref)

    def scores(sub, j, slot):
        start = pl.multiple_of(j * blk, blk)
        kk = k_ref[0, pl.ds(start, blk), :]
        qt = q_ref[0, :, sub * blk:(sub + 1) * blk]
        slot[sub] = jnp.dot(kk, qt, preferred_element_type=F32)

    def softmax_pv(sub, j, slot, masked):
        vt = jnp.concatenate([v_ref[0, j], ones], axis=0)
        s = slot[sub]
        if masked:
            key = lax.broadcasted_iota(jnp.int32, s.shape, 0)
            qry = lax.broadcasted_iota(jnp.int32, s.shape, 1)
            s = jnp.where(key <= qry, s, NEG)
        m_prev = m_ref[sub]
        m_new = jnp.maximum(m_prev, jnp.max(s, axis=0, keepdims=True))
        corr = jnp.exp(m_prev - m_new)
        p = jnp.exp(s - m_new).astype(BF16)
        acc_ref[sub] = corr * acc_ref[sub] + jnp.dot(vt, p, preferred_element_type=F32)
        m_ref[sub] = m_new

    for sub in range(nsub):
        scores(sub, 0, slots[0])

    def full_blocks(t):
        for c in range(nsub):
            j = t * nsub + c
            for sub in range(nsub):
                scores(sub, j + 1, slots[(c + 1) % 2])
                softmax_pv(sub, j, slots[c % 2], masked=False)

    def body(t, carry):
        full_blocks(t)
        return carry

    lax.fori_loop(0, qi, body, 0)
    base = qi * nsub
    for c in range(nsub):
        for sub in range(c, nsub):
            if sub > c:
                scores(sub, base + c + 1, slots[(c + 1) % 2])
            softmax_pv(sub, base + c, slots[c % 2], masked=(sub == c))
    for sub in range(nsub):
        acc = acc_ref[sub]
        o_t = acc[:V_DIM, :] / acc[V_DIM:V_DIM + 1, :]
        o_ref[sub * blk:(sub + 1) * blk, :] = o_t.T.astype(o_ref.dtype)


def _mla_attn_t(q_t, k, v_t, nsub_pref=4):
    heads, s, _ = k.shape
    blk = v_t.shape[3]
    nsub = _tile(s // blk, nsub_pref)
    assert nsub % 2 == 0, "score slots alternate with key-block parity"
    tq = blk * nsub
    return pl.pallas_call(
        functools.partial(_mla_attn_t_kernel, blk=blk, nsub=nsub),
        out_shape=jax.ShapeDtypeStruct((s, heads * V_DIM), BF16),
        grid=(heads, s // tq),
        in_specs=[pl.BlockSpec((1, QK_DIM, tq), lambda h, i: (h, 0, i)),
                  pl.BlockSpec((1, s, QK_DIM), lambda h, i: (h, 0, 0)),
                  pl.BlockSpec((1, s // blk, V_DIM, blk), lambda h, i: (h, 0, 0, 0))],
        out_specs=pl.BlockSpec((tq, V_DIM), lambda h, i: (i, h)),
        scratch_shapes=[pltpu.VMEM((nsub, 1, blk), F32),
                        pltpu.VMEM((nsub, V_DIM + ONES_ROWS, blk), F32),
                        pltpu.VMEM((nsub, blk, blk), F32),
                        pltpu.VMEM((nsub, blk, blk), F32)],
        compiler_params=_params("parallel", "arbitrary"),
    )(q_t, k, v_t)


def _ffn_block(x, xb, w_gate, w_up, w_down, l, g, b, alpha, want_bf16):
    hidden = w_gate.shape[2]
    hp = _round_up(hidden, 512)
    wg = _cast_pad(w_gate, l, cols_p=hp)
    wu = _cast_pad(w_up, l, cols_p=hp)
    wd = _cast_pad(w_down, l, rows_p=hp)
    hmid = _ffn_up(xb, wg, wu)
    return _mm_res_ln(hmid, wd, x, g, b, alpha, want_bf16=want_bf16)


def _attn_conv_layer(x, xb, w_in, conv_w, w_out, j, g, b, alpha):
    s, d = x.shape
    a_width = d // 2
    hproj = _in_proj(xb, _cast_pad(w_in, j), _rope_tables(s, HEAD_DIM), rope_cols=2 * a_width,
                     scale=HEAD_DIM ** -0.5, scaled_cols=a_width)
    outs, lses = [], []
    for window, dilation in DILATED_PATTERNS:
        assert window // dilation == DIL_BLOCK and s % window == 0
        o, lse = _dilated_branch(hproj, a_width, dilation)
        outs.append(o)
        lses.append(lse)
    ab = _mix(outs, lses, hproj, conv_w[j], a_width)
    return _proj_res_ln(ab, _cast_pad(w_out, j), x, g, b, alpha)


def _mla_layer(x, xb, w_in, q_norm, kv_norm, w_uq, w_ukv, w_out, j, g, b, alpha):
    s, d = x.shape
    heads = d // 128
    w_in_p = _cast_pad(w_in, j, cols_p=w_in.shape[2] + LANES - QK_ROPE)
    cq, ckv, kr = _mla_in(xb, w_in_p, q_norm[j], kv_norm[j], _rope_tables(s, QK_ROPE))
    w_uq_h = w_uq[j].astype(BF16).reshape(Q_LORA, heads, QK_DIM)
    w_uq_g = jnp.concatenate([w_uq_h[:, :, :QK_NOPE].reshape(Q_LORA, heads * QK_NOPE),
                              w_uq_h[:, :, QK_NOPE:].reshape(Q_LORA, heads * QK_ROPE)], axis=1)
    q_t = _q_up_t(cq, w_uq_g, _rope_tables_t(s, QK_ROPE), heads, QK_DIM ** -0.5)
    w_ukv_h = w_ukv[j].astype(BF16).reshape(KV_LORA, heads, QK_NOPE + V_DIM)
    w_ukv_g = jnp.concatenate([w_ukv_h[:, :, :QK_NOPE].reshape(KV_LORA, heads * QK_NOPE),
                               w_ukv_h[:, :, QK_NOPE:].reshape(KV_LORA, heads * V_DIM)], axis=1)
    k, v_t = _kv_up_t(ckv, w_ukv_g, kr, heads, _tile(s, ATTN_BLOCK))
    o = _mla_attn_t(q_t, k, v_t)
    return _proj_res_ln(o, _cast_pad(w_out, j), x, g, b, alpha)


def kernel(x, w_in_a, conv_w, w_out_a, w_in_c, q_norm, kv_norm, w_uq, w_ukv, w_out_c,
           ln1_g, ln1_b, w_gate, w_up, w_down, ln2_g, ln2_b):
    batch, s, d = x.shape
    depth = ln1_g.shape[0]
    alpha = (2.0 * depth) ** 0.25
    outs = []
    for bi in range(batch):
        xf = x[bi]
        xb = xf.astype(BF16)
        for l in range(depth):
            j = l // 2
            if l % 2 == 0:
                xf, xb = _attn_conv_layer(xf, xb, w_in_a, conv_w, w_out_a, j,
                                          ln1_g[l], ln1_b[l], alpha)
            else:
                xf, xb = _mla_layer(xf, xb, w_in_c, q_norm, kv_norm, w_uq, w_ukv, w_out_c, j,
                                    ln1_g[l], ln1_b[l], alpha)
            xf, xb = _ffn_block(xf, xb, w_gate, w_up, w_down, l, ln2_g[l], ln2_b[l],
                                alpha, want_bf16=l + 1 < depth)
        outs.append(xf)
    return jnp.stack(outs)
```

```python
import functools
import math

import jax
import jax.numpy as jnp
from jax import lax
from jax.experimental import pallas as pl
from jax.experimental.pallas import tpu as pltpu

F32 = jnp.float32
BF16 = jnp.bfloat16

HEAD_DIM = 128
CONV_WIDTH = 3
DILATED_PATTERNS = ((128, 1), (512, 4), (2048, 16))
DIL_BLOCK = 128
Q_LORA = 1536
KV_LORA = 512
QK_NOPE = 128
QK_ROPE = 64
V_DIM = 128
ROPE_THETA = 10000.0
LN_EPS = 1e-5
RMS_EPS = 1e-6
NEG = -1e30

LANES = 128
SUBLANES = 8
VMEM_LIMIT_BYTES = 60 * 1024 * 1024


def _round_up(n, m):
    return -(-n // m) * m


def _tile(n, pref):
    if n <= pref:
        return n
    t = pref
    while n % t:
        t //= 2
    return t


def _params(*sem):
    return pltpu.CompilerParams(dimension_semantics=sem, vmem_limit_bytes=VMEM_LIMIT_BYTES)


def _cast_pad_kernel(w_ref, o_ref, *, rows, cols, rows_p):
    tr, cols_p = o_ref.shape
    val = w_ref[...].astype(BF16)
    if rows_p > rows:
        row = pl.program_id(0) * tr + lax.broadcasted_iota(jnp.int32, val.shape, 0)
        val = jnp.where(row < rows, val, jnp.zeros_like(val))
    o_ref[:, :cols] = val
    if cols_p > cols:
        o_ref[:, cols:] = jnp.zeros((tr, cols_p - cols), BF16)


def _cast_pad(w_stack, layer, rows_p=None, cols_p=None, tr_pref=256):
    _, rows, cols = w_stack.shape
    rows_p = rows_p or rows
    cols_p = cols_p or cols
    tr = _tile(rows_p, tr_pref)
    last = (rows - 1) // tr
    return pl.pallas_call(
        functools.partial(_cast_pad_kernel, rows=rows, cols=cols, rows_p=rows_p),
        out_shape=jax.ShapeDtypeStruct((rows_p, cols_p), BF16),
        grid=(rows_p // tr,),
        in_specs=[pl.BlockSpec((None, tr, cols), lambda i: (layer, jnp.minimum(i, last), 0))],
        out_specs=pl.BlockSpec((tr, cols_p), lambda i: (i, 0)),
        compiler_params=_params("parallel"),
    )(w_stack)


def _rope_tables(seq, dim):
    half = dim // 2
    inv = ROPE_THETA ** (-jnp.arange(half, dtype=F32) * 2.0 / dim)
    ang = jnp.arange(seq).astype(F32)[:, None] * inv[None, :]
    cos, sin = jnp.cos(ang), jnp.sin(ang)
    reps = LANES // dim
    return (jnp.concatenate([cos, cos] * reps, axis=1),
            jnp.concatenate([-sin, sin] * reps, axis=1))


def _rope_full_lanes(t, cos, sin):
    return t * cos + pltpu.roll(t, HEAD_DIM // 2, axis=1) * sin


def _rope_half_lanes(y, cos, sin):
    half = QK_ROPE // 2
    lane = lax.broadcasted_iota(jnp.int32, y.shape, 1)
    partner = jnp.where(lane % QK_ROPE < half, pltpu.roll(y, LANES - half, axis=1),
                        pltpu.roll(y, half, axis=1))
    return y * cos + partner * sin


def _in_proj_kernel(a_ref, w_ref, cos_ref, sin_ref, o_ref, *, scale, n_scaled_tiles,
                    n_rope_tiles):
    j = pl.program_id(1)
    acc = jnp.dot(a_ref[...], w_ref[...], preferred_element_type=F32)

    @pl.when(j >= n_rope_tiles)
    def _():
        o_ref[...] = acc

    @pl.when(j < n_rope_tiles)
    def _():
        cos = cos_ref[...]
        sin = sin_ref[...]
        mult = jnp.where(j < n_scaled_tiles, scale, 1.0).astype(F32)
        for c in range(acc.shape[1] // HEAD_DIM):
            sl = slice(c * HEAD_DIM, (c + 1) * HEAD_DIM)
            o_ref[:, sl] = _rope_full_lanes(acc[:, sl], cos, sin) * mult


def _in_proj(a, w, rope_tables, *, rope_cols, scale, scaled_cols, tm_pref=1024, tn_pref=1024):
    m, k = a.shape
    n = w.shape[1]
    tm = _tile(m, tm_pref)
    tn = _tile(math.gcd(n, rope_cols, scaled_cols), tn_pref)
    return pl.pallas_call(
        functools.partial(_in_proj_kernel, scale=scale, n_scaled_tiles=scaled_cols // tn,
                          n_rope_tiles=rope_cols // tn),
        out_shape=jax.ShapeDtypeStruct((m, n), F32),
        grid=(m // tm, n // tn),
        in_specs=[pl.BlockSpec((tm, k), lambda i, j: (i, 0)),
                  pl.BlockSpec((k, tn), lambda i, j: (0, j)),
                  pl.BlockSpec((tm, LANES), lambda i, j: (i, 0)),
                  pl.BlockSpec((tm, LANES), lambda i, j: (i, 0))],
        out_specs=pl.BlockSpec((tm, tn), lambda i, j: (i, j)),
        compiler_params=_params("parallel", "arbitrary"),
    )(a, w, *rope_tables)


DIL_ROWS = 2048


DIL_STAGE_STRIDE = 4


def _dilated_kernel(q_ref, kp_ref, kc_ref, vp_ref, vc_ref, o_ref, lse_ref, *stage, d, groups):
    n = pl.program_id(0)
    h = pl.program_id(1)
    blk = DIL_BLOCK
    span = blk * d
    qi = lax.broadcasted_iota(jnp.int32, (blk, 2 * blk), 0)
    ki = lax.broadcasted_iota(jnp.int32, (blk, 2 * blk), 1)
    dist = qi + blk - ki
    band = (dist >= 0) & (dist <= blk)
    band_first = band & ((ki >= blk) | (n > 0))
    lane = lax.broadcasted_iota(jnp.int32, (blk, LANES), 1)

    @pl.when(h == 0)
    def _():
        lse_ref[...] = jnp.zeros(lse_ref.shape, F32)

    inner = DIL_STAGE_STRIDE if stage else 1
    if stage:
        sq, skp, skc, svp, svc, so, sl = stage
        for src, dst in ((q_ref, sq), (kp_ref, skp), (kc_ref, skc), (vp_ref, svp),
                         (vc_ref, svc), (lse_ref, sl)):
            for rr in range(inner):
                dst[rr] = src[pl.ds(rr, src.shape[0] // inner, stride=inner), :]
    else:
        sq, skp, skc, svp, svc, so, sl = q_ref, kp_ref, kc_ref, vp_ref, vc_ref, o_ref, lse_ref

    def rows_of(ref, start, r):
        if not stage:
            return ref, (pl.ds(start + r, blk, stride=d) if d > 1 else pl.ds(start + r, blk))
        return ref.at[r % inner], pl.ds(start // inner + r // inner, blk, stride=d // inner)

    def read(ref, start, r):
        view, idx = rows_of(ref, start, r)
        return view[idx, :]

    for g in range(groups):
        for r in range(d):
            q = read(sq, g * span, r).astype(BF16)
            if g == 0:
                k_prev, v_prev = read(skp, 0, r), read(svp, 0, r)
            else:
                k_prev, v_prev = read(skc, (g - 1) * span, r), read(svc, (g - 1) * span, r)
            kk = jnp.concatenate([k_prev, read(skc, g * span, r)], axis=0).astype(BF16)
            vv = jnp.concatenate([v_prev, read(svc, g * span, r)], axis=0).astype(BF16)
            s = lax.dot_general(q, kk, (((1,), (1,)), ((), ())), preferred_element_type=F32)
            s = jnp.where(band_first if g == 0 else band, s, NEG)
            m = jnp.max(s, axis=-1, keepdims=True)
            p = jnp.exp(s - m)
            den = jnp.sum(p, axis=-1, keepdims=True)
            o = jnp.dot(p.astype(BF16), vv, preferred_element_type=F32)
            o_view, idx = rows_of(so, g * span, r)
            o_view[idx, :] = o / den
            l_view, idx = rows_of(sl, g * span, r)
            l_view[idx, :] = jnp.where(lane == h, m + jnp.log(den), l_view[idx, :])

    if stage:
        for rr in range(inner):
            dst_rows = pl.ds(rr, o_ref.shape[0] // inner, stride=inner)
            o_ref[dst_rows, :] = so[rr]
            lse_ref[dst_rows, :] = sl[rr]


def _dilated_branch(hproj, a_width, dilation):
    s = hproj.shape[0]
    heads = a_width // HEAD_DIM
    d = dilation
    span = DIL_BLOCK * d
    rows = _tile(s, DIL_ROWS)
    groups = rows // span
    assert groups >= 1 and rows % span == 0
    cur = lambda col0: pl.BlockSpec((rows, HEAD_DIM), lambda n, h: (n, col0 + h))
    prev = lambda col0: pl.BlockSpec(
        (span, HEAD_DIM), lambda n, h: (jnp.maximum(n * groups - 1, 0), col0 + h))
    stage = []
    if d % (4 * DIL_STAGE_STRIDE) == 0:
        staged = lambda nrows: pltpu.VMEM((DIL_STAGE_STRIDE, nrows // DIL_STAGE_STRIDE, HEAD_DIM), F32)
        stage = [staged(rows), staged(span), staged(rows), staged(span), staged(rows),
                 staged(rows), staged(rows)]
    return pl.pallas_call(
        functools.partial(_dilated_kernel, d=d, groups=groups),
        out_shape=(jax.ShapeDtypeStruct((s, a_width), F32),
                   jax.ShapeDtypeStruct((s, LANES), F32)),
        grid=(s // rows, heads),
        in_specs=[cur(0), prev(heads), cur(heads), prev(2 * heads), cur(2 * heads)],
        out_specs=(pl.BlockSpec((rows, HEAD_DIM), lambda n, h: (n, h)),
                   pl.BlockSpec((rows, LANES), lambda n, h: (n, 0))),
        scratch_shapes=stage,
        compiler_params=_params("parallel", "arbitrary"),
    )(hproj, hproj, hproj, hproj, hproj)


def _mix_kernel(o1_ref, o2_ref, o3_ref, l1_ref, l2_ref, l3_ref,
                gb_ref, gc_ref, hin_ref, gch_ref, hinh_ref, cw_ref, out_ref, *, heads, a_width):
    i = pl.program_id(0)
    l1, l2, l3 = l1_ref[...], l2_ref[...], l3_ref[...]
    m = jnp.maximum(jnp.maximum(l1, l2), l3)
    e1, e2, e3 = jnp.exp(l1 - m), jnp.exp(l2 - m), jnp.exp(l3 - m)
    inv = 1.0 / (e1 + e2 + e3)
    w1, w2, w3 = e1 * inv, e2 * inv, e3 * inv
    for h in range(heads):
        sl = slice(h * HEAD_DIM, (h + 1) * HEAD_DIM)
        hs = slice(h, h + 1)
        mixed = (w1[:, hs] * o1_ref[:, sl] + w2[:, hs] * o2_ref[:, sl]
                 + w3[:, hs] * o3_ref[:, sl])
        out_ref[:, sl] = mixed.astype(out_ref.dtype)

    u = gc_ref[...] * hin_ref[...]
    halo = gch_ref[...] * hinh_ref[...]
    halo = halo * jnp.where(i > 0, 1.0, 0.0).astype(F32)
    row = lax.broadcasted_iota(jnp.int32, u.shape, 0)
    hm1 = halo[SUBLANES - 1:SUBLANES, :]
    hm2 = halo[SUBLANES - 2:SUBLANES - 1, :]
    u1 = jnp.where(row == 0, hm1, pltpu.roll(u, 1, axis=0))
    u2 = jnp.where(row == 0, hm2, jnp.where(row == 1, hm1, pltpu.roll(u, 2, axis=0)))
    cw = cw_ref[...]
    y = cw[0:1, :] * u2 + cw[1:2, :] * u1 + cw[2:3, :] * u
    out_ref[:, a_width:] = (gb_ref[...] * y).astype(out_ref.dtype)


def _mix(o_list, lse_list, hproj, conv_w, a_width, tm_pref=256):
    s = hproj.shape[0]
    b_width = (hproj.shape[1] - 3 * a_width) // 3
    assert b_width == a_width, "gate blocks are addressed in units of the attention width"
    gb_blk, gc_blk, hin_blk = 3, 4, 5
    heads = a_width // HEAD_DIM
    tm = _tile(s, tm_pref)
    hb = tm // SUBLANES
    row_blk = lambda c: pl.BlockSpec((tm, a_width), lambda i: (i, c))
    halo_blk = lambda c: pl.BlockSpec((SUBLANES, b_width),
                                      lambda i: (jnp.maximum(i * hb - 1, 0), c))
    lse_blk = pl.BlockSpec((tm, LANES), lambda i: (i, 0))
    return pl.pallas_call(
        functools.partial(_mix_kernel, heads=heads, a_width=a_width),
        out_shape=jax.ShapeDtypeStruct((s, a_width + b_width), BF16),
        grid=(s // tm,),
        in_specs=[row_blk(0)] * 3 + [lse_blk] * 3
        + [row_blk(gb_blk), row_blk(gc_blk), row_blk(hin_blk), halo_blk(gc_blk),
           halo_blk(hin_blk), pl.BlockSpec((CONV_WIDTH, b_width), lambda i: (0, 0))],
        out_specs=pl.BlockSpec((tm, a_width + b_width), lambda i: (i, 0)),
        compiler_params=_params("parallel"),
    )(*o_list, *lse_list, hproj, hproj, hproj, hproj, hproj, conv_w)


LN_COL_CHUNK = 1024
LN_ROW_CHUNK = 8


def _residual_ln(x_ref, g_ref, b_ref, o_ref, ob_ref, alpha):
    tm = o_ref.shape[0]
    rc = _tile(tm, LN_ROW_CHUNK)
    for r in range(tm // rc):
        rows = slice(r * rc, (r + 1) * rc)
        z = alpha * x_ref[rows, :] + o_ref[rows, :]
        mu = jnp.mean(z, axis=-1, keepdims=True)
        zc = z - mu
        var = jnp.mean(zc * zc, axis=-1, keepdims=True)
        y = zc * lax.rsqrt(var + LN_EPS) * g_ref[...] + b_ref[...]
        o_ref[rows, :] = y
        if ob_ref is not None:
            ob_ref[rows, :] = y.astype(BF16)


def _mm_res_ln_kernel(a_ref, w_ref, x_ref, g_ref, b_ref, o_ref, *maybe_ob, alpha):
    k = pl.program_id(1)
    n = o_ref.shape[1]

    nc = _tile(n, LN_COL_CHUNK)

    def partial_products(accumulate):
        a = a_ref[...]
        for c in range(n // nc):
            sl = slice(c * nc, (c + 1) * nc)
            part = jnp.dot(a, w_ref[:, sl], preferred_element_type=F32)
            o_ref[:, sl] = o_ref[:, sl] + part if accumulate else part

    @pl.when(k == 0)
    def _():
        partial_products(accumulate=False)

    @pl.when(k > 0)
    def _():
        partial_products(accumulate=True)

    @pl.when(k == pl.num_programs(1) - 1)
    def _():
        _residual_ln(x_ref, g_ref, b_ref, o_ref, maybe_ob[0] if maybe_ob else None, alpha)


def _mm_res_ln(a, w, xres, g, b, alpha, *, want_bf16, tm_pref=512, tk_pref=512):
    m, k = a.shape
    n = w.shape[1]
    tm = _tile(m, tm_pref)
    tk = _tile(k, tk_pref)
    out_shape = [jax.ShapeDtypeStruct((m, n), F32)]
    out_specs = [pl.BlockSpec((tm, n), lambda i, kk: (i, 0))]
    if want_bf16:
        out_shape.append(jax.ShapeDtypeStruct((m, n), BF16))
        out_specs.append(pl.BlockSpec((tm, n), lambda i, kk: (i, 0)))
    res = pl.pallas_call(
        functools.partial(_mm_res_ln_kernel, alpha=alpha),
        out_shape=tuple(out_shape),
        grid=(m // tm, k // tk),
        in_specs=[
            pl.BlockSpec((tm, tk), lambda i, kk: (i, kk)),
            pl.BlockSpec((tk, n), lambda i, kk: (kk, 0)),
            pl.BlockSpec((tm, n), lambda i, kk: (i, 0)),
            pl.BlockSpec((1, n), lambda i, kk: (0, 0)),
            pl.BlockSpec((1, n), lambda i, kk: (0, 0)),
        ],
        out_specs=tuple(out_specs),
        compiler_params=_params("parallel", "arbitrary"),
    )(a, w, xres, g.reshape(1, n), b.reshape(1, n))
    return res if want_bf16 else (res[0], None)


def _proj_res_ln_kernel(a_ref, w_ref, x_ref, g_ref, b_ref, o_ref, ob_ref, *, alpha):
    n = o_ref.shape[1]
    a = a_ref[...]
    nc = _tile(n, LN_COL_CHUNK)
    for c in range(n // nc):
        sl = slice(c * nc, (c + 1) * nc)
        o_ref[:, sl] = jnp.dot(a, w_ref[:, sl], preferred_element_type=F32)
    _residual_ln(x_ref, g_ref, b_ref, o_ref, ob_ref, alpha)


def _proj_res_ln(a, w, xres, g, b, alpha, tm_pref=128):
    m, k = a.shape
    n = w.shape[1]
    tm = _tile(m, tm_pref)
    row = lambda width: pl.BlockSpec((tm, width), lambda i: (i, 0))
    vec = pl.BlockSpec((1, n), lambda i: (0, 0))
    return pl.pallas_call(
        functools.partial(_proj_res_ln_kernel, alpha=alpha),
        out_shape=(jax.ShapeDtypeStruct((m, n), F32), jax.ShapeDtypeStruct((m, n), BF16)),
        grid=(m // tm,),
        in_specs=[row(k),
                  pl.BlockSpec((k, n), lambda i: (0, 0), pipeline_mode=pl.Buffered(1)),
                  row(n), vec, vec],
        out_specs=(row(n), row(n)),
        compiler_params=_params("arbitrary"),
    )(a, w, xres, g.reshape(1, n), b.reshape(1, n))


FFN_COL_CHUNK = 512


def _ffn_up_kernel(a_ref, wg_ref, wu_ref, o_ref):
    a = a_ref[...]
    tn = o_ref.shape[1]
    nc = _tile(tn, FFN_COL_CHUNK)
    for c in range(tn // nc):
        sl = slice(c * nc, (c + 1) * nc)
        g = jnp.dot(a, wg_ref[:, sl], preferred_element_type=F32)
        u = jnp.dot(a, wu_ref[:, sl], preferred_element_type=F32)
        o_ref[:, sl] = (g * jax.nn.sigmoid(g) * u).astype(o_ref.dtype)


def _ffn_up(a, wg, wu, tm_pref=1024, tn_pref=1024):
    m, k = a.shape
    n = wg.shape[1]
    tm = _tile(m, tm_pref)
    tn = _tile(n, tn_pref)
    return pl.pallas_call(
        _ffn_up_kernel,
        out_shape=jax.ShapeDtypeStruct((m, n), BF16),
        grid=(m // tm, n // tn),
        in_specs=[pl.BlockSpec((tm, k), lambda i, j: (i, 0), pipeline_mode=pl.Buffered(1)),
                  pl.BlockSpec((k, tn), lambda i, j: (0, j)),
                  pl.BlockSpec((k, tn), lambda i, j: (0, j))],
        out_specs=pl.BlockSpec((tm, tn), lambda i, j: (i, j)),
        compiler_params=_params("parallel", "arbitrary"),
    )(a, wg, wu)


def _mla_in_kernel(a_ref, w_ref, qn_ref, kvn_ref, cos_ref, sin_ref, cq_ref, ckv_ref, kr_ref):
    def rms(t, gain):
        r = lax.rsqrt(jnp.mean(t * t, axis=-1, keepdims=True) + RMS_EPS)
        return t * r * gain

    a = a_ref[...]
    kv0 = Q_LORA + KV_LORA
    cq = jnp.dot(a, w_ref[:, :Q_LORA], preferred_element_type=F32)
    cq_ref[...] = rms(cq, qn_ref[...]).astype(BF16)
    ckv = jnp.dot(a, w_ref[:, Q_LORA:kv0], preferred_element_type=F32)
    ckv_ref[...] = rms(ckv, kvn_ref[...]).astype(BF16)
    y = jnp.dot(a, w_ref[:, kv0:], preferred_element_type=F32)
    kr_ref[...] = _rope_half_lanes(y, cos_ref[...], sin_ref[...]).astype(BF16)


def _mla_in(a, w, q_norm, kv_norm, tables, tm_pref=512):
    m, k = a.shape
    n = w.shape[1]
    tm = _tile(m, tm_pref)
    row = lambda width: pl.BlockSpec((tm, width), lambda i: (i, 0))
    return pl.pallas_call(
        _mla_in_kernel,
        out_shape=(jax.ShapeDtypeStruct((m, Q_LORA), BF16),
                   jax.ShapeDtypeStruct((m, KV_LORA), BF16),
                   jax.ShapeDtypeStruct((m, LANES), BF16)),
        grid=(m // tm,),
        in_specs=[row(k),
                  pl.BlockSpec((k, n), lambda i: (0, 0), pipeline_mode=pl.Buffered(1)),
                  pl.BlockSpec((1, Q_LORA), lambda i: (0, 0)),
                  pl.BlockSpec((1, KV_LORA), lambda i: (0, 0)),
                  row(LANES), row(LANES)],
        out_specs=(row(Q_LORA), row(KV_LORA), row(LANES)),
        compiler_params=_params("arbitrary"),
    )(a, w, q_norm.reshape(1, Q_LORA), kv_norm.reshape(1, KV_LORA), *tables)


QK_DIM = QK_NOPE + QK_ROPE
UP_HEADS_PER_DOT = 4


ATTN_BLOCK = 512
ONES_ROWS = 16


def _rope_tables_t(seq, dim):
    half = dim // 2
    inv = ROPE_THETA ** (-jnp.arange(half, dtype=F32) * 2.0 / dim)
    ang = inv[:, None] * jnp.arange(seq).astype(F32)[None, :]
    return jnp.cos(ang), jnp.sin(ang)


def _q_up_t_kernel(a_ref, wn_ref, wr_ref, cos_ref, sin_ref, q_ref, *, heads_per_step, scale):
    a = a_ref[...]
    cos, sin = cos_ref[...], sin_ref[...]
    hpd = UP_HEADS_PER_DOT
    half = QK_ROPE // 2
    tn = (((0,), (1,)), ((), ()))
    for c in range(heads_per_step // hpd):
        nope = lax.dot_general(wn_ref[:, c * hpd * QK_NOPE:(c + 1) * hpd * QK_NOPE], a, tn,
                               preferred_element_type=F32)
        rope = lax.dot_general(wr_ref[:, c * hpd * QK_ROPE:(c + 1) * hpd * QK_ROPE], a, tn,
                               preferred_element_type=F32)
        for h in range(hpd):
            y1 = rope[h * QK_ROPE:h * QK_ROPE + half, :]
            y2 = rope[h * QK_ROPE + half:(h + 1) * QK_ROPE, :]
            q_ref[c * hpd + h, :QK_NOPE, :] = (
                nope[h * QK_NOPE:(h + 1) * QK_NOPE, :] * scale).astype(BF16)
            q_ref[c * hpd + h, QK_NOPE:QK_NOPE + half, :] = (
                (y1 * cos - y2 * sin) * scale).astype(BF16)
            q_ref[c * hpd + h, QK_NOPE + half:, :] = ((y2 * cos + y1 * sin) * scale).astype(BF16)


def _q_up_t(cq, w_uq_g, tables_t, heads, scale, tm_pref=1024, hps_pref=8):
    m, k = cq.shape
    tm = _tile(m, tm_pref)
    hps = _tile(heads, hps_pref)
    assert hps % UP_HEADS_PER_DOT == 0
    half = QK_ROPE // 2
    rope_blk0 = heads * QK_NOPE // (hps * QK_ROPE)
    return pl.pallas_call(
        functools.partial(_q_up_t_kernel, heads_per_step=hps, scale=scale),
        out_shape=jax.ShapeDtypeStruct((heads, QK_DIM, m), BF16),
        grid=(m // tm, heads // hps),
        in_specs=[pl.BlockSpec((tm, k), lambda i, j: (i, 0)),
                  pl.BlockSpec((k, hps * QK_NOPE), lambda i, j: (0, j)),
                  pl.BlockSpec((k, hps * QK_ROPE), lambda i, j: (0, rope_blk0 + j)),
                  pl.BlockSpec((half, tm), lambda i, j: (0, i)),
                  pl.BlockSpec((half, tm), lambda i, j: (0, i))],
        out_specs=pl.BlockSpec((hps, QK_DIM, tm), lambda i, j: (j, 0, i)),
        compiler_params=_params("parallel", "arbitrary"),
    )(cq, w_uq_g, w_uq_g, *tables_t)


def _kv_up_t_kernel(a_ref, wk_ref, wv_ref, kr_ref, k_ref, v_ref, *, heads_per_step, blk):
    a = a_ref[...]
    kr = kr_ref[:, :QK_ROPE]
    hpd = UP_HEADS_PER_DOT
    tn = (((0,), (1,)), ((), ()))
    for c in range(heads_per_step // hpd):
        kn = jnp.dot(a, wk_ref[:, c * hpd * QK_NOPE:(c + 1) * hpd * QK_NOPE],
                     preferred_element_type=F32)
        vt = lax.dot_general(wv_ref[:, c * hpd * V_DIM:(c + 1) * hpd * V_DIM], a, tn,
                             preferred_element_type=F32)
        for h in range(hpd):
            k_ref[c * hpd + h, :, :QK_NOPE] = kn[:, h * QK_NOPE:(h + 1) * QK_NOPE].astype(BF16)
            k_ref[c * hpd + h, :, QK_NOPE:] = kr
            for b in range(a.shape[0] // blk):
                v_ref[c * hpd + h, b] = vt[h * V_DIM:(h + 1) * V_DIM,
                                           b * blk:(b + 1) * blk].astype(BF16)


def _kv_up_t(ckv, w_ukv_g, kr, heads, blk, tm_pref=1024, hps_pref=16):
    m, k = ckv.shape
    tm = _tile(m, tm_pref)
    hps = _tile(heads, hps_pref)
    assert hps % UP_HEADS_PER_DOT == 0 and tm % blk == 0
    v_blk0 = heads * QK_NOPE // (hps * V_DIM)
    return pl.pallas_call(
        functools.partial(_kv_up_t_kernel, heads_per_step=hps, blk=blk),
        out_shape=(jax.ShapeDtypeStruct((heads, m, QK_DIM), BF16),
                   jax.ShapeDtypeStruct((heads, m // blk, V_DIM, blk), BF16)),
        grid=(m // tm, heads // hps),
        in_specs=[pl.BlockSpec((tm, k), lambda i, j: (i, 0)),
                  pl.BlockSpec((k, hps * QK_NOPE), lambda i, j: (0, j)),
                  pl.BlockSpec((k, hps * V_DIM), lambda i, j: (0, v_blk0 + j)),
                  pl.BlockSpec((tm, LANES), lambda i, j: (i, 0))],
        out_specs=(pl.BlockSpec((hps, tm, QK_DIM), lambda i, j: (j, i, 0)),
                   pl.BlockSpec((hps, tm // blk, V_DIM, blk), lambda i, j: (j, i, 0, 0))),
        compiler_params=_params("parallel", "arbitrary"),
    )(ckv, w_ukv_g, w_ukv_g, kr)


def _mla_attn_t_kernel(q_ref, k_ref, v_ref, o_ref, m_ref, acc_ref, s0_ref, s1_ref, *, blk, nsub):
    qi = pl.program_id(1)
    m_ref[...] = jnp.full(m_ref.shape, NEG, F32)
    acc_ref[...] = jnp.zeros(acc_ref.shape, F32)
    ones = jnp.ones((ONES_ROWS, blk), BF16)
    slots = (s0_ref, s1_ref)

    def scores(sub, j, slot):
        start = pl.multiple_of(j * blk, blk)
        kk = k_ref[0, pl.ds(start, blk), :]
        qt = q_ref[0, :, sub * blk:(sub + 1) * blk]
        slot[sub] = jnp.dot(kk, qt, preferred_element_type=F32)

    def softmax_pv(sub, j, slot, masked):
        vt = jnp.concatenate([v_ref[0, j], ones], axis=0)
        s = slot[sub]
        if masked:
            key = lax.broadcasted_iota(jnp.int32, s.shape, 0)
            qry = lax.broadcasted_iota(jnp.int32, s.shape, 1)
            s = jnp.where(key <= qry, s, NEG)
        m_prev = m_ref[sub]
        m_new = jnp.maximum(m_prev, jnp.max(s, axis=0, keepdims=True))
        corr = jnp.exp(m_prev - m_new)
        p = jnp.exp(s - m_new).astype(BF16)
        acc_ref[sub] = corr * acc_ref[sub] + jnp.dot(vt, p, preferred_element_type=F32)
        m_ref[sub] = m_new

    for sub in range(nsub):
        scores(sub, 0, slots[0])

    def full_blocks(t):
        for c in range(nsub):
            j = t * nsub + c
            for sub in range(nsub):
                scores(sub, j + 1, slots[(c + 1) % 2])
                softmax_pv(sub, j, slots[c % 2], masked=False)

    def body(t, carry):
        full_blocks(t)
        return carry

    lax.fori_loop(0, qi, body, 0)
    base = qi * nsub
    for c in range(nsub):
        for sub in range(c, nsub):
            if sub > c:
                scores(sub, base + c + 1, slots[(c + 1) % 2])
            softmax_pv(sub, base + c, slots[c % 2], masked=(sub == c))
    for sub in range(nsub):
        acc = acc_ref[sub]
        o_t = acc[:V_DIM, :] / acc[V_DIM:V_DIM + 1, :]
        o_ref[sub * blk:(sub + 1) * blk, :] = o_t.T.astype(o_ref.dtype)


def _mla_attn_t(q_t, k, v_t, nsub_pref=4):
    heads, s, _ = k.shape
    blk = v_t.shape[3]
    nsub = _tile(s // blk, nsub_pref)
    assert nsub % 2 == 0, "score slots alternate with key-block parity"
    tq = blk * nsub
    return pl.pallas_call(
        functools.partial(_mla_attn_t_kernel, blk=blk, nsub=nsub),
        out_shape=jax.ShapeDtypeStruct((s, heads * V_DIM), BF16),
        grid=(heads, s // tq),
        in_specs=[pl.BlockSpec((1, QK_DIM, tq), lambda h, i: (h, 0, i)),
                  pl.BlockSpec((1, s, QK_DIM), lambda h, i: (h, 0, 0)),
                  pl.BlockSpec((1, s // blk, V_DIM, blk), lambda h, i: (h, 0, 0, 0))],
        out_specs=pl.BlockSpec((tq, V_DIM), lambda h, i: (i, h)),
        scratch_shapes=[pltpu.VMEM((nsub, 1, blk), F32),
                        pltpu.VMEM((nsub, V_DIM + ONES_ROWS, blk), F32),
                        pltpu.VMEM((nsub, blk, blk), F32),
                        pltpu.VMEM((nsub, blk, blk), F32)],
        compiler_params=_params("parallel", "arbitrary"),
    )(q_t, k, v_t)


def _ffn_block(x, xb, w_gate, w_up, w_down, l, g, b, alpha, want_bf16):
    hidden = w_gate.shape[2]
    hp = _round_up(hidden, 512)
    wg = _cast_pad(w_gate, l, cols_p=hp)
    wu = _cast_pad(w_up, l, cols_p=hp)
    wd = _cast_pad(w_down, l, rows_p=hp)
    hmid = _ffn_up(xb, wg, wu)
    return _mm_res_ln(hmid, wd, x, g, b, alpha, want_bf16=want_bf16)


def _attn_conv_layer(x, xb, w_in, conv_w, w_out, j, g, b, alpha):
    s, d = x.shape
    a_width = d // 2
    hproj = _in_proj(xb, _cast_pad(w_in, j), _rope_tables(s, HEAD_DIM), rope_cols=2 * a_width,
                     scale=HEAD_DIM ** -0.5, scaled_cols=a_width)
    outs, lses = [], []
    for window, dilation in DILATED_PATTERNS:
        assert window // dilation == DIL_BLOCK and s % window == 0
        o, lse = _dilated_branch(hproj, a_width, dilation)
        outs.append(o)
        lses.append(lse)
    ab = _mix(outs, lses, hproj, conv_w[j], a_width)
    return _proj_res_ln(ab, _cast_pad(w_out, j), x, g, b, alpha)


def _mla_layer(x, xb, w_in, q_norm, kv_norm, w_uq, w_ukv, w_out, j, g, b, alpha):
    s, d = x.shape
    heads = d // 128
    w_in_p = _cast_pad(w_in, j, cols_p=w_in.shape[2] + LANES - QK_ROPE)
    cq, ckv, kr = _mla_in(xb, w_in_p, q_norm[j], kv_norm[j], _rope_tables(s, QK_ROPE))
    w_uq_h = w_uq[j].astype(BF16).reshape(Q_LORA, heads, QK_DIM)
    w_uq_g = jnp.concatenate([w_uq_h[:, :, :QK_NOPE].reshape(Q_LORA, heads * QK_NOPE),
                              w_uq_h[:, :, QK_NOPE:].reshape(Q_LORA, heads * QK_ROPE)], axis=1)
    q_t = _q_up_t(cq, w_uq_g, _rope_tables_t(s, QK_ROPE), heads, QK_DIM ** -0.5)
    w_ukv_h = w_ukv[j].astype(BF16).reshape(KV_LORA, heads, QK_NOPE + V_DIM)
    w_ukv_g = jnp.concatenate([w_ukv_h[:, :, :QK_NOPE].reshape(KV_LORA, heads * QK_NOPE),
                               w_ukv_h[:, :, QK_NOPE:].reshape(KV_LORA, heads * V_DIM)], axis=1)
    k, v_t = _kv_up_t(ckv, w_ukv_g, kr, heads, _tile(s, ATTN_BLOCK))
    o = _mla_attn_t(q_t, k, v_t)
    return _proj_res_ln(o, _cast_pad(w_out, j), x, g, b, alpha)


def kernel(x, w_in_a, conv_w, w_out_a, w_in_c, q_norm, kv_norm, w_uq, w_ukv, w_out_c,
           ln1_g, ln1_b, w_gate, w_up, w_down, ln2_g, ln2_b):
    batch, s, d = x.shape
    depth = ln1_g.shape[0]
    alpha = (2.0 * depth) ** 0.25
    outs = []
    for bi in range(batch):
        xf = x[bi]
        xb = xf.astype(BF16)
        for l in range(depth):
            j = l // 2
            if l % 2 == 0:
                xf, xb = _attn_conv_layer(xf, xb, w_in_a, conv_w, w_out_a, j,
                                          ln1_g[l], ln1_b[l], alpha)
            else:
                xf, xb = _mla_layer(xf, xb, w_in_c, q_norm, kv_norm, w_uq, w_ukv, w_out_c, j,
                                    ln1_g[l], ln1_b[l], alpha)
            xf, xb = _ffn_block(xf, xb, w_gate, w_up, w_down, l, ln2_g[l], ln2_b[l],
                                alpha, want_bf16=l + 1 < depth)
        outs.append(xf)
    return jnp.stack(outs)
```

```python
import functools
import math

import jax
import jax.numpy as jnp
from jax import lax
from jax.experimental import pallas as pl
from jax.experimental.pallas import tpu as pltpu

F32 = jnp.float32
BF16 = jnp.bfloat16

HEAD_DIM = 128
CONV_WIDTH = 3
DILATED_PATTERNS = ((128, 1), (512, 4), (2048, 16))
DIL_BLOCK = 128
Q_LORA = 1536
KV_LORA = 512
QK_NOPE = 128
QK_ROPE = 64
V_DIM = 128
ROPE_THETA = 10000.0
LN_EPS = 1e-5
RMS_EPS = 1e-6
NEG = -1e30

LANES = 128
SUBLANES = 8
VMEM_LIMIT_BYTES = 60 * 1024 * 1024


def _round_up(n, m):
    return -(-n // m) * m


def _tile(n, pref):
    if n <= pref:
        return n
    t = pref
    while n % t:
        t //= 2
    return t


def _params(*sem):
    return pltpu.CompilerParams(dimension_semantics=sem, vmem_limit_bytes=VMEM_LIMIT_BYTES)


def _cast_pad_kernel(w_ref, o_ref, *, rows, cols, rows_p):
    tr, cols_p = o_ref.shape
    val = w_ref[...].astype(BF16)
    if rows_p > rows:
        row = pl.program_id(0) * tr + lax.broadcasted_iota(jnp.int32, val.shape, 0)
        val = jnp.where(row < rows, val, jnp.zeros_like(val))
    o_ref[:, :cols] = val
    if cols_p > cols:
        o_ref[:, cols:] = jnp.zeros((tr, cols_p - cols), BF16)


def _cast_pad(w_stack, layer, rows_p=None, cols_p=None, tr_pref=256):
    _, rows, cols = w_stack.shape
    rows_p = rows_p or rows
    cols_p = cols_p or cols
    tr = _tile(rows_p, tr_pref)
    last = (rows - 1) // tr
    return pl.pallas_call(
        functools.partial(_cast_pad_kernel, rows=rows, cols=cols, rows_p=rows_p),
        out_shape=jax.ShapeDtypeStruct((rows_p, cols_p), BF16),
        grid=(rows_p // tr,),
        in_specs=[pl.BlockSpec((None, tr, cols), lambda i: (layer, jnp.minimum(i, last), 0))],
        out_specs=pl.BlockSpec((tr, cols_p), lambda i: (i, 0)),
        compiler_params=_params("parallel"),
    )(w_stack)


def _rope_tables(seq, dim):
    half = dim // 2
    inv = ROPE_THETA ** (-jnp.arange(half, dtype=F32) * 2.0 / dim)
    ang = jnp.arange(seq).astype(F32)[:, None] * inv[None, :]
    cos, sin = jnp.cos(ang), jnp.sin(ang)
    reps = LANES // dim
    return (jnp.concatenate([cos, cos] * reps, axis=1),
            jnp.concatenate([-sin, sin] * reps, axis=1))


def _rope_full_lanes(t, cos, sin):
    return t * cos + pltpu.roll(t, HEAD_DIM // 2, axis=1) * sin


def _rope_half_lanes(y, cos, sin):
    half = QK_ROPE // 2
    lane = lax.broadcasted_iota(jnp.int32, y.shape, 1)
    partner = jnp.where(lane % QK_ROPE < half, pltpu.roll(y, LANES - half, axis=1),
                        pltpu.roll(y, half, axis=1))
    return y * cos + partner * sin


def _in_proj_kernel(a_ref, w_ref, cos_ref, sin_ref, o_ref, *, scale, n_scaled_tiles,
                    n_rope_tiles):
    j = pl.program_id(1)
    acc = jnp.dot(a_ref[...], w_ref[...], preferred_element_type=F32)

    @pl.when(j >= n_rope_tiles)
    def _():
        o_ref[...] = acc

    @pl.when(j < n_rope_tiles)
    def _():
        cos = cos_ref[...]
        sin = sin_ref[...]
        mult = jnp.where(j < n_scaled_tiles, scale, 1.0).astype(F32)
        for c in range(acc.shape[1] // HEAD_DIM):
            sl = slice(c * HEAD_DIM, (c + 1) * HEAD_DIM)
            o_ref[:, sl] = _rope_full_lanes(acc[:, sl], cos, sin) * mult


def _in_proj(a, w, rope_tables, *, rope_cols, scale, scaled_cols, tm_pref=1024, tn_pref=1024):
    m, k = a.shape
    n = w.shape[1]
    tm = _tile(m, tm_pref)
    tn = _tile(math.gcd(n, rope_cols, scaled_cols), tn_pref)
    return pl.pallas_call(
        functools.partial(_in_proj_kernel, scale=scale, n_scaled_tiles=scaled_cols // tn,
                          n_rope_tiles=rope_cols // tn),
        out_shape=jax.ShapeDtypeStruct((m, n), F32),
        grid=(m // tm, n // tn),
        in_specs=[pl.BlockSpec((tm, k), lambda i, j: (i, 0)),
                  pl.BlockSpec((k, tn), lambda i, j: (0, j)),
                  pl.BlockSpec((tm, LANES), lambda i, j: (i, 0)),
                  pl.BlockSpec((tm, LANES), lambda i, j: (i, 0))],
        out_specs=pl.BlockSpec((tm, tn), lambda i, j: (i, j)),
        compiler_params=_params("parallel", "arbitrary"),
    )(a, w, *rope_tables)


DIL_ROWS = 2048


DIL_STAGE_STRIDE = 4


def _dilated_kernel(q_ref, kp_ref, kc_ref, vp_ref, vc_ref, o_ref, lse_ref, *stage, d, groups):
    n = pl.program_id(0)
    h = pl.program_id(1)
    blk = DIL_BLOCK
    span = blk * d
    qi = lax.broadcasted_iota(jnp.int32, (blk, 2 * blk), 0)
    ki = lax.broadcasted_iota(jnp.int32, (blk, 2 * blk), 1)
    dist = qi + blk - ki
    band = (dist >= 0) & (dist <= blk)
    band_first = band & ((ki >= blk) | (n > 0))
    lane = lax.broadcasted_iota(jnp.int32, (blk, LANES), 1)

    @pl.when(h == 0)
    def _():
        lse_ref[...] = jnp.zeros(lse_ref.shape, F32)

    inner = DIL_STAGE_STRIDE if stage else 1
    if stage:
        sq, skp, skc, svp, svc, so, sl = stage
        for src, dst in ((q_ref, sq), (kp_ref, skp), (kc_ref, skc), (vp_ref, svp),
                         (vc_ref, svc), (lse_ref, sl)):
            for rr in range(inner):
                dst[rr] = src[pl.ds(rr, src.shape[0] // inner, stride=inner), :]
    else:
        sq, skp, skc, svp, svc, so, sl = q_ref, kp_ref, kc_ref, vp_ref, vc_ref, o_ref, lse_ref

    def rows_of(ref, start, r):
        if not stage:
            return ref, (pl.ds(start + r, blk, stride=d) if d > 1 else pl.ds(start + r, blk))
        return ref.at[r % inner], pl.ds(start // inner + r // inner, blk, stride=d // inner)

    def read(ref, start, r):
        view, idx = rows_of(ref, start, r)
        return view[idx, :]

    for g in range(groups):
        for r in range(d):
            q = read(sq, g * span, r).astype(BF16)
            if g == 0:
                k_prev, v_prev = read(skp, 0, r), read(svp, 0, r)
            else:
                k_prev, v_prev = read(skc, (g - 1) * span, r), read(svc, (g - 1) * span, r)
            kk = jnp.concatenate([k_prev, read(skc, g * span, r)], axis=0).astype(BF16)
            vv = jnp.concatenate([v_prev, read(svc, g * span, r)], axis=0).astype(BF16)
            s = lax.dot_general(q, kk, (((1,), (1,)), ((), ())), preferred_element_type=F32)
            s = jnp.where(band_first if g == 0 else band, s, NEG)
            m = jnp.max(s, axis=-1, keepdims=True)
            p = jnp.exp(s - m)
            den = jnp.sum(p, axis=-1, keepdims=True)
            o = jnp.dot(p.astype(BF16), vv, preferred_element_type=F32)
            o_view, idx = rows_of(so, g * span, r)
            o_view[idx, :] = o / den
            l_view, idx = rows_of(sl, g * span, r)
            l_view[idx, :] = jnp.where(lane == h, m + jnp.log(den), l_view[idx, :])

    if stage:
        for rr in range(inner):
            dst_rows = pl.ds(rr, o_ref.shape[0] // inner, stride=inner)
            o_ref[dst_rows, :] = so[rr]
            lse_ref[dst_rows, :] = sl[rr]


def _dilated_branch(hproj, a_width, dilation):
    s = hproj.shape[0]
    heads = a_width // HEAD_DIM
    d = dilation
    span = DIL_BLOCK * d
    rows = _tile(s, DIL_ROWS)
    groups = rows // span
    assert groups >= 1 and rows % span == 0
    cur = lambda col0: pl.BlockSpec((rows, HEAD_DIM), lambda n, h: (n, col0 + h))
    prev = lambda col0: pl.BlockSpec(
        (span, HEAD_DIM), lambda n, h: (jnp.maximum(n * groups - 1, 0), col0 + h))
    stage = []
    if d % (4 * DIL_STAGE_STRIDE) == 0:
        staged = lambda nrows: pltpu.VMEM((DIL_STAGE_STRIDE, nrows // DIL_STAGE_STRIDE, HEAD_DIM), F32)
        stage = [staged(rows), staged(span), staged(rows), staged(span), staged(rows),
                 staged(rows), staged(rows)]
    return pl.pallas_call(
        functools.partial(_dilated_kernel, d=d, groups=groups),
        out_shape=(jax.ShapeDtypeStruct((s, a_width), F32),
                   jax.ShapeDtypeStruct((s, LANES), F32)),
        grid=(s // rows, heads),
        in_specs=[cur(0), prev(heads), cur(heads), prev(2 * heads), cur(2 * heads)],
        out_specs=(pl.BlockSpec((rows, HEAD_DIM), lambda n, h: (n, h)),
                   pl.BlockSpec((rows, LANES), lambda n, h: (n, 0))),
        scratch_shapes=stage,
        compiler_params=_params("parallel", "arbitrary"),
    )(hproj, hproj, hproj, hproj, hproj)


def _mix_kernel(o1_ref, o2_ref, o3_ref, l1_ref, l2_ref, l3_ref,
                gb_ref, gc_ref, hin_ref, gch_ref, hinh_ref, cw_ref, out_ref, *, heads, a_width):
    i = pl.program_id(0)
    l1, l2, l3 = l1_ref[...], l2_ref[...], l3_ref[...]
    m = jnp.maximum(jnp.maximum(l1, l2), l3)
    e1, e2, e3 = jnp.exp(l1 - m), jnp.exp(l2 - m), jnp.exp(l3 - m)
    inv = 1.0 / (e1 + e2 + e3)
    w1, w2, w3 = e1 * inv, e2 * inv, e3 * inv
    for h in range(heads):
        sl = slice(h * HEAD_DIM, (h + 1) * HEAD_DIM)
        hs = slice(h, h + 1)
        mixed = (w1[:, hs] * o1_ref[:, sl] + w2[:, hs] * o2_ref[:, sl]
                 + w3[:, hs] * o3_ref[:, sl])
        out_ref[:, sl] = mixed.astype(out_ref.dtype)

    u = gc_ref[...] * hin_ref[...]
    halo = gch_ref[...] * hinh_ref[...]
    halo = halo * jnp.where(i > 0, 1.0, 0.0).astype(F32)
    row = lax.broadcasted_iota(jnp.int32, u.shape, 0)
    hm1 = halo[SUBLANES - 1:SUBLANES, :]
    hm2 = halo[SUBLANES - 2:SUBLANES - 1, :]
    u1 = jnp.where(row == 0, hm1, pltpu.roll(u, 1, axis=0))
    u2 = jnp.where(row == 0, hm2, jnp.where(row == 1, hm1, pltpu.roll(u, 2, axis=0)))
    cw = cw_ref[...]
    y = cw[0:1, :] * u2 + cw[1:2, :] * u1 + cw[2:3, :] * u
    out_ref[:, a_width:] = (gb_ref[...] * y).astype(out_ref.dtype)


def _mix(o_list, lse_list, hproj, conv_w, a_width, tm_pref=256):
    s = hproj.shape[0]
    b_width = (hproj.shape[1] - 3 * a_width) // 3
    assert b_width == a_width, "gate blocks are addressed in units of the attention width"
    gb_blk, gc_blk, hin_blk = 3, 4, 5
    heads = a_width // HEAD_DIM
    tm = _tile(s, tm_pref)
    hb = tm // SUBLANES
    row_blk = lambda c: pl.BlockSpec((tm, a_width), lambda i: (i, c))
    halo_blk = lambda c: pl.BlockSpec((SUBLANES, b_width),
                                      lambda i: (jnp.maximum(i * hb - 1, 0), c))
    lse_blk = pl.BlockSpec((tm, LANES), lambda i: (i, 0))
    return pl.pallas_call(
        functools.partial(_mix_kernel, heads=heads, a_width=a_width),
        out_shape=jax.ShapeDtypeStruct((s, a_width + b_width), BF16),
        grid=(s // tm,),
        in_specs=[row_blk(0)] * 3 + [lse_blk] * 3
        + [row_blk(gb_blk), row_blk(gc_blk), row_blk(hin_blk), halo_blk(gc_blk),
           halo_blk(hin_blk), pl.BlockSpec((CONV_WIDTH, b_width), lambda i: (0, 0))],
        out_specs=pl.BlockSpec((tm, a_width + b_width), lambda i: (i, 0)),
        compiler_params=_params("parallel"),
    )(*o_list, *lse_list, hproj, hproj, hproj, hproj, hproj, conv_w)


LN_COL_CHUNK = 1024
LN_ROW_CHUNK = 8


def _residual_ln(x_ref, g_ref, b_ref, o_ref, ob_ref, alpha):
    tm = o_ref.shape[0]
    rc = _tile(tm, LN_ROW_CHUNK)
    for r in range(tm // rc):
        rows = slice(r * rc, (r + 1) * rc)
        z = alpha * x_ref[rows, :] + o_ref[rows, :]
        mu = jnp.mean(z, axis=-1, keepdims=True)
        zc = z - mu
        var = jnp.mean(zc * zc, axis=-1, keepdims=True)
        y = zc * lax.rsqrt(var + LN_EPS) * g_ref[...] + b_ref[...]
        o_ref[rows, :] = y
        if ob_ref is not None:
            ob_ref[rows, :] = y.astype(BF16)


def _mm_res_ln_kernel(a_ref, w_ref, x_ref, g_ref, b_ref, o_ref, *maybe_ob, alpha):
    k = pl.program_id(1)
    n = o_ref.shape[1]

    @pl.when(k == 0)
    def _():
        o_ref[...] = jnp.zeros(o_ref.shape, F32)

    a = a_ref[...]
    nc = _tile(n, LN_COL_CHUNK)
    for c in range(n // nc):
        sl = slice(c * nc, (c + 1) * nc)
        o_ref[:, sl] += jnp.dot(a, w_ref[:, sl], preferred_element_type=F32)

    @pl.when(k == pl.num_programs(1) - 1)
    def _():
        _residual_ln(x_ref, g_ref, b_ref, o_ref, maybe_ob[0] if maybe_ob else None, alpha)


def _mm_res_ln(a, w, xres, g, b, alpha, *, want_bf16, tm_pref=512, tk_pref=512):
    m, k = a.shape
    n = w.shape[1]
    tm = _tile(m, tm_pref)
    tk = _tile(k, tk_pref)
    out_shape = [jax.ShapeDtypeStruct((m, n), F32)]
    out_specs = [pl.BlockSpec((tm, n), lambda i, kk: (i, 0))]
    if want_bf16:
        out_shape.append(jax.ShapeDtypeStruct((m, n), BF16))
        out_specs.append(pl.BlockSpec((tm, n), lambda i, kk: (i, 0)))
    res = pl.pallas_call(
        functools.partial(_mm_res_ln_kernel, alpha=alpha),
        out_shape=tuple(out_shape),
        grid=(m // tm, k // tk),
        in_specs=[
            pl.BlockSpec((tm, tk), lambda i, kk: (i, kk)),
            pl.BlockSpec((tk, n), lambda i, kk: (kk, 0)),
            pl.BlockSpec((tm, n), lambda i, kk: (i, 0)),
            pl.BlockSpec((1, n), lambda i, kk: (0, 0)),
            pl.BlockSpec((1, n), lambda i, kk: (0, 0)),
        ],
        out_specs=tuple(out_specs),
        compiler_params=_params("parallel", "arbitrary"),
    )(a, w, xres, g.reshape(1, n), b.reshape(1, n))
    return res if want_bf16 else (res[0], None)


def _proj_res_ln_kernel(a_ref, w_ref, x_ref, g_ref, b_ref, o_ref, ob_ref, *, alpha):
    n = o_ref.shape[1]
    a = a_ref[...]
    nc = _tile(n, LN_COL_CHUNK)
    for c in range(n // nc):
        sl = slice(c * nc, (c + 1) * nc)
        o_ref[:, sl] = jnp.dot(a, w_ref[:, sl], preferred_element_type=F32)
    _residual_ln(x_ref, g_ref, b_ref, o_ref, ob_ref, alpha)


def _proj_res_ln(a, w, xres, g, b, alpha, tm_pref=128):
    m, k = a.shape
    n = w.shape[1]
    tm = _tile(m, tm_pref)
    row = lambda width: pl.BlockSpec((tm, width), lambda i: (i, 0))
    vec = pl.BlockSpec((1, n), lambda i: (0, 0))
    return pl.pallas_call(
        functools.partial(_proj_res_ln_kernel, alpha=alpha),
        out_shape=(jax.ShapeDtypeStruct((m, n), F32), jax.ShapeDtypeStruct((m, n), BF16)),
        grid=(m // tm,),
        in_specs=[row(k),
                  pl.BlockSpec((k, n), lambda i: (0, 0), pipeline_mode=pl.Buffered(1)),
                  row(n), vec, vec],
        out_specs=(row(n), row(n)),
        compiler_params=_params("arbitrary"),
    )(a, w, xres, g.reshape(1, n), b.reshape(1, n))


def _ffn_up_kernel(a_ref, wg_ref, wu_ref, o_ref):
    a = a_ref[...]
    g = jnp.dot(a, wg_ref[...], preferred_element_type=F32)
    u = jnp.dot(a, wu_ref[...], preferred_element_type=F32)
    o_ref[...] = (g * jax.nn.sigmoid(g) * u).astype(o_ref.dtype)


def _ffn_up(a, wg, wu, tm_pref=1024, tn_pref=512):
    m, k = a.shape
    n = wg.shape[1]
    tm = _tile(m, tm_pref)
    tn = _tile(n, tn_pref)
    return pl.pallas_call(
        _ffn_up_kernel,
        out_shape=jax.ShapeDtypeStruct((m, n), BF16),
        grid=(m // tm, n // tn),
        in_specs=[pl.BlockSpec((tm, k), lambda i, j: (i, 0)),
                  pl.BlockSpec((k, tn), lambda i, j: (0, j)),
                  pl.BlockSpec((k, tn), lambda i, j: (0, j))],
        out_specs=pl.BlockSpec((tm, tn), lambda i, j: (i, j)),
        compiler_params=_params("parallel", "arbitrary"),
    )(a, wg, wu)


def _mla_in_kernel(a_ref, w_ref, qn_ref, kvn_ref, cos_ref, sin_ref, cq_ref, ckv_ref, kr_ref):
    def rms(t, gain):
        r = lax.rsqrt(jnp.mean(t * t, axis=-1, keepdims=True) + RMS_EPS)
        return t * r * gain

    a = a_ref[...]
    kv0 = Q_LORA + KV_LORA
    cq = jnp.dot(a, w_ref[:, :Q_LORA], preferred_element_type=F32)
    cq_ref[...] = rms(cq, qn_ref[...]).astype(BF16)
    ckv = jnp.dot(a, w_ref[:, Q_LORA:kv0], preferred_element_type=F32)
    ckv_ref[...] = rms(ckv, kvn_ref[...]).astype(BF16)
    y = jnp.dot(a, w_ref[:, kv0:], preferred_element_type=F32)
    kr_ref[...] = _rope_half_lanes(y, cos_ref[...], sin_ref[...]).astype(BF16)


def _mla_in(a, w, q_norm, kv_norm, tables, tm_pref=512):
    m, k = a.shape
    n = w.shape[1]
    tm = _tile(m, tm_pref)
    row = lambda width: pl.BlockSpec((tm, width), lambda i: (i, 0))
    return pl.pallas_call(
        _mla_in_kernel,
        out_shape=(jax.ShapeDtypeStruct((m, Q_LORA), BF16),
                   jax.ShapeDtypeStruct((m, KV_LORA), BF16),
                   jax.ShapeDtypeStruct((m, LANES), BF16)),
        grid=(m // tm,),
        in_specs=[row(k),
                  pl.BlockSpec((k, n), lambda i: (0, 0), pipeline_mode=pl.Buffered(1)),
                  pl.BlockSpec((1, Q_LORA), lambda i: (0, 0)),
                  pl.BlockSpec((1, KV_LORA), lambda i: (0, 0)),
                  row(LANES), row(LANES)],
        out_specs=(row(Q_LORA), row(KV_LORA), row(LANES)),
        compiler_params=_params("arbitrary"),
    )(a, w, q_norm.reshape(1, Q_LORA), kv_norm.reshape(1, KV_LORA), *tables)


QK_DIM = QK_NOPE + QK_ROPE
UP_HEADS_PER_DOT = 4


ATTN_BLOCK = 512
ONES_ROWS = 16


def _rope_tables_t(seq, dim):
    half = dim // 2
    inv = ROPE_THETA ** (-jnp.arange(half, dtype=F32) * 2.0 / dim)
    ang = inv[:, None] * jnp.arange(seq).astype(F32)[None, :]
    return jnp.cos(ang), jnp.sin(ang)


def _q_up_t_kernel(a_ref, wn_ref, wr_ref, cos_ref, sin_ref, q_ref, *, heads_per_step, scale):
    a = a_ref[...]
    cos, sin = cos_ref[...], sin_ref[...]
    hpd = UP_HEADS_PER_DOT
    half = QK_ROPE // 2
    tn = (((0,), (1,)), ((), ()))
    for c in range(heads_per_step // hpd):
        nope = lax.dot_general(wn_ref[:, c * hpd * QK_NOPE:(c + 1) * hpd * QK_NOPE], a, tn,
                               preferred_element_type=F32)
        rope = lax.dot_general(wr_ref[:, c * hpd * QK_ROPE:(c + 1) * hpd * QK_ROPE], a, tn,
                               preferred_element_type=F32)
        for h in range(hpd):
            y1 = rope[h * QK_ROPE:h * QK_ROPE + half, :]
            y2 = rope[h * QK_ROPE + half:(h + 1) * QK_ROPE, :]
            q_ref[c * hpd + h, :QK_NOPE, :] = (
                nope[h * QK_NOPE:(h + 1) * QK_NOPE, :] * scale).astype(BF16)
            q_ref[c * hpd + h, QK_NOPE:QK_NOPE + half, :] = (
                (y1 * cos - y2 * sin) * scale).astype(BF16)
            q_ref[c * hpd + h, QK_NOPE + half:, :] = ((y2 * cos + y1 * sin) * scale).astype(BF16)


def _q_up_t(cq, w_uq_g, tables_t, heads, scale, tm_pref=1024, hps_pref=8):
    m, k = cq.shape
    tm = _tile(m, tm_pref)
    hps = _tile(heads, hps_pref)
    assert hps % UP_HEADS_PER_DOT == 0
    half = QK_ROPE // 2
    rope_blk0 = heads * QK_NOPE // (hps * QK_ROPE)
    return pl.pallas_call(
        functools.partial(_q_up_t_kernel, heads_per_step=hps, scale=scale),
        out_shape=jax.ShapeDtypeStruct((heads, QK_DIM, m), BF16),
        grid=(m // tm, heads // hps),
        in_specs=[pl.BlockSpec((tm, k), lambda i, j: (i, 0)),
                  pl.BlockSpec((k, hps * QK_NOPE), lambda i, j: (0, j)),
                  pl.BlockSpec((k, hps * QK_ROPE), lambda i, j: (0, rope_blk0 + j)),
                  pl.BlockSpec((half, tm), lambda i, j: (0, i)),
                  pl.BlockSpec((half, tm), lambda i, j: (0, i))],
        out_specs=pl.BlockSpec((hps, QK_DIM, tm), lambda i, j: (j, 0, i)),
        compiler_params=_params("parallel", "arbitrary"),
    )(cq, w_uq_g, w_uq_g, *tables_t)


def _kv_up_t_kernel(a_ref, wk_ref, wv_ref, kr_ref, k_ref, v_ref, *, heads_per_step, blk):
    a = a_ref[...]
    kr = kr_ref[:, :QK_ROPE]
    hpd = UP_HEADS_PER_DOT
    tn = (((0,), (1,)), ((), ()))
    for c in range(heads_per_step // hpd):
        kn = jnp.dot(a, wk_ref[:, c * hpd * QK_NOPE:(c + 1) * hpd * QK_NOPE],
                     preferred_element_type=F32)
        vt = lax.dot_general(wv_ref[:, c * hpd * V_DIM:(c + 1) * hpd * V_DIM], a, tn,
                             preferred_element_type=F32)
        for h in range(hpd):
            k_ref[c * hpd + h, :, :QK_NOPE] = kn[:, h * QK_NOPE:(h + 1) * QK_NOPE].astype(BF16)
            k_ref[c * hpd + h, :, QK_NOPE:] = kr
            for b in range(a.shape[0] // blk):
                v_ref[c * hpd + h, b] = vt[h * V_DIM:(h + 1) * V_DIM,
                                           b * blk:(b + 1) * blk].astype(BF16)


def _kv_up_t(ckv, w_ukv_g, kr, heads, blk, tm_pref=1024, hps_pref=16):
    m, k = ckv.shape
    tm = _tile(m, tm_pref)
    hps = _tile(heads, hps_pref)
    assert hps % UP_HEADS_PER_DOT == 0 and tm % blk == 0
    v_blk0 = heads * QK_NOPE // (hps * V_DIM)
    return pl.pallas_call(
        functools.partial(_kv_up_t_kernel, heads_per_step=hps, blk=blk),
        out_shape=(jax.ShapeDtypeStruct((heads, m, QK_DIM), BF16),
                   jax.ShapeDtypeStruct((heads, m // blk, V_DIM, blk), BF16)),
        grid=(m // tm, heads // hps),
        in_specs=[pl.BlockSpec((tm, k), lambda i, j: (i, 0)),
                  pl.BlockSpec((k, hps * QK_NOPE), lambda i, j: (0, j)),
                  pl.BlockSpec((k, hps * V_DIM), lambda i, j: (0, v_blk0 + j)),
                  pl.BlockSpec((tm, LANES), lambda i, j: (i, 0))],
        out_specs=(pl.BlockSpec((hps, tm, QK_DIM), lambda i, j: (j, i, 0)),
                   pl.BlockSpec((hps, tm // blk, V_DIM, blk), lambda i, j: (j, i, 0, 0))),
        compiler_params=_params("parallel", "arbitrary"),
    )(ckv, w_ukv_g, w_ukv_g, kr)


def _mla_attn_t_kernel(q_ref, k_ref, v_ref, o_ref, m_ref, acc_ref, s0_ref, s1_ref, *, blk, nsub):
    qi = pl.program_id(1)
    m_ref[...] = jnp.full(m_ref.shape, NEG, F32)
    acc_ref[...] = jnp.zeros(acc_ref.shape, F32)
    ones = jnp.ones((ONES_ROWS, blk), BF16)
    slots = (s0_ref, s1_ref)

    def scores(sub, j, slot):
        start = pl.multiple_of(j * blk, blk)
        kk = k_ref[0, pl.ds(start, blk), :]
        qt = q_ref[0, :, sub * blk:(sub + 1) * blk]
        slot[sub] = jnp.dot(kk, qt, preferred_element_type=F32)

    def softmax_pv(sub, j, slot, masked):
        vt = jnp.concatenate([v_ref[0, j], ones], axis=0)
        s = slot[sub]
        if masked:
            key = lax.broadcasted_iota(jnp.int32, s.shape, 0)
            qry = lax.broadcasted_iota(jnp.int32, s.shape, 1)
            s = jnp.where(key <= qry, s, NEG)
        m_prev = m_ref[sub]
        m_new = jnp.maximum(m_prev, jnp.max(s, axis=0, keepdims=True))
        corr = jnp.exp(m_prev - m_new)
        p = jnp.exp(s - m_new).astype(BF16)
        acc_ref[sub] = corr * acc_ref[sub] + jnp.dot(vt, p, preferred_element_type=F32)
        m_ref[sub] = m_new

    for sub in range(nsub):
        scores(sub, 0, slots[0])

    def full_blocks(t):
        for c in range(nsub):
            j = t * nsub + c
            for sub in range(nsub):
                scores(sub, j + 1, slots[(c + 1) % 2])
                softmax_pv(sub, j, slots[c % 2], masked=False)

    def body(t, carry):
        full_blocks(t)
        return carry

    lax.fori_loop(0, qi, body, 0)
    base = qi * nsub
    for c in range(nsub):
        for sub in range(c, nsub):
            if sub > c:
                scores(sub, base + c + 1, slots[(c + 1) % 2])
            softmax_pv(sub, base + c, slots[c % 2], masked=(sub == c))
    for sub in range(nsub):
        acc = acc_ref[sub]
        o_t = acc[:V_DIM, :] / acc[V_DIM:V_DIM + 1, :]
        o_ref[sub * blk:(sub + 1) * blk, :] = o_t.T.astype(o_ref.dtype)


def _mla_attn_t(q_t, k, v_t, nsub_pref=4):
    heads, s, _ = k.shape
    blk = v_t.shape[3]
    nsub = _tile(s // blk, nsub_pref)
    assert nsub % 2 == 0, "score slots alternate with key-block parity"
    tq = blk * nsub
    return pl.pallas_call(
        functools.partial(_mla_attn_t_kernel, blk=blk, nsub=nsub),
        out_shape=jax.ShapeDtypeStruct((s, heads * V_DIM), BF16),
        grid=(heads, s // tq),
        in_specs=[pl.BlockSpec((1, QK_DIM, tq), lambda h, i: (h, 0, i)),
                  pl.BlockSpec((1, s, QK_DIM), lambda h, i: (h, 0, 0)),
                  pl.BlockSpec((1, s // blk, V_DIM, blk), lambda h, i: (h, 0, 0, 0))],
        out_specs=pl.BlockSpec((tq, V_DIM), lambda h, i: (i, h)),
        scratch_shapes=[pltpu.VMEM((nsub, 1, blk), F32),
                        pltpu.VMEM((nsub, V_DIM + ONES_ROWS, blk), F32),
                        pltpu.VMEM((nsub, blk, blk), F32),
                        pltpu.VMEM((nsub, blk, blk), F32)],
        compiler_params=_params("parallel", "arbitrary"),
    )(q_t, k, v_t)


def _ffn_block(x, xb, w_gate, w_up, w_down, l, g, b, alpha, want_bf16):
    hidden = w_gate.shape[2]
    hp = _round_up(hidden, 512)
    wg = _cast_pad(w_gate, l, cols_p=hp)
    wu = _cast_pad(w_up, l, cols_p=hp)
    wd = _cast_pad(w_down, l, rows_p=hp)
    hmid = _ffn_up(xb, wg, wu)
    return _mm_res_ln(hmid, wd, x, g, b, alpha, want_bf16=want_bf16)


def _attn_conv_layer(x, xb, w_in, conv_w, w_out, j, g, b, alpha):
    s, d = x.shape
    a_width = d // 2
    hproj = _in_proj(xb, _cast_pad(w_in, j), _rope_tables(s, HEAD_DIM), rope_cols=2 * a_width,
                     scale=HEAD_DIM ** -0.5, scaled_cols=a_width)
    outs, lses = [], []
    for window, dilation in DILATED_PATTERNS:
        assert window // dilation == DIL_BLOCK and s % window == 0
        o, lse = _dilated_branch(hproj, a_width, dilation)
        outs.append(o)
        lses.append(lse)
    ab = _mix(outs, lses, hproj, conv_w[j], a_width)
    return _proj_res_ln(ab, _cast_pad(w_out, j), x, g, b, alpha)


def _mla_layer(x, xb, w_in, q_norm, kv_norm, w_uq, w_ukv, w_out, j, g, b, alpha):
    s, d = x.shape
    heads = d // 128
    w_in_p = _cast_pad(w_in, j, cols_p=w_in.shape[2] + LANES - QK_ROPE)
    cq, ckv, kr = _mla_in(xb, w_in_p, q_norm[j], kv_norm[j], _rope_tables(s, QK_ROPE))
    w_uq_h = w_uq[j].astype(BF16).reshape(Q_LORA, heads, QK_DIM)
    w_uq_g = jnp.concatenate([w_uq_h[:, :, :QK_NOPE].reshape(Q_LORA, heads * QK_NOPE),
                              w_uq_h[:, :, QK_NOPE:].reshape(Q_LORA, heads * QK_ROPE)], axis=1)
    q_t = _q_up_t(cq, w_uq_g, _rope_tables_t(s, QK_ROPE), heads, QK_DIM ** -0.5)
    w_ukv_h = w_ukv[j].astype(BF16).reshape(KV_LORA, heads, QK_NOPE + V_DIM)
    w_ukv_g = jnp.concatenate([w_ukv_h[:, :, :QK_NOPE].reshape(KV_LORA, heads * QK_NOPE),
                               w_ukv_h[:, :, QK_NOPE:].reshape(KV_LORA, heads * V_DIM)], axis=1)
    k, v_t = _kv_up_t(ckv, w_ukv_g, kr, heads, _tile(s, ATTN_BLOCK))
    o = _mla_attn_t(q_t, k, v_t)
    return _proj_res_ln(o, _cast_pad(w_out, j), x, g, b, alpha)


def kernel(x, w_in_a, conv_w, w_out_a, w_in_c, q_norm, kv_norm, w_uq, w_ukv, w_out_c,
           ln1_g, ln1_b, w_gate, w_up, w_down, ln2_g, ln2_b):
    batch, s, d = x.shape
    depth = ln1_g.shape[0]
    alpha = (2.0 * depth) ** 0.25
    outs = []
    for bi in range(batch):
        xf = x[bi]
        xb = xf.astype(BF16)
        for l in range(depth):
            j = l // 2
            if l % 2 == 0:
                xf, xb = _attn_conv_layer(xf, xb, w_in_a, conv_w, w_out_a, j,
                                          ln1_g[l], ln1_b[l], alpha)
            else:
                xf, xb = _mla_layer(xf, xb, w_in_c, q_norm, kv_norm, w_uq, w_ukv, w_out_c, j,
                                    ln1_g[l], ln1_b[l], alpha)
            xf, xb = _ffn_block(xf, xb, w_gate, w_up, w_down, l, ln2_g[l], ln2_b[l],
                                alpha, want_bf16=l + 1 < depth)
        outs.append(xf)
    return jnp.stack(outs)
```

```python
import functools
import math

import jax
import jax.numpy as jnp
from jax import lax
from jax.experimental import pallas as pl
from jax.experimental.pallas import tpu as pltpu

F32 = jnp.float32
BF16 = jnp.bfloat16

HEAD_DIM = 128
CONV_WIDTH = 3
DILATED_PATTERNS = ((128, 1), (512, 4), (2048, 16))
DIL_BLOCK = 128
Q_LORA = 1536
KV_LORA = 512
QK_NOPE = 128
QK_ROPE = 64
V_DIM = 128
ROPE_THETA = 10000.0
LN_EPS = 1e-5
RMS_EPS = 1e-6
NEG = -1e30

LANES = 128
SUBLANES = 8
VMEM_LIMIT_BYTES = 60 * 1024 * 1024


def _round_up(n, m):
    return -(-n // m) * m


def _tile(n, pref):
    if n <= pref:
        return n
    t = pref
    while n % t:
        t //= 2
    return t


def _params(*sem):
    return pltpu.CompilerParams(dimension_semantics=sem, vmem_limit_bytes=VMEM_LIMIT_BYTES)


def _cast_pad_kernel(w_ref, o_ref, *, rows, cols, rows_p):
    tr, cols_p = o_ref.shape
    val = w_ref[...].astype(BF16)
    if rows_p > rows:
        row = pl.program_id(0) * tr + lax.broadcasted_iota(jnp.int32, val.shape, 0)
        val = jnp.where(row < rows, val, jnp.zeros_like(val))
    o_ref[:, :cols] = val
    if cols_p > cols:
        o_ref[:, cols:] = jnp.zeros((tr, cols_p - cols), BF16)


def _cast_pad(w_stack, layer, rows_p=None, cols_p=None, tr_pref=256):
    _, rows, cols = w_stack.shape
    rows_p = rows_p or rows
    cols_p = cols_p or cols
    tr = _tile(rows_p, tr_pref)
    last = (rows - 1) // tr
    return pl.pallas_call(
        functools.partial(_cast_pad_kernel, rows=rows, cols=cols, rows_p=rows_p),
        out_shape=jax.ShapeDtypeStruct((rows_p, cols_p), BF16),
        grid=(rows_p // tr,),
        in_specs=[pl.BlockSpec((None, tr, cols), lambda i: (layer, jnp.minimum(i, last), 0))],
        out_specs=pl.BlockSpec((tr, cols_p), lambda i: (i, 0)),
        compiler_params=_params("parallel"),
    )(w_stack)


def _rope_tables(seq, dim):
    half = dim // 2
    inv = ROPE_THETA ** (-jnp.arange(half, dtype=F32) * 2.0 / dim)
    ang = jnp.arange(seq).astype(F32)[:, None] * inv[None, :]
    cos, sin = jnp.cos(ang), jnp.sin(ang)
    reps = LANES // dim
    return (jnp.concatenate([cos, cos] * reps, axis=1),
            jnp.concatenate([-sin, sin] * reps, axis=1))


def _rope_full_lanes(t, cos, sin):
    return t * cos + pltpu.roll(t, HEAD_DIM // 2, axis=1) * sin


def _rope_half_lanes(y, cos, sin):
    half = QK_ROPE // 2
    lane = lax.broadcasted_iota(jnp.int32, y.shape, 1)
    partner = jnp.where(lane % QK_ROPE < half, pltpu.roll(y, LANES - half, axis=1),
                        pltpu.roll(y, half, axis=1))
    return y * cos + partner * sin


def _in_proj_kernel(a_ref, w_ref, cos_ref, sin_ref, o_ref, *, scale, n_scaled_tiles,
                    n_rope_tiles):
    j = pl.program_id(1)
    acc = jnp.dot(a_ref[...], w_ref[...], preferred_element_type=F32)

    @pl.when(j >= n_rope_tiles)
    def _():
        o_ref[...] = acc

    @pl.when(j < n_rope_tiles)
    def _():
        cos = cos_ref[...]
        sin = sin_ref[...]
        mult = jnp.where(j < n_scaled_tiles, scale, 1.0).astype(F32)
        for c in range(acc.shape[1] // HEAD_DIM):
            sl = slice(c * HEAD_DIM, (c + 1) * HEAD_DIM)
            o_ref[:, sl] = _rope_full_lanes(acc[:, sl], cos, sin) * mult


def _in_proj(a, w, rope_tables, *, rope_cols, scale, scaled_cols, tm_pref=1024, tn_pref=1024):
    m, k = a.shape
    n = w.shape[1]
    tm = _tile(m, tm_pref)
    tn = _tile(math.gcd(n, rope_cols, scaled_cols), tn_pref)
    return pl.pallas_call(
        functools.partial(_in_proj_kernel, scale=scale, n_scaled_tiles=scaled_cols // tn,
                          n_rope_tiles=rope_cols // tn),
        out_shape=jax.ShapeDtypeStruct((m, n), F32),
        grid=(m // tm, n // tn),
        in_specs=[pl.BlockSpec((tm, k), lambda i, j: (i, 0)),
                  pl.BlockSpec((k, tn), lambda i, j: (0, j)),
                  pl.BlockSpec((tm, LANES), lambda i, j: (i, 0)),
                  pl.BlockSpec((tm, LANES), lambda i, j: (i, 0))],
        out_specs=pl.BlockSpec((tm, tn), lambda i, j: (i, j)),
        compiler_params=_params("parallel", "arbitrary"),
    )(a, w, *rope_tables)


DIL_ROWS = 2048
DIL_STAGE_STRIDE = 4


MIX_ROW_CHUNK = 256


def _dilated_mix_kernel(q_ref, kp_ref, kc_ref, vp_ref, vc_ref, a_ref, *scratch, dilations):
    n = pl.program_id(0)
    rows = q_ref.shape[0]
    blk = DIL_BLOCK
    nb = len(dilations)
    o_nat, l_nat, stage_bufs = scratch[:nb], scratch[nb:2 * nb], scratch[2 * nb:]
    qi = lax.broadcasted_iota(jnp.int32, (blk, 2 * blk), 0)
    ki = lax.broadcasted_iota(jnp.int32, (blk, 2 * blk), 1)
    dist = qi + blk - ki
    band = (dist >= 0) & (dist <= blk)
    band_first = band & ((ki >= blk) | (n > 0))
    inner = DIL_STAGE_STRIDE

    for bi, d in enumerate(dilations):
        span = blk * d
        groups = rows // span
        staged = d % (4 * inner) == 0
        if staged:
            sq, skp, skc, svp, svc, so, sl = stage_bufs
            for src, dst in ((q_ref, sq), (kp_ref, skp), (kc_ref, skc), (vp_ref, svp),
                             (vc_ref, svc)):
                for rr in range(inner):
                    dst[rr] = src[pl.ds(rr, rows // inner, stride=inner), :]
        else:
            sq, skp, skc, svp, svc = q_ref, kp_ref, kc_ref, vp_ref, vc_ref
            so, sl = o_nat[bi], l_nat[bi]

        def rows_of(ref, start, r, d=d, staged=staged):
            if not staged:
                return ref, (pl.ds(start + r, blk, stride=d) if d > 1 else pl.ds(start + r, blk))
            return ref.at[r % inner], pl.ds(start // inner + r // inner, blk, stride=d // inner)

        def read(ref, start, r, rows_of=rows_of):
            view, idx = rows_of(ref, start, r)
            return view[idx, :]

        for g in range(groups):
            for r in range(d):
                q = read(sq, g * span, r).astype(BF16)
                if g == 0:
                    tail = (groups - 1) * span
                    k_prev, v_prev = read(skp, tail, r), read(svp, tail, r)
                else:
                    k_prev, v_prev = read(skc, (g - 1) * span, r), read(svc, (g - 1) * span, r)
                kk = jnp.concatenate([k_prev, read(skc, g * span, r)], axis=0).astype(BF16)
                vv = jnp.concatenate([v_prev, read(svc, g * span, r)], axis=0).astype(BF16)
                s = lax.dot_general(q, kk, (((1,), (1,)), ((), ())),
                                    preferred_element_type=F32)
                s = jnp.where(band_first if g == 0 else band, s, NEG)
                m = jnp.max(s, axis=-1, keepdims=True)
                p = jnp.exp(s - m)
                den = jnp.sum(p, axis=-1, keepdims=True)
                o = jnp.dot(p.astype(BF16), vv, preferred_element_type=F32)
                o_view, idx = rows_of(so, g * span, r)
                o_view[idx, :] = o / den
                l_view, idx = rows_of(sl, g * span, r)
                l_view[idx, :] = jnp.broadcast_to(m + jnp.log(den), (blk, LANES))

        if staged:
            for rr in range(inner):
                dst_rows = pl.ds(rr, rows // inner, stride=inner)
                o_nat[bi][dst_rows, :] = so[rr]
                l_nat[bi][dst_rows, :] = sl[rr]

    ch = _tile(rows, MIX_ROW_CHUNK)
    for c in range(rows // ch):
        rs = slice(c * ch, (c + 1) * ch)
        lses = [l[rs, :] for l in l_nat]
        top = functools.reduce(jnp.maximum, lses)
        es = [jnp.exp(l - top) for l in lses]
        num = functools.reduce(lambda x, y: x + y, [e * o[rs, :] for e, o in zip(es, o_nat)])
        den = functools.reduce(lambda x, y: x + y, es)
        a_ref[rs, :] = (num / den).astype(a_ref.dtype)


def _dilated_mix(hproj, a_width, dilations):
    s = hproj.shape[0]
    heads = a_width // HEAD_DIM
    rows = _tile(s, DIL_ROWS)
    assert all(rows % (DIL_BLOCK * d) == 0 for d in dilations)
    cur = lambda col0: pl.BlockSpec((rows, HEAD_DIM), lambda n, h: (n, col0 + h))
    prev = lambda col0: pl.BlockSpec((rows, HEAD_DIM),
                                     lambda n, h: (jnp.maximum(n - 1, 0), col0 + h))
    natural = pltpu.VMEM((rows, HEAD_DIM), F32)
    staged = pltpu.VMEM((DIL_STAGE_STRIDE, rows // DIL_STAGE_STRIDE, HEAD_DIM), F32)
    return pl.pallas_call(
        functools.partial(_dilated_mix_kernel, dilations=tuple(dilations)),
        out_shape=jax.ShapeDtypeStruct((s, a_width), BF16),
        grid=(s // rows, heads),
        in_specs=[cur(0), prev(heads), cur(heads), prev(2 * heads), cur(2 * heads)],
        out_specs=pl.BlockSpec((rows, HEAD_DIM), lambda n, h: (n, h)),
        scratch_shapes=[natural] * (2 * len(dilations)) + [staged] * 7,
        compiler_params=_params("parallel", "arbitrary"),
    )(hproj, hproj, hproj, hproj, hproj)


def _conv_gate_kernel(gb_ref, gc_ref, hin_ref, gch_ref, hinh_ref, cw_ref, out_ref):
    i = pl.program_id(0)
    u = gc_ref[...] * hin_ref[...]
    halo = gch_ref[...] * hinh_ref[...]
    halo = halo * jnp.where(i > 0, 1.0, 0.0).astype(F32)
    row = lax.broadcasted_iota(jnp.int32, u.shape, 0)
    hm1 = halo[SUBLANES - 1:SUBLANES, :]
    hm2 = halo[SUBLANES - 2:SUBLANES - 1, :]
    u1 = jnp.where(row == 0, hm1, pltpu.roll(u, 1, axis=0))
    u2 = jnp.where(row == 0, hm2, jnp.where(row == 1, hm1, pltpu.roll(u, 2, axis=0)))
    cw = cw_ref[...]
    y = cw[0:1, :] * u2 + cw[1:2, :] * u1 + cw[2:3, :] * u
    out_ref[...] = (gb_ref[...] * y).astype(out_ref.dtype)


def _conv_gate(hproj, conv_w, a_width, tm_pref=512):
    s = hproj.shape[0]
    b_width = (hproj.shape[1] - 3 * a_width) // 3
    assert b_width == a_width, "gate blocks are addressed in units of the attention width"
    gb_blk, gc_blk, hin_blk = 3, 4, 5
    tm = _tile(s, tm_pref)
    hb = tm // SUBLANES
    row_blk = lambda c: pl.BlockSpec((tm, b_width), lambda i: (i, c))
    halo_blk = lambda c: pl.BlockSpec((SUBLANES, b_width),
                                      lambda i: (jnp.maximum(i * hb - 1, 0), c))
    return pl.pallas_call(
        _conv_gate_kernel,
        out_shape=jax.ShapeDtypeStruct((s, b_width), BF16),
        grid=(s // tm,),
        in_specs=[row_blk(gb_blk), row_blk(gc_blk), row_blk(hin_blk), halo_blk(gc_blk),
                  halo_blk(hin_blk), pl.BlockSpec((CONV_WIDTH, b_width), lambda i: (0, 0))],
        out_specs=pl.BlockSpec((tm, b_width), lambda i: (i, 0)),
        compiler_params=_params("parallel"),
    )(hproj, hproj, hproj, hproj, hproj, conv_w)


LN_COL_CHUNK = 1024
LN_ROW_CHUNK = 8


def _residual_ln(x_ref, g_ref, b_ref, o_ref, ob_ref, alpha):
    tm = o_ref.shape[0]
    rc = _tile(tm, LN_ROW_CHUNK)
    for r in range(tm // rc):
        rows = slice(r * rc, (r + 1) * rc)
        z = alpha * x_ref[rows, :] + o_ref[rows, :]
        mu = jnp.mean(z, axis=-1, keepdims=True)
        zc = z - mu
        var = jnp.mean(zc * zc, axis=-1, keepdims=True)
        y = zc * lax.rsqrt(var + LN_EPS) * g_ref[...] + b_ref[...]
        o_ref[rows, :] = y
        if ob_ref is not None:
            ob_ref[rows, :] = y.astype(BF16)


def _mm_res_ln_kernel(a_ref, w_ref, x_ref, g_ref, b_ref, o_ref, *maybe_ob, alpha):
    k = pl.program_id(1)
    n = o_ref.shape[1]

    @pl.when(k == 0)
    def _():
        o_ref[...] = jnp.zeros(o_ref.shape, F32)

    a = a_ref[...]
    nc = _tile(n, LN_COL_CHUNK)
    for c in range(n // nc):
        sl = slice(c * nc, (c + 1) * nc)
        o_ref[:, sl] += jnp.dot(a, w_ref[:, sl], preferred_element_type=F32)

    @pl.when(k == pl.num_programs(1) - 1)
    def _():
        _residual_ln(x_ref, g_ref, b_ref, o_ref, maybe_ob[0] if maybe_ob else None, alpha)


def _mm_res_ln(a, w, xres, g, b, alpha, *, want_bf16, tm_pref=512, tk_pref=512):
    m, k = a.shape
    n = w.shape[1]
    tm = _tile(m, tm_pref)
    tk = _tile(k, tk_pref)
    out_shape = [jax.ShapeDtypeStruct((m, n), F32)]
    out_specs = [pl.BlockSpec((tm, n), lambda i, kk: (i, 0))]
    if want_bf16:
        out_shape.append(jax.ShapeDtypeStruct((m, n), BF16))
        out_specs.append(pl.BlockSpec((tm, n), lambda i, kk: (i, 0)))
    res = pl.pallas_call(
        functools.partial(_mm_res_ln_kernel, alpha=alpha),
        out_shape=tuple(out_shape),
        grid=(m // tm, k // tk),
        in_specs=[
            pl.BlockSpec((tm, tk), lambda i, kk: (i, kk)),
            pl.BlockSpec((tk, n), lambda i, kk: (kk, 0)),
            pl.BlockSpec((tm, n), lambda i, kk: (i, 0)),
            pl.BlockSpec((1, n), lambda i, kk: (0, 0)),
            pl.BlockSpec((1, n), lambda i, kk: (0, 0)),
        ],
        out_specs=tuple(out_specs),
        compiler_params=_params("parallel", "arbitrary"),
    )(a, w, xres, g.reshape(1, n), b.reshape(1, n))
    return res if want_bf16 else (res[0], None)


def _proj_res_ln_kernel(*refs, n_parts, alpha):
    a_refs = refs[:n_parts]
    w_ref, x_ref, g_ref, b_ref, o_ref, ob_ref = refs[n_parts:]
    n = o_ref.shape[1]
    nc = _tile(n, LN_COL_CHUNK)
    for c in range(n // nc):
        sl = slice(c * nc, (c + 1) * nc)
        k0 = 0
        acc = None
        for a_ref in a_refs:
            kw = a_ref.shape[1]
            part = jnp.dot(a_ref[...], w_ref[k0:k0 + kw, sl], preferred_element_type=F32)
            acc = part if acc is None else acc + part
            k0 += kw
        o_ref[:, sl] = acc
    _residual_ln(x_ref, g_ref, b_ref, o_ref, ob_ref, alpha)


def _proj_res_ln(parts, w, xres, g, b, alpha, tm_pref=128):
    m = parts[0].shape[0]
    k, n = w.shape
    assert sum(p.shape[1] for p in parts) == k
    tm = _tile(m, tm_pref)
    row = lambda width: pl.BlockSpec((tm, width), lambda i: (i, 0))
    vec = pl.BlockSpec((1, n), lambda i: (0, 0))
    return pl.pallas_call(
        functools.partial(_proj_res_ln_kernel, n_parts=len(parts), alpha=alpha),
        out_shape=(jax.ShapeDtypeStruct((m, n), F32), jax.ShapeDtypeStruct((m, n), BF16)),
        grid=(m // tm,),
        in_specs=[row(p.shape[1]) for p in parts]
        + [pl.BlockSpec((k, n), lambda i: (0, 0), pipeline_mode=pl.Buffered(1)),
           row(n), vec, vec],
        out_specs=(row(n), row(n)),
        compiler_params=_params("arbitrary"),
    )(*parts, w, xres, g.reshape(1, n), b.reshape(1, n))


def _ffn_up_kernel(a_ref, wg_ref, wu_ref, o_ref):
    a = a_ref[...]
    g = jnp.dot(a, wg_ref[...], preferred_element_type=F32)
    u = jnp.dot(a, wu_ref[...], preferred_element_type=F32)
    o_ref[...] = (g * jax.nn.sigmoid(g) * u).astype(o_ref.dtype)


def _ffn_up(a, wg, wu, tm_pref=1024, tn_pref=512):
    m, k = a.shape
    n = wg.shape[1]
    tm = _tile(m, tm_pref)
    tn = _tile(n, tn_pref)
    return pl.pallas_call(
        _ffn_up_kernel,
        out_shape=jax.ShapeDtypeStruct((m, n), BF16),
        grid=(m // tm, n // tn),
        in_specs=[pl.BlockSpec((tm, k), lambda i, j: (i, 0)),
                  pl.BlockSpec((k, tn), lambda i, j: (0, j)),
                  pl.BlockSpec((k, tn), lambda i, j: (0, j))],
        out_specs=pl.BlockSpec((tm, tn), lambda i, j: (i, j)),
        compiler_params=_params("parallel", "arbitrary"),
    )(a, wg, wu)


def _mla_in_kernel(a_ref, w_ref, qn_ref, kvn_ref, cos_ref, sin_ref, cq_ref, ckv_ref, kr_ref):
    def rms(t, gain):
        r = lax.rsqrt(jnp.mean(t * t, axis=-1, keepdims=True) + RMS_EPS)
        return t * r * gain

    a = a_ref[...]
    kv0 = Q_LORA + KV_LORA
    cq = jnp.dot(a, w_ref[:, :Q_LORA], preferred_element_type=F32)
    cq_ref[...] = rms(cq, qn_ref[...]).astype(BF16)
    ckv = jnp.dot(a, w_ref[:, Q_LORA:kv0], preferred_element_type=F32)
    ckv_ref[...] = rms(ckv, kvn_ref[...]).astype(BF16)
    y = jnp.dot(a, w_ref[:, kv0:], preferred_element_type=F32)
    kr_ref[...] = _rope_half_lanes(y, cos_ref[...], sin_ref[...]).astype(BF16)


def _mla_in(a, w, q_norm, kv_norm, tables, tm_pref=512):
    m, k = a.shape
    n = w.shape[1]
    tm = _tile(m, tm_pref)
    row = lambda width: pl.BlockSpec((tm, width), lambda i: (i, 0))
    return pl.pallas_call(
        _mla_in_kernel,
        out_shape=(jax.ShapeDtypeStruct((m, Q_LORA), BF16),
                   jax.ShapeDtypeStruct((m, KV_LORA), BF16),
                   jax.ShapeDtypeStruct((m, LANES), BF16)),
        grid=(m // tm,),
        in_specs=[row(k),
                  pl.BlockSpec((k, n), lambda i: (0, 0), pipeline_mode=pl.Buffered(1)),
                  pl.BlockSpec((1, Q_LORA), lambda i: (0, 0)),
                  pl.BlockSpec((1, KV_LORA), lambda i: (0, 0)),
                  row(LANES), row(LANES)],
        out_specs=(row(Q_LORA), row(KV_LORA), row(LANES)),
        compiler_params=_params("arbitrary"),
    )(a, w, q_norm.reshape(1, Q_LORA), kv_norm.reshape(1, KV_LORA), *tables)


QK_DIM = QK_NOPE + QK_ROPE
UP_HEADS_PER_DOT = 4


ATTN_BLOCK = 512
ONES_ROWS = 16


def _rope_tables_t(seq, dim):
    half = dim // 2
    inv = ROPE_THETA ** (-jnp.arange(half, dtype=F32) * 2.0 / dim)
    ang = inv[:, None] * jnp.arange(seq).astype(F32)[None, :]
    return jnp.cos(ang), jnp.sin(ang)


def _q_up_t_kernel(a_ref, wn_ref, wr_ref, cos_ref, sin_ref, q_ref, *, heads_per_step, scale):
    a = a_ref[...]
    cos, sin = cos_ref[...], sin_ref[...]
    hpd = UP_HEADS_PER_DOT
    half = QK_ROPE // 2
    tn = (((0,), (1,)), ((), ()))
    for c in range(heads_per_step // hpd):
        nope = lax.dot_general(wn_ref[:, c * hpd * QK_NOPE:(c + 1) * hpd * QK_NOPE], a, tn,
                               preferred_element_type=F32)
        rope = lax.dot_general(wr_ref[:, c * hpd * QK_ROPE:(c + 1) * hpd * QK_ROPE], a, tn,
                               preferred_element_type=F32)
        for h in range(hpd):
            y1 = rope[h * QK_ROPE:h * QK_ROPE + half, :]
            y2 = rope[h * QK_ROPE + half:(h + 1) * QK_ROPE, :]
            q_ref[c * hpd + h, :QK_NOPE, :] = (
                nope[h * QK_NOPE:(h + 1) * QK_NOPE, :] * scale).astype(BF16)
            q_ref[c * hpd + h, QK_NOPE:QK_NOPE + half, :] = (
                (y1 * cos - y2 * sin) * scale).astype(BF16)
            q_ref[c * hpd + h, QK_NOPE + half:, :] = ((y2 * cos + y1 * sin) * scale).astype(BF16)


def _q_up_t(cq, w_uq_g, tables_t, heads, scale, tm_pref=1024, hps_pref=8):
    m, k = cq.shape
    tm = _tile(m, tm_pref)
    hps = _tile(heads, hps_pref)
    assert hps % UP_HEADS_PER_DOT == 0
    half = QK_ROPE // 2
    rope_blk0 = heads * QK_NOPE // (hps * QK_ROPE)
    return pl.pallas_call(
        functools.partial(_q_up_t_kernel, heads_per_step=hps, scale=scale),
        out_shape=jax.ShapeDtypeStruct((heads, QK_DIM, m), BF16),
        grid=(m // tm, heads // hps),
        in_specs=[pl.BlockSpec((tm, k), lambda i, j: (i, 0)),
                  pl.BlockSpec((k, hps * QK_NOPE), lambda i, j: (0, j)),
                  pl.BlockSpec((k, hps * QK_ROPE), lambda i, j: (0, rope_blk0 + j)),
                  pl.BlockSpec((half, tm), lambda i, j: (0, i)),
                  pl.BlockSpec((half, tm), lambda i, j: (0, i))],
        out_specs=pl.BlockSpec((hps, QK_DIM, tm), lambda i, j: (j, 0, i)),
        compiler_params=_params("parallel", "arbitrary"),
    )(cq, w_uq_g, w_uq_g, *tables_t)


def _kv_up_t_kernel(a_ref, wk_ref, wv_ref, kr_ref, k_ref, v_ref, *, heads_per_step, blk):
    a = a_ref[...]
    kr = kr_ref[:, :QK_ROPE]
    hpd = UP_HEADS_PER_DOT
    tn = (((0,), (1,)), ((), ()))
    for c in range(heads_per_step // hpd):
        kn = jnp.dot(a, wk_ref[:, c * hpd * QK_NOPE:(c + 1) * hpd * QK_NOPE],
                     preferred_element_type=F32)
        vt = lax.dot_general(wv_ref[:, c * hpd * V_DIM:(c + 1) * hpd * V_DIM], a, tn,
                             preferred_element_type=F32)
        for h in range(hpd):
            k_ref[c * hpd + h, :, :QK_NOPE] = kn[:, h * QK_NOPE:(h + 1) * QK_NOPE].astype(BF16)
            k_ref[c * hpd + h, :, QK_NOPE:] = kr
            for b in range(a.shape[0] // blk):
                v_ref[c * hpd + h, b] = vt[h * V_DIM:(h + 1) * V_DIM,
                                           b * blk:(b + 1) * blk].astype(BF16)


def _kv_up_t(ckv, w_ukv_g, kr, heads, blk, tm_pref=1024, hps_pref=16):
    m, k = ckv.shape
    tm = _tile(m, tm_pref)
    hps = _tile(heads, hps_pref)
    assert hps % UP_HEADS_PER_DOT == 0 and tm % blk == 0
    v_blk0 = heads * QK_NOPE // (hps * V_DIM)
    return pl.pallas_call(
        functools.partial(_kv_up_t_kernel, heads_per_step=hps, blk=blk),
        out_shape=(jax.ShapeDtypeStruct((heads, m, QK_DIM), BF16),
                   jax.ShapeDtypeStruct((heads, m // blk, V_DIM, blk), BF16)),
        grid=(m // tm, heads // hps),
        in_specs=[pl.BlockSpec((tm, k), lambda i, j: (i, 0)),
                  pl.BlockSpec((k, hps * QK_NOPE), lambda i, j: (0, j)),
                  pl.BlockSpec((k, hps * V_DIM), lambda i, j: (0, v_blk0 + j)),
                  pl.BlockSpec((tm, LANES), lambda i, j: (i, 0))],
        out_specs=(pl.BlockSpec((hps, tm, QK_DIM), lambda i, j: (j, i, 0)),
                   pl.BlockSpec((hps, tm // blk, V_DIM, blk), lambda i, j: (j, i, 0, 0))),
        compiler_params=_params("parallel", "arbitrary"),
    )(ckv, w_ukv_g, w_ukv_g, kr)


def _mla_attn_t_kernel(q_ref, k_ref, v_ref, o_ref, m_ref, acc_ref, s0_ref, s1_ref, *, blk, nsub):
    qi = pl.program_id(1)
    m_ref[...] = jnp.full(m_ref.shape, NEG, F32)
    acc_ref[...] = jnp.zeros(acc_ref.shape, F32)
    ones = jnp.ones((ONES_ROWS, blk), BF16)
    slots = (s0_ref, s1_ref)

    def scores(sub, j, slot):
        start = pl.multiple_of(j * blk, blk)
        kk = k_ref[0, pl.ds(start, blk), :]
        qt = q_ref[0, :, sub * blk:(sub + 1) * blk]
        slot[sub] = jnp.dot(kk, qt, preferred_element_type=F32)

    def softmax_pv(sub, j, slot, masked):
        vt = jnp.concatenate([v_ref[0, j], ones], axis=0)
        s = slot[sub]
        if masked:
            key = lax.broadcasted_iota(jnp.int32, s.shape, 0)
            qry = lax.broadcasted_iota(jnp.int32, s.shape, 1)
            s = jnp.where(key <= qry, s, NEG)
        m_prev = m_ref[sub]
        m_new = jnp.maximum(m_prev, jnp.max(s, axis=0, keepdims=True))
        corr = jnp.exp(m_prev - m_new)
        p = jnp.exp(s - m_new).astype(BF16)
        acc_ref[sub] = corr * acc_ref[sub] + jnp.dot(vt, p, preferred_element_type=F32)
        m_ref[sub] = m_new

    for sub in range(nsub):
        scores(sub, 0, slots[0])

    def full_blocks(t):
        for c in range(nsub):
            j = t * nsub + c
            for sub in range(nsub):
                scores(sub, j + 1, slots[(c + 1) % 2])
                softmax_pv(sub, j, slots[c % 2], masked=False)

    def body(t, carry):
        full_blocks(t)
        return carry

    lax.fori_loop(0, qi, body, 0)
    base = qi * nsub
    for c in range(nsub):
        for sub in range(c, nsub):
            if sub > c:
                scores(sub, base + c + 1, slots[(c + 1) % 2])
            softmax_pv(sub, base + c, slots[c % 2], masked=(sub == c))
    for sub in range(nsub):
        acc = acc_ref[sub]
        o_t = acc[:V_DIM, :] / acc[V_DIM:V_DIM + 1, :]
        o_ref[sub * blk:(sub + 1) * blk, :] = o_t.T.astype(o_ref.dtype)


def _mla_attn_t(q_t, k, v_t, nsub_pref=4):
    heads, s, _ = k.shape
    blk = v_t.shape[3]
    nsub = _tile(s // blk, nsub_pref)
    assert nsub % 2 == 0, "score slots alternate with key-block parity"
    tq = blk * nsub
    return pl.pallas_call(
        functools.partial(_mla_attn_t_kernel, blk=blk, nsub=nsub),
        out_shape=jax.ShapeDtypeStruct((s, heads * V_DIM), BF16),
        grid=(heads, s // tq),
        in_specs=[pl.BlockSpec((1, QK_DIM, tq), lambda h, i: (h, 0, i)),
                  pl.BlockSpec((1, s, QK_DIM), lambda h, i: (h, 0, 0)),
                  pl.BlockSpec((1, s // blk, V_DIM, blk), lambda h, i: (h, 0, 0, 0))],
        out_specs=pl.BlockSpec((tq, V_DIM), lambda h, i: (i, h)),
        scratch_shapes=[pltpu.VMEM((nsub, 1, blk), F32),
                        pltpu.VMEM((nsub, V_DIM + ONES_ROWS, blk), F32),
                        pltpu.VMEM((nsub, blk, blk), F32),
                        pltpu.VMEM((nsub, blk, blk), F32)],
        compiler_params=_params("parallel", "arbitrary"),
    )(q_t, k, v_t)


def _ffn_block(x, xb, w_gate, w_up, w_down, l, g, b, alpha, want_bf16):
    hidden = w_gate.shape[2]
    hp = _round_up(hidden, 512)
    wg = _cast_pad(w_gate, l, cols_p=hp)
    wu = _cast_pad(w_up, l, cols_p=hp)
    wd = _cast_pad(w_down, l, rows_p=hp)
    hmid = _ffn_up(xb, wg, wu)
    return _mm_res_ln(hmid, wd, x, g, b, alpha, want_bf16=want_bf16)


def _attn_conv_layer(x, xb, w_in, conv_w, w_out, j, g, b, alpha):
    s, d = x.shape
    a_width = d // 2
    hproj = _in_proj(xb, _cast_pad(w_in, j), _rope_tables(s, HEAD_DIM), rope_cols=2 * a_width,
                     scale=HEAD_DIM ** -0.5, scaled_cols=a_width)
    assert all(window // d == DIL_BLOCK and s % window == 0 for window, d in DILATED_PATTERNS)
    attn = _dilated_mix(hproj, a_width, [d for _, d in DILATED_PATTERNS])
    gated = _conv_gate(hproj, conv_w[j], a_width)
    return _proj_res_ln([attn, gated], _cast_pad(w_out, j), x, g, b, alpha)


def _mla_layer(x, xb, w_in, q_norm, kv_norm, w_uq, w_ukv, w_out, j, g, b, alpha):
    s, d = x.shape
    heads = d // 128
    w_in_p = _cast_pad(w_in, j, cols_p=w_in.shape[2] + LANES - QK_ROPE)
    cq, ckv, kr = _mla_in(xb, w_in_p, q_norm[j], kv_norm[j], _rope_tables(s, QK_ROPE))
    w_uq_h = w_uq[j].astype(BF16).reshape(Q_LORA, heads, QK_DIM)
    w_uq_g = jnp.concatenate([w_uq_h[:, :, :QK_NOPE].reshape(Q_LORA, heads * QK_NOPE),
                              w_uq_h[:, :, QK_NOPE:].reshape(Q_LORA, heads * QK_ROPE)], axis=1)
    q_t = _q_up_t(cq, w_uq_g, _rope_tables_t(s, QK_ROPE), heads, QK_DIM ** -0.5)
    w_ukv_h = w_ukv[j].astype(BF16).reshape(KV_LORA, heads, QK_NOPE + V_DIM)
    w_ukv_g = jnp.concatenate([w_ukv_h[:, :, :QK_NOPE].reshape(KV_LORA, heads * QK_NOPE),
                               w_ukv_h[:, :, QK_NOPE:].reshape(KV_LORA, heads * V_DIM)], axis=1)
    k, v_t = _kv_up_t(ckv, w_ukv_g, kr, heads, _tile(s, ATTN_BLOCK))
    o = _mla_attn_t(q_t, k, v_t)
    return _proj_res_ln([o], _cast_pad(w_out, j), x, g, b, alpha)


def kernel(x, w_in_a, conv_w, w_out_a, w_in_c, q_norm, kv_norm, w_uq, w_ukv, w_out_c,
           ln1_g, ln1_b, w_gate, w_up, w_down, ln2_g, ln2_b):
    batch, s, d = x.shape
    depth = ln1_g.shape[0]
    alpha = (2.0 * depth) ** 0.25
    outs = []
    for bi in range(batch):
        xf = x[bi]
        xb = xf.astype(BF16)
        for l in range(depth):
            j = l // 2
            if l % 2 == 0:
                xf, xb = _attn_conv_layer(xf, xb, w_in_a, conv_w, w_out_a, j,
                                          ln1_g[l], ln1_b[l], alpha)
            else:
                xf, xb = _mla_layer(xf, xb, w_in_c, q_norm, kv_norm, w_uq, w_ukv, w_out_c, j,
                                    ln1_g[l], ln1_b[l], alpha)
            xf, xb = _ffn_block(xf, xb, w_gate, w_up, w_down, l, ln2_g[l], ln2_b[l],
                                alpha, want_bf16=l + 1 < depth)
        outs.append(xf)
    return jnp.stack(outs)
```

```python
import functools
import math

import jax
import jax.numpy as jnp
from jax import lax
from jax.experimental import pallas as pl
from jax.experimental.pallas import tpu as pltpu

F32 = jnp.float32
BF16 = jnp.bfloat16

HEAD_DIM = 128
CONV_WIDTH = 3
DILATED_PATTERNS = ((128, 1), (512, 4), (2048, 16))
DIL_BLOCK = 128
Q_LORA = 1536
KV_LORA = 512
QK_NOPE = 128
QK_ROPE = 64
V_DIM = 128
ROPE_THETA = 10000.0
LN_EPS = 1e-5
RMS_EPS = 1e-6
NEG = -1e30

LANES = 128
SUBLANES = 8
VMEM_LIMIT_BYTES = 60 * 1024 * 1024


def _round_up(n, m):
    return -(-n // m) * m


def _tile(n, pref):
    if n <= pref:
        return n
    t = pref
    while n % t:
        t //= 2
    return t


def _params(*sem):
    return pltpu.CompilerParams(dimension_semantics=sem, vmem_limit_bytes=VMEM_LIMIT_BYTES)


def _cast_pad_kernel(w_ref, o_ref, *, rows, cols, rows_p):
    tr, cols_p = o_ref.shape
    val = w_ref[...].astype(BF16)
    if rows_p > rows:
        row = pl.program_id(0) * tr + lax.broadcasted_iota(jnp.int32, val.shape, 0)
        val = jnp.where(row < rows, val, jnp.zeros_like(val))
    o_ref[:, :cols] = val
    if cols_p > cols:
        o_ref[:, cols:] = jnp.zeros((tr, cols_p - cols), BF16)


def _cast_pad(w_stack, layer, rows_p=None, cols_p=None, tr_pref=256):
    _, rows, cols = w_stack.shape
    rows_p = rows_p or rows
    cols_p = cols_p or cols
    tr = _tile(rows_p, tr_pref)
    last = (rows - 1) // tr
    return pl.pallas_call(
        functools.partial(_cast_pad_kernel, rows=rows, cols=cols, rows_p=rows_p),
        out_shape=jax.ShapeDtypeStruct((rows_p, cols_p), BF16),
        grid=(rows_p // tr,),
        in_specs=[pl.BlockSpec((None, tr, cols), lambda i: (layer, jnp.minimum(i, last), 0))],
        out_specs=pl.BlockSpec((tr, cols_p), lambda i: (i, 0)),
        compiler_params=_params("parallel"),
    )(w_stack)


def _rope_tables(seq, dim):
    half = dim // 2
    inv = ROPE_THETA ** (-jnp.arange(half, dtype=F32) * 2.0 / dim)
    ang = jnp.arange(seq).astype(F32)[:, None] * inv[None, :]
    cos, sin = jnp.cos(ang), jnp.sin(ang)
    reps = LANES // dim
    return (jnp.concatenate([cos, cos] * reps, axis=1),
            jnp.concatenate([-sin, sin] * reps, axis=1))


def _rope_full_lanes(t, cos, sin):
    return t * cos + pltpu.roll(t, HEAD_DIM // 2, axis=1) * sin


def _rope_half_lanes(y, cos, sin):
    half = QK_ROPE // 2
    lane = lax.broadcasted_iota(jnp.int32, y.shape, 1)
    partner = jnp.where(lane % QK_ROPE < half, pltpu.roll(y, LANES - half, axis=1),
                        pltpu.roll(y, half, axis=1))
    return y * cos + partner * sin


def _in_proj_kernel(a_ref, w_ref, cos_ref, sin_ref, o_ref, *, scale, n_scaled_tiles,
                    n_rope_tiles):
    j = pl.program_id(1)
    acc = jnp.dot(a_ref[...], w_ref[...], preferred_element_type=F32)

    @pl.when(j >= n_rope_tiles)
    def _():
        o_ref[...] = acc

    @pl.when(j < n_rope_tiles)
    def _():
        cos = cos_ref[...]
        sin = sin_ref[...]
        mult = jnp.where(j < n_scaled_tiles, scale, 1.0).astype(F32)
        for c in range(acc.shape[1] // HEAD_DIM):
            sl = slice(c * HEAD_DIM, (c + 1) * HEAD_DIM)
            o_ref[:, sl] = _rope_full_lanes(acc[:, sl], cos, sin) * mult


def _in_proj(a, w, rope_tables, *, ncols, rope_cols, scale, scaled_cols, tm_pref=1024,
             tn_pref=1024):
    m, k = a.shape
    n = ncols
    tm = _tile(m, tm_pref)
    tn = _tile(math.gcd(n, rope_cols, scaled_cols), tn_pref)
    return pl.pallas_call(
        functools.partial(_in_proj_kernel, scale=scale, n_scaled_tiles=scaled_cols // tn,
                          n_rope_tiles=rope_cols // tn),
        out_shape=jax.ShapeDtypeStruct((m, n), F32),
        grid=(m // tm, n // tn),
        in_specs=[pl.BlockSpec((tm, k), lambda i, j: (i, 0)),
                  pl.BlockSpec((k, tn), lambda i, j: (0, j)),
                  pl.BlockSpec((tm, LANES), lambda i, j: (i, 0)),
                  pl.BlockSpec((tm, LANES), lambda i, j: (i, 0))],
        out_specs=pl.BlockSpec((tm, tn), lambda i, j: (i, j)),
        compiler_params=_params("parallel", "arbitrary"),
    )(a, w, *rope_tables)


DIL_ROWS = 2048
DIL_STAGE_STRIDE = 4


MIX_ROW_CHUNK = 256


def _dilated_mix_kernel(q_ref, kp_ref, kc_ref, vp_ref, vc_ref, a_ref, *scratch, dilations):
    n = pl.program_id(0)
    rows = q_ref.shape[0]
    blk = DIL_BLOCK
    nb = len(dilations)
    o_nat, l_nat, stage_bufs = scratch[:nb], scratch[nb:2 * nb], scratch[2 * nb:]
    qi = lax.broadcasted_iota(jnp.int32, (blk, 2 * blk), 0)
    ki = lax.broadcasted_iota(jnp.int32, (blk, 2 * blk), 1)
    dist = qi + blk - ki
    band = (dist >= 0) & (dist <= blk)
    band_first = band & ((ki >= blk) | (n > 0))
    inner = DIL_STAGE_STRIDE

    for bi, d in enumerate(dilations):
        span = blk * d
        groups = rows // span
        staged = d % (4 * inner) == 0
        if staged:
            sq, skp, skc, svp, svc, so, sl = stage_bufs
            for src, dst in ((q_ref, sq), (kp_ref, skp), (kc_ref, skc), (vp_ref, svp),
                             (vc_ref, svc)):
                for rr in range(inner):
                    dst[rr] = src[pl.ds(rr, rows // inner, stride=inner), :]
        else:
            sq, skp, skc, svp, svc = q_ref, kp_ref, kc_ref, vp_ref, vc_ref
            so, sl = o_nat[bi], l_nat[bi]

        def rows_of(ref, start, r, d=d, staged=staged):
            if not staged:
                return ref, (pl.ds(start + r, blk, stride=d) if d > 1 else pl.ds(start + r, blk))
            return ref.at[r % inner], pl.ds(start // inner + r // inner, blk, stride=d // inner)

        def read(ref, start, r, rows_of=rows_of):
            view, idx = rows_of(ref, start, r)
            return view[idx, :]

        for g in range(groups):
            for r in range(d):
                q = read(sq, g * span, r).astype(BF16)
                if g == 0:
                    tail = (groups - 1) * span
                    k_prev, v_prev = read(skp, tail, r), read(svp, tail, r)
                else:
                    k_prev, v_prev = read(skc, (g - 1) * span, r), read(svc, (g - 1) * span, r)
                kk = jnp.concatenate([k_prev, read(skc, g * span, r)], axis=0).astype(BF16)
                vv = jnp.concatenate([v_prev, read(svc, g * span, r)], axis=0).astype(BF16)
                s = lax.dot_general(q, kk, (((1,), (1,)), ((), ())),
                                    preferred_element_type=F32)
                s = jnp.where(band_first if g == 0 else band, s, NEG)
                m = jnp.max(s, axis=-1, keepdims=True)
                p = jnp.exp(s - m)
                den = jnp.sum(p, axis=-1, keepdims=True)
                o = jnp.dot(p.astype(BF16), vv, preferred_element_type=F32)
                o_view, idx = rows_of(so, g * span, r)
                o_view[idx, :] = o / den
                l_view, idx = rows_of(sl, g * span, r)
                l_view[idx, :] = jnp.broadcast_to(m + jnp.log(den), (blk, LANES))

        if staged:
            for rr in range(inner):
                dst_rows = pl.ds(rr, rows // inner, stride=inner)
                o_nat[bi][dst_rows, :] = so[rr]
                l_nat[bi][dst_rows, :] = sl[rr]

    ch = _tile(rows, MIX_ROW_CHUNK)
    for c in range(rows // ch):
        rs = slice(c * ch, (c + 1) * ch)
        lses = [l[rs, :] for l in l_nat]
        top = functools.reduce(jnp.maximum, lses)
        es = [jnp.exp(l - top) for l in lses]
        num = functools.reduce(lambda x, y: x + y, [e * o[rs, :] for e, o in zip(es, o_nat)])
        den = functools.reduce(lambda x, y: x + y, es)
        a_ref[rs, :] = (num / den).astype(a_ref.dtype)


def _dilated_mix(hproj, a_width, dilations):
    s = hproj.shape[0]
    heads = a_width // HEAD_DIM
    rows = _tile(s, DIL_ROWS)
    assert all(rows % (DIL_BLOCK * d) == 0 for d in dilations)
    cur = lambda col0: pl.BlockSpec((rows, HEAD_DIM), lambda n, h: (n, col0 + h))
    prev = lambda col0: pl.BlockSpec((rows, HEAD_DIM),
                                     lambda n, h: (jnp.maximum(n - 1, 0), col0 + h))
    natural = pltpu.VMEM((rows, HEAD_DIM), F32)
    staged = pltpu.VMEM((DIL_STAGE_STRIDE, rows // DIL_STAGE_STRIDE, HEAD_DIM), F32)
    return pl.pallas_call(
        functools.partial(_dilated_mix_kernel, dilations=tuple(dilations)),
        out_shape=jax.ShapeDtypeStruct((s, a_width), BF16),
        grid=(s // rows, heads),
        in_specs=[cur(0), prev(heads), cur(heads), prev(2 * heads), cur(2 * heads)],
        out_specs=pl.BlockSpec((rows, HEAD_DIM), lambda n, h: (n, h)),
        scratch_shapes=[natural] * (2 * len(dilations)) + [staged] * 7,
        compiler_params=_params("parallel", "arbitrary"),
    )(hproj, hproj, hproj, hproj, hproj)


GATE_CHANNELS = 256


def _gate_conv_kernel(a_ref, wb_ref, wc_ref, wh_ref, cw_ref, out_ref, tail_ref):
    i = pl.program_id(0)
    j = pl.program_id(1)
    a = a_ref[...]
    gb = jnp.dot(a, wb_ref[...], preferred_element_type=F32)
    gc = jnp.dot(a, wc_ref[...], preferred_element_type=F32)
    hin = jnp.dot(a, wh_ref[...], preferred_element_type=F32)
    u = gc * hin
    tm = u.shape[0]

    @pl.when(i == 0)
    def _():
        tail_ref[j] = jnp.zeros(tail_ref.shape[1:], F32)

    halo = tail_ref[j]
    tail_ref[j] = u[tm - SUBLANES:, :]
    row = lax.broadcasted_iota(jnp.int32, u.shape, 0)
    hm1 = halo[SUBLANES - 1:SUBLANES, :]
    hm2 = halo[SUBLANES - 2:SUBLANES - 1, :]
    u1 = jnp.where(row == 0, hm1, pltpu.roll(u, 1, axis=0))
    u2 = jnp.where(row == 0, hm2, jnp.where(row == 1, hm1, pltpu.roll(u, 2, axis=0)))
    cw = cw_ref[...]
    y = cw[0:1, :] * u2 + cw[1:2, :] * u1 + cw[2:3, :] * u
    out_ref[...] = (gb * y).astype(out_ref.dtype)


def _gate_conv(a, w, conv_w, col0, b_width, tm_pref=1024):
    m, k = a.shape
    tm = _tile(m, tm_pref)
    cb = _tile(b_width, GATE_CHANNELS)
    assert col0 % cb == 0 and tm >= SUBLANES
    nct = b_width // cb
    wblk = lambda g: pl.BlockSpec((k, cb), lambda i, j: (0, (col0 + g * b_width) // cb + j))
    return pl.pallas_call(
        _gate_conv_kernel,
        out_shape=jax.ShapeDtypeStruct((m, b_width), BF16),
        grid=(m // tm, nct),
        in_specs=[pl.BlockSpec((tm, k), lambda i, j: (i, 0)), wblk(0), wblk(1), wblk(2),
                  pl.BlockSpec((CONV_WIDTH, cb), lambda i, j: (0, j))],
        out_specs=pl.BlockSpec((tm, cb), lambda i, j: (i, j)),
        scratch_shapes=[pltpu.VMEM((nct, SUBLANES, cb), F32)],
        compiler_params=_params("arbitrary", "arbitrary"),
    )(a, w, w, w, conv_w)


LN_COL_CHUNK = 1024
LN_ROW_CHUNK = 8


def _residual_ln(x_ref, g_ref, b_ref, o_ref, ob_ref, alpha):
    tm = o_ref.shape[0]
    rc = _tile(tm, LN_ROW_CHUNK)
    for r in range(tm // rc):
        rows = slice(r * rc, (r + 1) * rc)
        z = alpha * x_ref[rows, :] + o_ref[rows, :]
        mu = jnp.mean(z, axis=-1, keepdims=True)
        zc = z - mu
        var = jnp.mean(zc * zc, axis=-1, keepdims=True)
        y = zc * lax.rsqrt(var + LN_EPS) * g_ref[...] + b_ref[...]
        o_ref[rows, :] = y
        if ob_ref is not None:
            ob_ref[rows, :] = y.astype(BF16)


def _mm_res_ln_kernel(a_ref, w_ref, x_ref, g_ref, b_ref, o_ref, *maybe_ob, alpha):
    k = pl.program_id(1)
    n = o_ref.shape[1]

    @pl.when(k == 0)
    def _():
        o_ref[...] = jnp.zeros(o_ref.shape, F32)

    a = a_ref[...]
    nc = _tile(n, LN_COL_CHUNK)
    for c in range(n // nc):
        sl = slice(c * nc, (c + 1) * nc)
        o_ref[:, sl] += jnp.dot(a, w_ref[:, sl], preferred_element_type=F32)

    @pl.when(k == pl.num_programs(1) - 1)
    def _():
        _residual_ln(x_ref, g_ref, b_ref, o_ref, maybe_ob[0] if maybe_ob else None, alpha)


def _mm_res_ln(a, w, xres, g, b, alpha, *, want_bf16, tm_pref=512, tk_pref=512):
    m, k = a.shape
    n = w.shape[1]
    tm = _tile(m, tm_pref)
    tk = _tile(k, tk_pref)
    out_shape = [jax.ShapeDtypeStruct((m, n), F32)]
    out_specs = [pl.BlockSpec((tm, n), lambda i, kk: (i, 0))]
    if want_bf16:
        out_shape.append(jax.ShapeDtypeStruct((m, n), BF16))
        out_specs.append(pl.BlockSpec((tm, n), lambda i, kk: (i, 0)))
    res = pl.pallas_call(
        functools.partial(_mm_res_ln_kernel, alpha=alpha),
        out_shape=tuple(out_shape),
        grid=(m // tm, k // tk),
        in_specs=[
            pl.BlockSpec((tm, tk), lambda i, kk: (i, kk)),
            pl.BlockSpec((tk, n), lambda i, kk: (kk, 0)),
            pl.BlockSpec((tm, n), lambda i, kk: (i, 0)),
            pl.BlockSpec((1, n), lambda i, kk: (0, 0)),
            pl.BlockSpec((1, n), lambda i, kk: (0, 0)),
        ],
        out_specs=tuple(out_specs),
        compiler_params=_params("parallel", "arbitrary"),
    )(a, w, xres, g.reshape(1, n), b.reshape(1, n))
    return res if want_bf16 else (res[0], None)


def _proj_res_ln_kernel(*refs, n_parts, alpha):
    a_refs = refs[:n_parts]
    w_ref, x_ref, g_ref, b_ref, o_ref, ob_ref = refs[n_parts:]
    n = o_ref.shape[1]
    nc = _tile(n, LN_COL_CHUNK)
    for c in range(n // nc):
        sl = slice(c * nc, (c + 1) * nc)
        k0 = 0
        acc = None
        for a_ref in a_refs:
            kw = a_ref.shape[1]
            part = jnp.dot(a_ref[...], w_ref[k0:k0 + kw, sl], preferred_element_type=F32)
            acc = part if acc is None else acc + part
            k0 += kw
        o_ref[:, sl] = acc
    _residual_ln(x_ref, g_ref, b_ref, o_ref, ob_ref, alpha)


def _proj_res_ln(parts, w, xres, g, b, alpha, tm_pref=128):
    m = parts[0].shape[0]
    k, n = w.shape
    assert sum(p.shape[1] for p in parts) == k
    tm = _tile(m, tm_pref)
    row = lambda width: pl.BlockSpec((tm, width), lambda i: (i, 0))
    vec = pl.BlockSpec((1, n), lambda i: (0, 0))
    return pl.pallas_call(
        functools.partial(_proj_res_ln_kernel, n_parts=len(parts), alpha=alpha),
        out_shape=(jax.ShapeDtypeStruct((m, n), F32), jax.ShapeDtypeStruct((m, n), BF16)),
        grid=(m // tm,),
        in_specs=[row(p.shape[1]) for p in parts]
        + [pl.BlockSpec((k, n), lambda i: (0, 0), pipeline_mode=pl.Buffered(1)),
           row(n), vec, vec],
        out_specs=(row(n), row(n)),
        compiler_params=_params("arbitrary"),
    )(*parts, w, xres, g.reshape(1, n), b.reshape(1, n))


def _ffn_up_kernel(a_ref, wg_ref, wu_ref, o_ref):
    a = a_ref[...]
    g = jnp.dot(a, wg_ref[...], preferred_element_type=F32)
    u = jnp.dot(a, wu_ref[...], preferred_element_type=F32)
    o_ref[...] = (g * jax.nn.sigmoid(g) * u).astype(o_ref.dtype)


def _ffn_up(a, wg, wu, tm_pref=1024, tn_pref=512):
    m, k = a.shape
    n = wg.shape[1]
    tm = _tile(m, tm_pref)
    tn = _tile(n, tn_pref)
    return pl.pallas_call(
        _ffn_up_kernel,
        out_shape=jax.ShapeDtypeStruct((m, n), BF16),
        grid=(m // tm, n // tn),
        in_specs=[pl.BlockSpec((tm, k), lambda i, j: (i, 0)),
                  pl.BlockSpec((k, tn), lambda i, j: (0, j)),
                  pl.BlockSpec((k, tn), lambda i, j: (0, j))],
        out_specs=pl.BlockSpec((tm, tn), lambda i, j: (i, j)),
        compiler_params=_params("parallel", "arbitrary"),
    )(a, wg, wu)


def _mla_in_kernel(a_ref, w_ref, qn_ref, kvn_ref, cos_ref, sin_ref, cq_ref, ckv_ref, kr_ref):
    def rms(t, gain):
        r = lax.rsqrt(jnp.mean(t * t, axis=-1, keepdims=True) + RMS_EPS)
        return t * r * gain

    a = a_ref[...]
    kv0 = Q_LORA + KV_LORA
    cq = jnp.dot(a, w_ref[:, :Q_LORA], preferred_element_type=F32)
    cq_ref[...] = rms(cq, qn_ref[...]).astype(BF16)
    ckv = jnp.dot(a, w_ref[:, Q_LORA:kv0], preferred_element_type=F32)
    ckv_ref[...] = rms(ckv, kvn_ref[...]).astype(BF16)
    y = jnp.dot(a, w_ref[:, kv0:], preferred_element_type=F32)
    kr_ref[...] = _rope_half_lanes(y, cos_ref[...], sin_ref[...]).astype(BF16)


def _mla_in(a, w, q_norm, kv_norm, tables, tm_pref=512):
    m, k = a.shape
    n = w.shape[1]
    tm = _tile(m, tm_pref)
    row = lambda width: pl.BlockSpec((tm, width), lambda i: (i, 0))
    return pl.pallas_call(
        _mla_in_kernel,
        out_shape=(jax.ShapeDtypeStruct((m, Q_LORA), BF16),
                   jax.ShapeDtypeStruct((m, KV_LORA), BF16),
                   jax.ShapeDtypeStruct((m, LANES), BF16)),
        grid=(m // tm,),
        in_specs=[row(k),
                  pl.BlockSpec((k, n), lambda i: (0, 0), pipeline_mode=pl.Buffered(1)),
                  pl.BlockSpec((1, Q_LORA), lambda i: (0, 0)),
                  pl.BlockSpec((1, KV_LORA), lambda i: (0, 0)),
                  row(LANES), row(LANES)],
        out_specs=(row(Q_LORA), row(KV_LORA), row(LANES)),
        compiler_params=_params("arbitrary"),
    )(a, w, q_norm.reshape(1, Q_LORA), kv_norm.reshape(1, KV_LORA), *tables)


QK_DIM = QK_NOPE + QK_ROPE
UP_HEADS_PER_DOT = 4


ATTN_BLOCK = 512
ONES_ROWS = 16


def _rope_tables_t(seq, dim):
    half = dim // 2
    inv = ROPE_THETA ** (-jnp.arange(half, dtype=F32) * 2.0 / dim)
    ang = inv[:, None] * jnp.arange(seq).astype(F32)[None, :]
    return jnp.cos(ang), jnp.sin(ang)


def _q_up_t_kernel(a_ref, wn_ref, wr_ref, cos_ref, sin_ref, q_ref, *, heads_per_step, scale):
    a = a_ref[...]
    cos, sin = cos_ref[...], sin_ref[...]
    hpd = UP_HEADS_PER_DOT
    half = QK_ROPE // 2
    tn = (((0,), (1,)), ((), ()))
    for c in range(heads_per_step // hpd):
        nope = lax.dot_general(wn_ref[:, c * hpd * QK_NOPE:(c + 1) * hpd * QK_NOPE], a, tn,
                               preferred_element_type=F32)
        rope = lax.dot_general(wr_ref[:, c * hpd * QK_ROPE:(c + 1) * hpd * QK_ROPE], a, tn,
                               preferred_element_type=F32)
        for h in range(hpd):
            y1 = rope[h * QK_ROPE:h * QK_ROPE + half, :]
            y2 = rope[h * QK_ROPE + half:(h + 1) * QK_ROPE, :]
            q_ref[c * hpd + h, :QK_NOPE, :] = (
                nope[h * QK_NOPE:(h + 1) * QK_NOPE, :] * scale).astype(BF16)
            q_ref[c * hpd + h, QK_NOPE:QK_NOPE + half, :] = (
                (y1 * cos - y2 * sin) * scale).astype(BF16)
            q_ref[c * hpd + h, QK_NOPE + half:, :] = ((y2 * cos + y1 * sin) * scale).astype(BF16)


def _q_up_t(cq, w_uq_g, tables_t, heads, scale, tm_pref=1024, hps_pref=8):
    m, k = cq.shape
    tm = _tile(m, tm_pref)
    hps = _tile(heads, hps_pref)
    assert hps % UP_HEADS_PER_DOT == 0
    half = QK_ROPE // 2
    rope_blk0 = heads * QK_NOPE // (hps * QK_ROPE)
    return pl.pallas_call(
        functools.partial(_q_up_t_kernel, heads_per_step=hps, scale=scale),
        out_shape=jax.ShapeDtypeStruct((heads, QK_DIM, m), BF16),
        grid=(m // tm, heads // hps),
        in_specs=[pl.BlockSpec((tm, k), lambda i, j: (i, 0)),
                  pl.BlockSpec((k, hps * QK_NOPE), lambda i, j: (0, j)),
                  pl.BlockSpec((k, hps * QK_ROPE), lambda i, j: (0, rope_blk0 + j)),
                  pl.BlockSpec((half, tm), lambda i, j: (0, i)),
                  pl.BlockSpec((half, tm), lambda i, j: (0, i))],
        out_specs=pl.BlockSpec((hps, QK_DIM, tm), lambda i, j: (j, 0, i)),
        compiler_params=_params("parallel", "arbitrary"),
    )(cq, w_uq_g, w_uq_g, *tables_t)


def _kv_up_t_kernel(a_ref, wk_ref, wv_ref, kr_ref, k_ref, v_ref, *, heads_per_step, blk):
    a = a_ref[...]
    kr = kr_ref[:, :QK_ROPE]
    hpd = UP_HEADS_PER_DOT
    tn = (((0,), (1,)), ((), ()))
    for c in range(heads_per_step // hpd):
        kn = jnp.dot(a, wk_ref[:, c * hpd * QK_NOPE:(c + 1) * hpd * QK_NOPE],
                     preferred_element_type=F32)
        vt = lax.dot_general(wv_ref[:, c * hpd * V_DIM:(c + 1) * hpd * V_DIM], a, tn,
                             preferred_element_type=F32)
        for h in range(hpd):
            k_ref[c * hpd + h, :, :QK_NOPE] = kn[:, h * QK_NOPE:(h + 1) * QK_NOPE].astype(BF16)
            k_ref[c * hpd + h, :, QK_NOPE:] = kr
            for b in range(a.shape[0] // blk):
                v_ref[c * hpd + h, b] = vt[h * V_DIM:(h + 1) * V_DIM,
                                           b * blk:(b + 1) * blk].astype(BF16)


def _kv_up_t(ckv, w_ukv_g, kr, heads, blk, tm_pref=1024, hps_pref=16):
    m, k = ckv.shape
    tm = _tile(m, tm_pref)
    hps = _tile(heads, hps_pref)
    assert hps % UP_HEADS_PER_DOT == 0 and tm % blk == 0
    v_blk0 = heads * QK_NOPE // (hps * V_DIM)
    return pl.pallas_call(
        functools.partial(_kv_up_t_kernel, heads_per_step=hps, blk=blk),
        out_shape=(jax.ShapeDtypeStruct((heads, m, QK_DIM), BF16),
                   jax.ShapeDtypeStruct((heads, m // blk, V_DIM, blk), BF16)),
        grid=(m // tm, heads // hps),
        in_specs=[pl.BlockSpec((tm, k), lambda i, j: (i, 0)),
                  pl.BlockSpec((k, hps * QK_NOPE), lambda i, j: (0, j)),
                  pl.BlockSpec((k, hps * V_DIM), lambda i, j: (0, v_blk0 + j)),
                  pl.BlockSpec((tm, LANES), lambda i, j: (i, 0))],
        out_specs=(pl.BlockSpec((hps, tm, QK_DIM), lambda i, j: (j, i, 0)),
                   pl.BlockSpec((hps, tm // blk, V_DIM, blk), lambda i, j: (j, i, 0, 0))),
        compiler_params=_params("parallel", "arbitrary"),
    )(ckv, w_ukv_g, w_ukv_g, kr)


def _mla_attn_t_kernel(q_ref, k_ref, v_ref, o_ref, m_ref, acc_ref, s0_ref, s1_ref, *, blk, nsub):
    qi = pl.program_id(1)
    m_ref[...] = jnp.full(m_ref.shape, NEG, F32)
    acc_ref[...] = jnp.zeros(acc_ref.shape, F32)
    ones = jnp.ones((ONES_ROWS, blk), BF16)
    slots = (s0_ref, s1_ref)

    def scores(sub, j, slot):
        start = pl.multiple_of(j * blk, blk)
        kk = k_ref[0, pl.ds(start, blk), :]
        qt = q_ref[0, :, sub * blk:(sub + 1) * blk]
        slot[sub] = jnp.dot(kk, qt, preferred_element_type=F32)

    def softmax_pv(sub, j, slot, masked):
        vt = jnp.concatenate([v_ref[0, j], ones], axis=0)
        s = slot[sub]
        if masked:
            key = lax.broadcasted_iota(jnp.int32, s.shape, 0)
            qry = lax.broadcasted_iota(jnp.int32, s.shape, 1)
            s = jnp.where(key <= qry, s, NEG)
        m_prev = m_ref[sub]
        m_new = jnp.maximum(m_prev, jnp.max(s, axis=0, keepdims=True))
        corr = jnp.exp(m_prev - m_new)
        p = jnp.exp(s - m_new).astype(BF16)
        acc_ref[sub] = corr * acc_ref[sub] + jnp.dot(vt, p, preferred_element_type=F32)
        m_ref[sub] = m_new

    for sub in range(nsub):
        scores(sub, 0, slots[0])

    def full_blocks(t):
        for c in range(nsub):
            j = t * nsub + c
            for sub in range(nsub):
                scores(sub, j + 1, slots[(c + 1) % 2])
                softmax_pv(sub, j, slots[c % 2], masked=False)

    def body(t, carry):
        full_blocks(t)
        return carry

    lax.fori_loop(0, qi, body, 0)
    base = qi * nsub
    for c in range(nsub):
        for sub in range(c, nsub):
            if sub > c:
                scores(sub, base + c + 1, slots[(c + 1) % 2])
            softmax_pv(sub, base + c, slots[c % 2], masked=(sub == c))
    for sub in range(nsub):
        acc = acc_ref[sub]
        o_t = acc[:V_DIM, :] / acc[V_DIM:V_DIM + 1, :]
        o_ref[sub * blk:(sub + 1) * blk, :] = o_t.T.astype(o_ref.dtype)


def _mla_attn_t(q_t, k, v_t, nsub_pref=4):
    heads, s, _ = k.shape
    blk = v_t.shape[3]
    nsub = _tile(s // blk, nsub_pref)
    assert nsub % 2 == 0, "score slots alternate with key-block parity"
    tq = blk * nsub
    return pl.pallas_call(
        functools.partial(_mla_attn_t_kernel, blk=blk, nsub=nsub),
        out_shape=jax.ShapeDtypeStruct((s, heads * V_DIM), BF16),
        grid=(heads, s // tq),
        in_specs=[pl.BlockSpec((1, QK_DIM, tq), lambda h, i: (h, 0, i)),
                  pl.BlockSpec((1, s, QK_DIM), lambda h, i: (h, 0, 0)),
                  pl.BlockSpec((1, s // blk, V_DIM, blk), lambda h, i: (h, 0, 0, 0))],
        out_specs=pl.BlockSpec((tq, V_DIM), lambda h, i: (i, h)),
        scratch_shapes=[pltpu.VMEM((nsub, 1, blk), F32),
                        pltpu.VMEM((nsub, V_DIM + ONES_ROWS, blk), F32),
                        pltpu.VMEM((nsub, blk, blk), F32),
                        pltpu.VMEM((nsub, blk, blk), F32)],
        compiler_params=_params("parallel", "arbitrary"),
    )(q_t, k, v_t)


def _ffn_block(x, xb, w_gate, w_up, w_down, l, g, b, alpha, want_bf16):
    hidden = w_gate.shape[2]
    hp = _round_up(hidden, 512)
    wg = _cast_pad(w_gate, l, cols_p=hp)
    wu = _cast_pad(w_up, l, cols_p=hp)
    wd = _cast_pad(w_down, l, rows_p=hp)
    hmid = _ffn_up(xb, wg, wu)
    return _mm_res_ln(hmid, wd, x, g, b, alpha, want_bf16=want_bf16)


def _attn_conv_layer(x, xb, w_in, conv_w, w_out, j, g, b, alpha):
    s, d = x.shape
    a_width = d // 2
    w_in_b = _cast_pad(w_in, j)
    qkv = _in_proj(xb, w_in_b, _rope_tables(s, HEAD_DIM), ncols=3 * a_width,
                   rope_cols=2 * a_width, scale=HEAD_DIM ** -0.5, scaled_cols=a_width)
    assert all(window // d == DIL_BLOCK and s % window == 0 for window, d in DILATED_PATTERNS)
    attn = _dilated_mix(qkv, a_width, [d for _, d in DILATED_PATTERNS])
    gated = _gate_conv(xb, w_in_b, conv_w[j], 3 * a_width, d - a_width)
    return _proj_res_ln([attn, gated], _cast_pad(w_out, j), x, g, b, alpha)


def _mla_layer(x, xb, w_in, q_norm, kv_norm, w_uq, w_ukv, w_out, j, g, b, alpha):
    s, d = x.shape
    heads = d // 128
    w_in_p = _cast_pad(w_in, j, cols_p=w_in.shape[2] + LANES - QK_ROPE)
    cq, ckv, kr = _mla_in(xb, w_in_p, q_norm[j], kv_norm[j], _rope_tables(s, QK_ROPE))
    w_uq_h = w_uq[j].astype(BF16).reshape(Q_LORA, heads, QK_DIM)
    w_uq_g = jnp.concatenate([w_uq_h[:, :, :QK_NOPE].reshape(Q_LORA, heads * QK_NOPE),
                              w_uq_h[:, :, QK_NOPE:].reshape(Q_LORA, heads * QK_ROPE)], axis=1)
    q_t = _q_up_t(cq, w_uq_g, _rope_tables_t(s, QK_ROPE), heads, QK_DIM ** -0.5)
    w_ukv_h = w_ukv[j].astype(BF16).reshape(KV_LORA, heads, QK_NOPE + V_DIM)
    w_ukv_g = jnp.concatenate([w_ukv_h[:, :, :QK_NOPE].reshape(KV_LORA, heads * QK_NOPE),
                               w_ukv_h[:, :, QK_NOPE:].reshape(KV_LORA, heads * V_DIM)], axis=1)
    k, v_t = _kv_up_t(ckv, w_ukv_g, kr, heads, _tile(s, ATTN_BLOCK))
    o = _mla_attn_t(q_t, k, v_t)
    return _proj_res_ln([o], _cast_pad(w_out, j), x, g, b, alpha)


def kernel(x, w_in_a, conv_w, w_out_a, w_in_c, q_norm, kv_norm, w_uq, w_ukv, w_out_c,
           ln1_g, ln1_b, w_gate, w_up, w_down, ln2_g, ln2_b):
    batch, s, d = x.shape
    depth = ln1_g.shape[0]
    alpha = (2.0 * depth) ** 0.25
    outs = []
    for bi in range(batch):
        xf = x[bi]
        xb = xf.astype(BF16)
        for l in range(depth):
            j = l // 2
            if l % 2 == 0:
                xf, xb = _attn_conv_layer(xf, xb, w_in_a, conv_w, w_out_a, j,
                                          ln1_g[l], ln1_b[l], alpha)
            else:
                xf, xb = _mla_layer(xf, xb, w_in_c, q_norm, kv_norm, w_uq, w_ukv, w_out_c, j,
                                    ln1_g[l], ln1_b[l], alpha)
            xf, xb = _ffn_block(xf, xb, w_gate, w_up, w_down, l, ln2_g[l], ln2_b[l],
                                alpha, want_bf16=l + 1 < depth)
        outs.append(xf)
    return jnp.stack(outs)
```

```python
import functools
import math

import jax
import jax.numpy as jnp
from jax import lax
from jax.experimental import pallas as pl
from jax.experimental.pallas import tpu as pltpu

F32 = jnp.float32
BF16 = jnp.bfloat16

HEAD_DIM = 128
CONV_WIDTH = 3
DILATED_PATTERNS = ((128, 1), (512, 4), (2048, 16))
DIL_BLOCK = 128
Q_LORA = 1536
KV_LORA = 512
QK_NOPE = 128
QK_ROPE = 64
V_DIM = 128
ROPE_THETA = 10000.0
LN_EPS = 1e-5
RMS_EPS = 1e-6
NEG = -1e30

LANES = 128
SUBLANES = 8
VMEM_LIMIT_BYTES = 60 * 1024 * 1024


def _round_up(n, m):
    return -(-n // m) * m


def _tile(n, pref):
    if n <= pref:
        return n
    t = pref
    while n % t:
        t //= 2
    return t


def _params(*sem):
    return pltpu.CompilerParams(dimension_semantics=sem, vmem_limit_bytes=VMEM_LIMIT_BYTES)


def _cast_pad_kernel(w_ref, o_ref, *, rows, cols, rows_p):
    tr, cols_p = o_ref.shape
    val = w_ref[...].astype(BF16)
    if rows_p > rows:
        row = pl.program_id(0) * tr + lax.broadcasted_iota(jnp.int32, val.shape, 0)
        val = jnp.where(row < rows, val, jnp.zeros_like(val))
    o_ref[:, :cols] = val
    if cols_p > cols:
        o_ref[:, cols:] = jnp.zeros((tr, cols_p - cols), BF16)


def _cast_pad(w_stack, layer, rows_p=None, cols_p=None, tr_pref=256):
    _, rows, cols = w_stack.shape
    rows_p = rows_p or rows
    cols_p = cols_p or cols
    tr = _tile(rows_p, tr_pref)
    last = (rows - 1) // tr
    return pl.pallas_call(
        functools.partial(_cast_pad_kernel, rows=rows, cols=cols, rows_p=rows_p),
        out_shape=jax.ShapeDtypeStruct((rows_p, cols_p), BF16),
        grid=(rows_p // tr,),
        in_specs=[pl.BlockSpec((None, tr, cols), lambda i: (layer, jnp.minimum(i, last), 0))],
        out_specs=pl.BlockSpec((tr, cols_p), lambda i: (i, 0)),
        compiler_params=_params("parallel"),
    )(w_stack)


def _rope_tables(seq, dim):
    half = dim // 2
    inv = ROPE_THETA ** (-jnp.arange(half, dtype=F32) * 2.0 / dim)
    ang = jnp.arange(seq).astype(F32)[:, None] * inv[None, :]
    cos, sin = jnp.cos(ang), jnp.sin(ang)
    reps = LANES // dim
    return (jnp.concatenate([cos, cos] * reps, axis=1),
            jnp.concatenate([-sin, sin] * reps, axis=1))


def _rope_full_lanes(t, cos, sin):
    return t * cos + pltpu.roll(t, HEAD_DIM // 2, axis=1) * sin


def _rope_half_lanes(y, cos, sin):
    half = QK_ROPE // 2
    lane = lax.broadcasted_iota(jnp.int32, y.shape, 1)
    partner = jnp.where(lane % QK_ROPE < half, pltpu.roll(y, LANES - half, axis=1),
                        pltpu.roll(y, half, axis=1))
    return y * cos + partner * sin


def _in_proj_kernel(a_ref, w_ref, cos_ref, sin_ref, o_ref, *, scale, n_scaled_tiles,
                    n_rope_tiles):
    j = pl.program_id(1)
    acc = jnp.dot(a_ref[...], w_ref[...], preferred_element_type=F32)

    @pl.when(j >= n_rope_tiles)
    def _():
        o_ref[...] = acc

    @pl.when(j < n_rope_tiles)
    def _():
        cos = cos_ref[...]
        sin = sin_ref[...]
        mult = jnp.where(j < n_scaled_tiles, scale, 1.0).astype(F32)
        for c in range(acc.shape[1] // HEAD_DIM):
            sl = slice(c * HEAD_DIM, (c + 1) * HEAD_DIM)
            o_ref[:, sl] = _rope_full_lanes(acc[:, sl], cos, sin) * mult


def _in_proj(a, w, rope_tables, *, ncols, rope_cols, scale, scaled_cols, tm_pref=1024,
             tn_pref=1024):
    m, k = a.shape
    n = ncols
    tm = _tile(m, tm_pref)
    tn = _tile(math.gcd(n, rope_cols, scaled_cols), tn_pref)
    return pl.pallas_call(
        functools.partial(_in_proj_kernel, scale=scale, n_scaled_tiles=scaled_cols // tn,
                          n_rope_tiles=rope_cols // tn),
        out_shape=jax.ShapeDtypeStruct((m, n), F32),
        grid=(m // tm, n // tn),
        in_specs=[pl.BlockSpec((tm, k), lambda i, j: (i, 0)),
                  pl.BlockSpec((k, tn), lambda i, j: (0, j)),
                  pl.BlockSpec((tm, LANES), lambda i, j: (i, 0)),
                  pl.BlockSpec((tm, LANES), lambda i, j: (i, 0))],
        out_specs=pl.BlockSpec((tm, tn), lambda i, j: (i, j)),
        compiler_params=_params("parallel", "arbitrary"),
    )(a, w, *rope_tables)


DIL_ROWS = 2048
DIL_STAGE_STRIDE = 4


MIX_ROW_CHUNK = 256


def _dilated_mix_kernel(q_ref, kp_ref, kc_ref, vp_ref, vc_ref, a_ref, *scratch, dilations):
    n = pl.program_id(0)
    rows = q_ref.shape[0]
    blk = DIL_BLOCK
    nb = len(dilations)
    o_nat, l_nat, stage_bufs = scratch[:nb], scratch[nb:2 * nb], scratch[2 * nb:]
    qi = lax.broadcasted_iota(jnp.int32, (blk, 2 * blk), 0)
    ki = lax.broadcasted_iota(jnp.int32, (blk, 2 * blk), 1)
    dist = qi + blk - ki
    band = (dist >= 0) & (dist <= blk)
    band_first = band & ((ki >= blk) | (n > 0))
    inner = DIL_STAGE_STRIDE

    for bi, d in enumerate(dilations):
        span = blk * d
        groups = rows // span
        staged = d % (4 * inner) == 0
        if staged:
            sq, skp, skc, svp, svc, so, sl = stage_bufs
            for src, dst in ((q_ref, sq), (kp_ref, skp), (kc_ref, skc), (vp_ref, svp),
                             (vc_ref, svc)):
                for rr in range(inner):
                    dst[rr] = src[pl.ds(rr, rows // inner, stride=inner), :]
        else:
            sq, skp, skc, svp, svc = q_ref, kp_ref, kc_ref, vp_ref, vc_ref
            so, sl = o_nat[bi], l_nat[bi]

        def rows_of(ref, start, r, d=d, staged=staged):
            if not staged:
                return ref, (pl.ds(start + r, blk, stride=d) if d > 1 else pl.ds(start + r, blk))
            return ref.at[r % inner], pl.ds(start // inner + r // inner, blk, stride=d // inner)

        def read(ref, start, r, rows_of=rows_of):
            view, idx = rows_of(ref, start, r)
            return view[idx, :]

        for g in range(groups):
            for r in range(d):
                q = read(sq, g * span, r).astype(BF16)
                if g == 0:
                    tail = (groups - 1) * span
                    k_prev, v_prev = read(skp, tail, r), read(svp, tail, r)
                else:
                    k_prev, v_prev = read(skc, (g - 1) * span, r), read(svc, (g - 1) * span, r)
                kk = jnp.concatenate([k_prev, read(skc, g * span, r)], axis=0).astype(BF16)
                vv = jnp.concatenate([v_prev, read(svc, g * span, r)], axis=0).astype(BF16)
                s = lax.dot_general(q, kk, (((1,), (1,)), ((), ())),
                                    preferred_element_type=F32)
                s = jnp.where(band_first if g == 0 else band, s, NEG)
                m = jnp.max(s, axis=-1, keepdims=True)
                p = jnp.exp(s - m)
                den = jnp.sum(p, axis=-1, keepdims=True)
                o = jnp.dot(p.astype(BF16), vv, preferred_element_type=F32)
                o_view, idx = rows_of(so, g * span, r)
                o_view[idx, :] = o / den
                l_view, idx = rows_of(sl, g * span, r)
                l_view[idx, :] = jnp.broadcast_to(m + jnp.log(den), (blk, LANES))

        if staged:
            for rr in range(inner):
                dst_rows = pl.ds(rr, rows // inner, stride=inner)
                o_nat[bi][dst_rows, :] = so[rr]
                l_nat[bi][dst_rows, :] = sl[rr]

    ch = _tile(rows, MIX_ROW_CHUNK)
    for c in range(rows // ch):
        rs = slice(c * ch, (c + 1) * ch)
        lses = [l[rs, :] for l in l_nat]
        top = functools.reduce(jnp.maximum, lses)
        es = [jnp.exp(l - top) for l in lses]
        num = functools.reduce(lambda x, y: x + y, [e * o[rs, :] for e, o in zip(es, o_nat)])
        den = functools.reduce(lambda x, y: x + y, es)
        a_ref[rs, :] = (num / den).astype(a_ref.dtype)


def _dilated_mix(hproj, a_width, dilations):
    s = hproj.shape[0]
    heads = a_width // HEAD_DIM
    rows = _tile(s, DIL_ROWS)
    assert all(rows % (DIL_BLOCK * d) == 0 for d in dilations)
    cur = lambda col0: pl.BlockSpec((rows, HEAD_DIM), lambda n, h: (n, col0 + h))
    prev = lambda col0: pl.BlockSpec((rows, HEAD_DIM),
                                     lambda n, h: (jnp.maximum(n - 1, 0), col0 + h))
    natural = pltpu.VMEM((rows, HEAD_DIM), F32)
    staged = pltpu.VMEM((DIL_STAGE_STRIDE, rows // DIL_STAGE_STRIDE, HEAD_DIM), F32)
    return pl.pallas_call(
        functools.partial(_dilated_mix_kernel, dilations=tuple(dilations)),
        out_shape=jax.ShapeDtypeStruct((s, a_width), BF16),
        grid=(s // rows, heads),
        in_specs=[cur(0), prev(heads), cur(heads), prev(2 * heads), cur(2 * heads)],
        out_specs=pl.BlockSpec((rows, HEAD_DIM), lambda n, h: (n, h)),
        scratch_shapes=[natural] * (2 * len(dilations)) + [staged] * 7,
        compiler_params=_params("parallel", "arbitrary"),
    )(hproj, hproj, hproj, hproj, hproj)


GATE_CHANNELS = 256


def _gate_conv_kernel(a_ref, wb_ref, wc_ref, wh_ref, cw_ref, out_ref, tail_ref):
    i = pl.program_id(0)
    j = pl.program_id(1)
    a = a_ref[...]
    gb = jnp.dot(a, wb_ref[...], preferred_element_type=F32)
    gc = jnp.dot(a, wc_ref[...], preferred_element_type=F32)
    hin = jnp.dot(a, wh_ref[...], preferred_element_type=F32)
    u = gc * hin
    tm = u.shape[0]

    @pl.when(i == 0)
    def _():
        tail_ref[j] = jnp.zeros(tail_ref.shape[1:], F32)

    halo = tail_ref[j]
    tail_ref[j] = u[tm - SUBLANES:, :]
    row = lax.broadcasted_iota(jnp.int32, u.shape, 0)
    hm1 = halo[SUBLANES - 1:SUBLANES, :]
    hm2 = halo[SUBLANES - 2:SUBLANES - 1, :]
    u1 = jnp.where(row == 0, hm1, pltpu.roll(u, 1, axis=0))
    u2 = jnp.where(row == 0, hm2, jnp.where(row == 1, hm1, pltpu.roll(u, 2, axis=0)))
    cw = cw_ref[...]
    y = cw[0:1, :] * u2 + cw[1:2, :] * u1 + cw[2:3, :] * u
    out_ref[...] = (gb * y).astype(out_ref.dtype)


def _gate_conv(a, w, conv_w, col0, b_width, tm_pref=1024):
    m, k = a.shape
    tm = _tile(m, tm_pref)
    cb = _tile(b_width, GATE_CHANNELS)
    assert col0 % cb == 0 and tm >= SUBLANES
    nct = b_width // cb
    wblk = lambda g: pl.BlockSpec((k, cb), lambda i, j: (0, (col0 + g * b_width) // cb + j))
    return pl.pallas_call(
        _gate_conv_kernel,
        out_shape=jax.ShapeDtypeStruct((m, b_width), BF16),
        grid=(m // tm, nct),
        in_specs=[pl.BlockSpec((tm, k), lambda i, j: (i, 0)), wblk(0), wblk(1), wblk(2),
                  pl.BlockSpec((CONV_WIDTH, cb), lambda i, j: (0, j))],
        out_specs=pl.BlockSpec((tm, cb), lambda i, j: (i, j)),
        scratch_shapes=[pltpu.VMEM((nct, SUBLANES, cb), F32)],
        compiler_params=_params("arbitrary", "arbitrary"),
    )(a, w, w, w, conv_w)


LN_COL_CHUNK = 1024
LN_ROW_CHUNK = 8


def _residual_ln(x_ref, g_ref, b_ref, o_ref, ob_ref, alpha):
    tm = o_ref.shape[0]
    rc = _tile(tm, LN_ROW_CHUNK)
    for r in range(tm // rc):
        rows = slice(r * rc, (r + 1) * rc)
        z = alpha * x_ref[rows, :] + o_ref[rows, :]
        mu = jnp.mean(z, axis=-1, keepdims=True)
        zc = z - mu
        var = jnp.mean(zc * zc, axis=-1, keepdims=True)
        y = zc * lax.rsqrt(var + LN_EPS) * g_ref[...] + b_ref[...]
        o_ref[rows, :] = y
        if ob_ref is not None:
            ob_ref[rows, :] = y.astype(BF16)


def _mm_res_ln_kernel(a_ref, w_ref, x_ref, g_ref, b_ref, o_ref, *maybe_ob, alpha):
    k = pl.program_id(1)
    n = o_ref.shape[1]

    @pl.when(k == 0)
    def _():
        o_ref[...] = jnp.zeros(o_ref.shape, F32)

    a = a_ref[...]
    nc = _tile(n, LN_COL_CHUNK)
    for c in range(n // nc):
        sl = slice(c * nc, (c + 1) * nc)
        o_ref[:, sl] += jnp.dot(a, w_ref[:, sl], preferred_element_type=F32)

    @pl.when(k == pl.num_programs(1) - 1)
    def _():
        _residual_ln(x_ref, g_ref, b_ref, o_ref, maybe_ob[0] if maybe_ob else None, alpha)


def _mm_res_ln(a, w, xres, g, b, alpha, *, want_bf16, tm_pref=512, tk_pref=512):
    m, k = a.shape
    n = w.shape[1]
    tm = _tile(m, tm_pref)
    tk = _tile(k, tk_pref)
    out_shape = [jax.ShapeDtypeStruct((m, n), F32)]
    out_specs = [pl.BlockSpec((tm, n), lambda i, kk: (i, 0))]
    if want_bf16:
        out_shape.append(jax.ShapeDtypeStruct((m, n), BF16))
        out_specs.append(pl.BlockSpec((tm, n), lambda i, kk: (i, 0)))
    res = pl.pallas_call(
        functools.partial(_mm_res_ln_kernel, alpha=alpha),
        out_shape=tuple(out_shape),
        grid=(m // tm, k // tk),
        in_specs=[
            pl.BlockSpec((tm, tk), lambda i, kk: (i, kk)),
            pl.BlockSpec((tk, n), lambda i, kk: (kk, 0)),
            pl.BlockSpec((tm, n), lambda i, kk: (i, 0)),
            pl.BlockSpec((1, n), lambda i, kk: (0, 0)),
            pl.BlockSpec((1, n), lambda i, kk: (0, 0)),
        ],
        out_specs=tuple(out_specs),
        compiler_params=_params("parallel", "arbitrary"),
    )(a, w, xres, g.reshape(1, n), b.reshape(1, n))
    return res if want_bf16 else (res[0], None)


def _proj_res_ln_kernel(*refs, n_parts, alpha):
    a_refs = refs[:n_parts]
    w_ref, x_ref, g_ref, b_ref, o_ref, ob_ref = refs[n_parts:]
    n = o_ref.shape[1]
    nc = _tile(n, LN_COL_CHUNK)
    for c in range(n // nc):
        sl = slice(c * nc, (c + 1) * nc)
        k0 = 0
        acc = None
        for a_ref in a_refs:
            kw = a_ref.shape[1]
            part = jnp.dot(a_ref[...], w_ref[k0:k0 + kw, sl], preferred_element_type=F32)
            acc = part if acc is None else acc + part
            k0 += kw
        o_ref[:, sl] = acc
    _residual_ln(x_ref, g_ref, b_ref, o_ref, ob_ref, alpha)


def _proj_res_ln(parts, w, xres, g, b, alpha, tm_pref=128):
    m = parts[0].shape[0]
    k, n = w.shape
    assert sum(p.shape[1] for p in parts) == k
    tm = _tile(m, tm_pref)
    row = lambda width: pl.BlockSpec((tm, width), lambda i: (i, 0))
    vec = pl.BlockSpec((1, n), lambda i: (0, 0))
    return pl.pallas_call(
        functools.partial(_proj_res_ln_kernel, n_parts=len(parts), alpha=alpha),
        out_shape=(jax.ShapeDtypeStruct((m, n), F32), jax.ShapeDtypeStruct((m, n), BF16)),
        grid=(m // tm,),
        in_specs=[row(p.shape[1]) for p in parts]
        + [pl.BlockSpec((k, n), lambda i: (0, 0), pipeline_mode=pl.Buffered(1)),
           row(n), vec, vec],
        out_specs=(row(n), row(n)),
        compiler_params=_params("arbitrary"),
    )(*parts, w, xres, g.reshape(1, n), b.reshape(1, n))


def _ffn_up_kernel(a_ref, wg_ref, wu_ref, o_ref):
    a = a_ref[...]
    g = jnp.dot(a, wg_ref[...], preferred_element_type=F32)
    u = jnp.dot(a, wu_ref[...], preferred_element_type=F32)
    o_ref[...] = (g * jax.nn.sigmoid(g) * u).astype(o_ref.dtype)


def _ffn_up(a, wg, wu, tm_pref=1024, tn_pref=512):
    m, k = a.shape
    n = wg.shape[1]
    tm = _tile(m, tm_pref)
    tn = _tile(n, tn_pref)
    return pl.pallas_call(
        _ffn_up_kernel,
        out_shape=jax.ShapeDtypeStruct((m, n), BF16),
        grid=(m // tm, n // tn),
        in_specs=[pl.BlockSpec((tm, k), lambda i, j: (i, 0)),
                  pl.BlockSpec((k, tn), lambda i, j: (0, j)),
                  pl.BlockSpec((k, tn), lambda i, j: (0, j))],
        out_specs=pl.BlockSpec((tm, tn), lambda i, j: (i, j)),
        compiler_params=_params("parallel", "arbitrary"),
    )(a, wg, wu)


def _mla_in_kernel(a_ref, w_ref, qn_ref, kvn_ref, cos_ref, sin_ref, cq_ref, ckv_ref, kr_ref):
    def rms(t, gain):
        r = lax.rsqrt(jnp.mean(t * t, axis=-1, keepdims=True) + RMS_EPS)
        return t * r * gain

    a = a_ref[...]
    kv0 = Q_LORA + KV_LORA
    cq = jnp.dot(a, w_ref[:, :Q_LORA], preferred_element_type=F32)
    cq_ref[...] = rms(cq, qn_ref[...]).astype(BF16)
    ckv = jnp.dot(a, w_ref[:, Q_LORA:kv0], preferred_element_type=F32)
    ckv_ref[...] = rms(ckv, kvn_ref[...]).astype(BF16)
    y = jnp.dot(a, w_ref[:, kv0:], preferred_element_type=F32)
    kr_ref[...] = _rope_half_lanes(y, cos_ref[...], sin_ref[...]).astype(BF16)


def _mla_in(a, w, q_norm, kv_norm, tables, tm_pref=512):
    m, k = a.shape
    n = w.shape[1]
    tm = _tile(m, tm_pref)
    row = lambda width: pl.BlockSpec((tm, width), lambda i: (i, 0))
    return pl.pallas_call(
        _mla_in_kernel,
        out_shape=(jax.ShapeDtypeStruct((m, Q_LORA), BF16),
                   jax.ShapeDtypeStruct((m, KV_LORA), BF16),
                   jax.ShapeDtypeStruct((m, LANES), BF16)),
        grid=(m // tm,),
        in_specs=[row(k),
                  pl.BlockSpec((k, n), lambda i: (0, 0), pipeline_mode=pl.Buffered(1)),
                  pl.BlockSpec((1, Q_LORA), lambda i: (0, 0)),
                  pl.BlockSpec((1, KV_LORA), lambda i: (0, 0)),
                  row(LANES), row(LANES)],
        out_specs=(row(Q_LORA), row(KV_LORA), row(LANES)),
        compiler_params=_params("arbitrary"),
    )(a, w, q_norm.reshape(1, Q_LORA), kv_norm.reshape(1, KV_LORA), *tables)


QK_DIM = QK_NOPE + QK_ROPE
UP_HEADS_PER_DOT = 4


ATTN_BLOCK = 512
ONES_ROWS = 16


def _rope_tables_t(seq, dim):
    half = dim // 2
    inv = ROPE_THETA ** (-jnp.arange(half, dtype=F32) * 2.0 / dim)
    ang = inv[:, None] * jnp.arange(seq).astype(F32)[None, :]
    return jnp.cos(ang), jnp.sin(ang)


def _q_up_t_kernel(a_ref, wn_ref, wr_ref, cos_ref, sin_ref, q_ref, *, heads_per_step, scale):
    a = a_ref[...]
    cos, sin = cos_ref[...], sin_ref[...]
    hpd = UP_HEADS_PER_DOT
    half = QK_ROPE // 2
    tn = (((0,), (1,)), ((), ()))
    for c in range(heads_per_step // hpd):
        nope = lax.dot_general(wn_ref[:, c * hpd * QK_NOPE:(c + 1) * hpd * QK_NOPE], a, tn,
                               preferred_element_type=F32)
        rope = lax.dot_general(wr_ref[:, c * hpd * QK_ROPE:(c + 1) * hpd * QK_ROPE], a, tn,
                               preferred_element_type=F32)
        for h in range(hpd):
            y1 = rope[h * QK_ROPE:h * QK_ROPE + half, :]
            y2 = rope[h * QK_ROPE + half:(h + 1) * QK_ROPE, :]
            q_ref[c * hpd + h, :QK_NOPE, :] = (
                nope[h * QK_NOPE:(h + 1) * QK_NOPE, :] * scale).astype(BF16)
            q_ref[c * hpd + h, QK_NOPE:QK_NOPE + half, :] = (
                (y1 * cos - y2 * sin) * scale).astype(BF16)
            q_ref[c * hpd + h, QK_NOPE + half:, :] = ((y2 * cos + y1 * sin) * scale).astype(BF16)


def _q_up_t(cq, w_uq_g, tables_t, heads, scale, tm_pref=1024, hps_pref=8):
    m, k = cq.shape
    tm = _tile(m, tm_pref)
    hps = _tile(heads, hps_pref)
    assert hps % UP_HEADS_PER_DOT == 0
    half = QK_ROPE // 2
    rope_blk0 = heads * QK_NOPE // (hps * QK_ROPE)
    return pl.pallas_call(
        functools.partial(_q_up_t_kernel, heads_per_step=hps, scale=scale),
        out_shape=jax.ShapeDtypeStruct((heads, QK_DIM, m), BF16),
        grid=(m // tm, heads // hps),
        in_specs=[pl.BlockSpec((tm, k), lambda i, j: (i, 0)),
                  pl.BlockSpec((k, hps * QK_NOPE), lambda i, j: (0, j)),
                  pl.BlockSpec((k, hps * QK_ROPE), lambda i, j: (0, rope_blk0 + j)),
                  pl.BlockSpec((half, tm), lambda i, j: (0, i)),
                  pl.BlockSpec((half, tm), lambda i, j: (0, i))],
        out_specs=pl.BlockSpec((hps, QK_DIM, tm), lambda i, j: (j, 0, i)),
        compiler_params=_params("parallel", "arbitrary"),
    )(cq, w_uq_g, w_uq_g, *tables_t)


def _kv_up_t_kernel(a_ref, wk_ref, wv_ref, kr_ref, k_ref, v_ref, *, heads_per_step, blk):
    a = a_ref[...]
    kr = kr_ref[:, :QK_ROPE]
    hpd = UP_HEADS_PER_DOT
    tn = (((0,), (1,)), ((), ()))
    for c in range(heads_per_step // hpd):
        kn = jnp.dot(a, wk_ref[:, c * hpd * QK_NOPE:(c + 1) * hpd * QK_NOPE],
                     preferred_element_type=F32)
        vt = lax.dot_general(wv_ref[:, c * hpd * V_DIM:(c + 1) * hpd * V_DIM], a, tn,
                             preferred_element_type=F32)
        for h in range(hpd):
            k_ref[c * hpd + h, :, :QK_NOPE] = kn[:, h * QK_NOPE:(h + 1) * QK_NOPE].astype(BF16)
            k_ref[c * hpd + h, :, QK_NOPE:] = kr
            for b in range(a.shape[0] // blk):
                v_ref[c * hpd + h, b] = vt[h * V_DIM:(h + 1) * V_DIM,
                                           b * blk:(b + 1) * blk].astype(BF16)


def _kv_up_t(ckv, w_ukv_g, kr, heads, blk, tm_pref=1024, hps_pref=16):
    m, k = ckv.shape
    tm = _tile(m, tm_pref)
    hps = _tile(heads, hps_pref)
    assert hps % UP_HEADS_PER_DOT == 0 and tm % blk == 0
    v_blk0 = heads * QK_NOPE // (hps * V_DIM)
    return pl.pallas_call(
        functools.partial(_kv_up_t_kernel, heads_per_step=hps, blk=blk),
        out_shape=(jax.ShapeDtypeStruct((heads, m, QK_DIM), BF16),
                   jax.ShapeDtypeStruct((heads, m // blk, V_DIM, blk), BF16)),
        grid=(m // tm, heads // hps),
        in_specs=[pl.BlockSpec((tm, k), lambda i, j: (i, 0)),
                  pl.BlockSpec((k, hps * QK_NOPE), lambda i, j: (0, j)),
                  pl.BlockSpec((k, hps * V_DIM), lambda i, j: (0, v_blk0 + j)),
                  pl.BlockSpec((tm, LANES), lambda i, j: (i, 0))],
        out_specs=(pl.BlockSpec((hps, tm, QK_DIM), lambda i, j: (j, i, 0)),
                   pl.BlockSpec((hps, tm // blk, V_DIM, blk), lambda i, j: (j, i, 0, 0))),
        compiler_params=_params("parallel", "arbitrary"),
    )(ckv, w_ukv_g, w_ukv_g, kr)


def _mla_attn_t_kernel(q_ref, k_ref, v_ref, o_ref, m_ref, acc_ref, s0_ref, s1_ref, *, blk, nsub):
    qi = pl.program_id(1)
    m_ref[...] = jnp.full(m_ref.shape, NEG, F32)
    acc_ref[...] = jnp.zeros(acc_ref.shape, F32)
    ones = jnp.ones((ONES_ROWS, blk), BF16)
    slots = (s0_ref, s1_ref)

    def scores(sub, j, slot):
        start = pl.multiple_of(j * blk, blk)
        kk = k_ref[0, pl.ds(start, blk), :]
        qt = q_ref[0, :, sub * blk:(sub + 1) * blk]
        slot[sub] = jnp.dot(kk, qt, preferred_element_type=F32)

    def softmax_pv(sub, j, slot, masked):
        vt = jnp.concatenate([v_ref[0, j], ones], axis=0)
        s = slot[sub]
        if masked:
            key = lax.broadcasted_iota(jnp.int32, s.shape, 0)
            qry = lax.broadcasted_iota(jnp.int32, s.shape, 1)
            s = jnp.where(key <= qry, s, NEG)
        m_prev = m_ref[sub]
        m_new = jnp.maximum(m_prev, jnp.max(s, axis=0, keepdims=True))
        corr = jnp.exp2(m_prev - m_new)
        p = jnp.exp2(s - m_new).astype(BF16)
        acc_ref[sub] = corr * acc_ref[sub] + jnp.dot(vt, p, preferred_element_type=F32)
        m_ref[sub] = m_new

    for sub in range(nsub):
        scores(sub, 0, slots[0])

    def full_blocks(t):
        for c in range(nsub):
            j = t * nsub + c
            for sub in range(nsub):
                scores(sub, j + 1, slots[(c + 1) % 2])
                softmax_pv(sub, j, slots[c % 2], masked=False)

    def body(t, carry):
        full_blocks(t)
        return carry

    lax.fori_loop(0, qi, body, 0)
    base = qi * nsub
    for c in range(nsub):
        for sub in range(c, nsub):
            if sub > c:
                scores(sub, base + c + 1, slots[(c + 1) % 2])
            softmax_pv(sub, base + c, slots[c % 2], masked=(sub == c))
    for sub in range(nsub):
        acc = acc_ref[sub]
        o_t = acc[:V_DIM, :] / acc[V_DIM:V_DIM + 1, :]
        o_ref[sub * blk:(sub + 1) * blk, :] = o_t.T.astype(o_ref.dtype)


def _mla_attn_t(q_t, k, v_t, nsub_pref=4):
    heads, s, _ = k.shape
    blk = v_t.shape[3]
    nsub = _tile(s // blk, nsub_pref)
    assert nsub % 2 == 0, "score slots alternate with key-block parity"
    tq = blk * nsub
    return pl.pallas_call(
        functools.partial(_mla_attn_t_kernel, blk=blk, nsub=nsub),
        out_shape=jax.ShapeDtypeStruct((s, heads * V_DIM), BF16),
        grid=(heads, s // tq),
        in_specs=[pl.BlockSpec((1, QK_DIM, tq), lambda h, i: (h, 0, i)),
                  pl.BlockSpec((1, s, QK_DIM), lambda h, i: (h, 0, 0)),
                  pl.BlockSpec((1, s // blk, V_DIM, blk), lambda h, i: (h, 0, 0, 0))],
        out_specs=pl.BlockSpec((tq, V_DIM), lambda h, i: (i, h)),
        scratch_shapes=[pltpu.VMEM((nsub, 1, blk), F32),
                        pltpu.VMEM((nsub, V_DIM + ONES_ROWS, blk), F32),
                        pltpu.VMEM((nsub, blk, blk), F32),
                        pltpu.VMEM((nsub, blk, blk), F32)],
        compiler_params=_params("parallel", "arbitrary"),
    )(q_t, k, v_t)


def _ffn_block(x, xb, w_gate, w_up, w_down, l, g, b, alpha, want_bf16):
    hidden = w_gate.shape[2]
    hp = _round_up(hidden, 512)
    wg = _cast_pad(w_gate, l, cols_p=hp)
    wu = _cast_pad(w_up, l, cols_p=hp)
    wd = _cast_pad(w_down, l, rows_p=hp)
    hmid = _ffn_up(xb, wg, wu)
    return _mm_res_ln(hmid, wd, x, g, b, alpha, want_bf16=want_bf16)


def _attn_conv_layer(x, xb, w_in, conv_w, w_out, j, g, b, alpha):
    s, d = x.shape
    a_width = d // 2
    w_in_b = _cast_pad(w_in, j)
    qkv = _in_proj(xb, w_in_b, _rope_tables(s, HEAD_DIM), ncols=3 * a_width,
                   rope_cols=2 * a_width, scale=HEAD_DIM ** -0.5, scaled_cols=a_width)
    assert all(window // d == DIL_BLOCK and s % window == 0 for window, d in DILATED_PATTERNS)
    attn = _dilated_mix(qkv, a_width, [d for _, d in DILATED_PATTERNS])
    gated = _gate_conv(xb, w_in_b, conv_w[j], 3 * a_width, d - a_width)
    return _proj_res_ln([attn, gated], _cast_pad(w_out, j), x, g, b, alpha)


def _mla_layer(x, xb, w_in, q_norm, kv_norm, w_uq, w_ukv, w_out, j, g, b, alpha):
    s, d = x.shape
    heads = d // 128
    w_in_p = _cast_pad(w_in, j, cols_p=w_in.shape[2] + LANES - QK_ROPE)
    cq, ckv, kr = _mla_in(xb, w_in_p, q_norm[j], kv_norm[j], _rope_tables(s, QK_ROPE))
    w_uq_h = w_uq[j].astype(BF16).reshape(Q_LORA, heads, QK_DIM)
    w_uq_g = jnp.concatenate([w_uq_h[:, :, :QK_NOPE].reshape(Q_LORA, heads * QK_NOPE),
                              w_uq_h[:, :, QK_NOPE:].reshape(Q_LORA, heads * QK_ROPE)], axis=1)
    q_t = _q_up_t(cq, w_uq_g, _rope_tables_t(s, QK_ROPE), heads, QK_DIM ** -0.5 * math.log2(math.e))
    w_ukv_h = w_ukv[j].astype(BF16).reshape(KV_LORA, heads, QK_NOPE + V_DIM)
    w_ukv_g = jnp.concatenate([w_ukv_h[:, :, :QK_NOPE].reshape(KV_LORA, heads * QK_NOPE),
                               w_ukv_h[:, :, QK_NOPE:].reshape(KV_LORA, heads * V_DIM)], axis=1)
    k, v_t = _kv_up_t(ckv, w_ukv_g, kr, heads, _tile(s, ATTN_BLOCK))
    o = _mla_attn_t(q_t, k, v_t)
    return _proj_res_ln([o], _cast_pad(w_out, j), x, g, b, alpha)


def kernel(x, w_in_a, conv_w, w_out_a, w_in_c, q_norm, kv_norm, w_uq, w_ukv, w_out_c,
           ln1_g, ln1_b, w_gate, w_up, w_down, ln2_g, ln2_b):
    batch, s, d = x.shape
    depth = ln1_g.shape[0]
    alpha = (2.0 * depth) ** 0.25
    outs = []
    for bi in range(batch):
        xf = x[bi]
        xb = xf.astype(BF16)
        for l in range(depth):
            j = l // 2
            if l % 2 == 0:
                xf, xb = _attn_conv_layer(xf, xb, w_in_a, conv_w, w_out_a, j,
                                          ln1_g[l], ln1_b[l], alpha)
            else:
                xf, xb = _mla_layer(xf, xb, w_in_c, q_norm, kv_norm, w_uq, w_ukv, w_out_c, j,
                                    ln1_g[l], ln1_b[l], alpha)
            xf, xb = _ffn_block(xf, xb, w_gate, w_up, w_down, l, ln2_g[l], ln2_b[l],
                                alpha, want_bf16=l + 1 < depth)
        outs.append(xf)
    return jnp.stack(outs)
```

```python
import functools
import math

import jax
import jax.numpy as jnp
from jax import lax
from jax.experimental import pallas as pl
from jax.experimental.pallas import tpu as pltpu

F32 = jnp.float32
BF16 = jnp.bfloat16

HEAD_DIM = 128
CONV_WIDTH = 3
DILATED_PATTERNS = ((128, 1), (512, 4), (2048, 16))
DIL_BLOCK = 128
Q_LORA = 1536
KV_LORA = 512
QK_NOPE = 128
QK_ROPE = 64
V_DIM = 128
ROPE_THETA = 10000.0
LN_EPS = 1e-5
RMS_EPS = 1e-6
NEG = -1e30

LANES = 128
SUBLANES = 8
VMEM_LIMIT_BYTES = 60 * 1024 * 1024


def _round_up(n, m):
    return -(-n // m) * m


def _tile(n, pref):
    if n <= pref:
        return n
    t = pref
    while n % t:
        t //= 2
    return t


def _params(*sem):
    return pltpu.CompilerParams(dimension_semantics=sem, vmem_limit_bytes=VMEM_LIMIT_BYTES)


def _cast_pad_kernel(w_ref, o_ref, *, rows, cols, rows_p):
    tr, cols_p = o_ref.shape
    val = w_ref[...].astype(BF16)
    if rows_p > rows:
        row = pl.program_id(0) * tr + lax.broadcasted_iota(jnp.int32, val.shape, 0)
        val = jnp.where(row < rows, val, jnp.zeros_like(val))
    o_ref[:, :cols] = val
    if cols_p > cols:
        o_ref[:, cols:] = jnp.zeros((tr, cols_p - cols), BF16)


def _cast_pad(w_stack, layer, rows_p=None, cols_p=None, tr_pref=256):
    _, rows, cols = w_stack.shape
    rows_p = rows_p or rows
    cols_p = cols_p or cols
    tr = _tile(rows_p, tr_pref)
    last = (rows - 1) // tr
    return pl.pallas_call(
        functools.partial(_cast_pad_kernel, rows=rows, cols=cols, rows_p=rows_p),
        out_shape=jax.ShapeDtypeStruct((rows_p, cols_p), BF16),
        grid=(rows_p // tr,),
        in_specs=[pl.BlockSpec((None, tr, cols), lambda i: (layer, jnp.minimum(i, last), 0))],
        out_specs=pl.BlockSpec((tr, cols_p), lambda i: (i, 0)),
        compiler_params=_params("parallel"),
    )(w_stack)


def _rope_tables(seq, dim):
    half = dim // 2
    inv = ROPE_THETA ** (-jnp.arange(half, dtype=F32) * 2.0 / dim)
    ang = jnp.arange(seq).astype(F32)[:, None] * inv[None, :]
    cos, sin = jnp.cos(ang), jnp.sin(ang)
    reps = LANES // dim
    return (jnp.concatenate([cos, cos] * reps, axis=1),
            jnp.concatenate([-sin, sin] * reps, axis=1))


def _rope_full_lanes(t, cos, sin):
    return t * cos + pltpu.roll(t, HEAD_DIM // 2, axis=1) * sin


def _rope_half_lanes(y, cos, sin):
    half = QK_ROPE // 2
    lane = lax.broadcasted_iota(jnp.int32, y.shape, 1)
    partner = jnp.where(lane % QK_ROPE < half, pltpu.roll(y, LANES - half, axis=1),
                        pltpu.roll(y, half, axis=1))
    return y * cos + partner * sin


def _in_proj_kernel(a_ref, w_ref, cos_ref, sin_ref, o_ref, *, scale, n_scaled_tiles,
                    n_rope_tiles):
    j = pl.program_id(1)
    acc = jnp.dot(a_ref[...], w_ref[...], preferred_element_type=F32)

    @pl.when(j >= n_rope_tiles)
    def _():
        o_ref[...] = acc

    @pl.when(j < n_rope_tiles)
    def _():
        cos = cos_ref[...]
        sin = sin_ref[...]
        mult = jnp.where(j < n_scaled_tiles, scale, 1.0).astype(F32)
        for c in range(acc.shape[1] // HEAD_DIM):
            sl = slice(c * HEAD_DIM, (c + 1) * HEAD_DIM)
            o_ref[:, sl] = _rope_full_lanes(acc[:, sl], cos, sin) * mult


def _in_proj(a, w, rope_tables, *, ncols, rope_cols, scale, scaled_cols, tm_pref=1024,
             tn_pref=1024):
    m, k = a.shape
    n = ncols
    tm = _tile(m, tm_pref)
    tn = _tile(math.gcd(n, rope_cols, scaled_cols), tn_pref)
    return pl.pallas_call(
        functools.partial(_in_proj_kernel, scale=scale, n_scaled_tiles=scaled_cols // tn,
                          n_rope_tiles=rope_cols // tn),
        out_shape=jax.ShapeDtypeStruct((m, n), F32),
        grid=(m // tm, n // tn),
        in_specs=[pl.BlockSpec((tm, k), lambda i, j: (i, 0)),
                  pl.BlockSpec((k, tn), lambda i, j: (0, j)),
                  pl.BlockSpec((tm, LANES), lambda i, j: (i, 0)),
                  pl.BlockSpec((tm, LANES), lambda i, j: (i, 0))],
        out_specs=pl.BlockSpec((tm, tn), lambda i, j: (i, j)),
        compiler_params=_params("parallel", "arbitrary"),
    )(a, w, *rope_tables)


DIL_ROWS = 2048
DIL_STAGE_STRIDE = 4


MIX_ROW_CHUNK = 256


def _dilated_mix_kernel(q_ref, kp_ref, kc_ref, vp_ref, vc_ref, a_ref, *scratch, dilations):
    n = pl.program_id(0)
    rows = q_ref.shape[0]
    blk = DIL_BLOCK
    nb = len(dilations)
    o_nat, l_nat, stage_bufs = scratch[:nb], scratch[nb:2 * nb], scratch[2 * nb:]
    qi = lax.broadcasted_iota(jnp.int32, (blk, 2 * blk), 0)
    ki = lax.broadcasted_iota(jnp.int32, (blk, 2 * blk), 1)
    dist = qi + blk - ki
    band = (dist >= 0) & (dist <= blk)
    band_first = band & ((ki >= blk) | (n > 0))
    ones = jnp.ones((2 * blk, HEAD_DIM), BF16)
    inner = DIL_STAGE_STRIDE

    for bi, d in enumerate(dilations):
        span = blk * d
        groups = rows // span
        staged = d % (4 * inner) == 0
        if staged:
            sq, skp, skc, svp, svc, so, sl = stage_bufs
            for src, dst in ((q_ref, sq), (kp_ref, skp), (kc_ref, skc), (vp_ref, svp),
                             (vc_ref, svc)):
                for rr in range(inner):
                    dst[rr] = src[pl.ds(rr, rows // inner, stride=inner), :]
        else:
            sq, skp, skc, svp, svc = q_ref, kp_ref, kc_ref, vp_ref, vc_ref
            so, sl = o_nat[bi], l_nat[bi]

        def rows_of(ref, start, r, d=d, staged=staged):
            if not staged:
                return ref, (pl.ds(start + r, blk, stride=d) if d > 1 else pl.ds(start + r, blk))
            return ref.at[r % inner], pl.ds(start // inner + r // inner, blk, stride=d // inner)

        def read(ref, start, r, rows_of=rows_of):
            view, idx = rows_of(ref, start, r)
            return view[idx, :]

        for g in range(groups):
            for r in range(d):
                q = read(sq, g * span, r).astype(BF16)
                if g == 0:
                    tail = (groups - 1) * span
                    k_prev, v_prev = read(skp, tail, r), read(svp, tail, r)
                else:
                    k_prev, v_prev = read(skc, (g - 1) * span, r), read(svc, (g - 1) * span, r)
                kk = jnp.concatenate([k_prev, read(skc, g * span, r)], axis=0).astype(BF16)
                vv = jnp.concatenate([v_prev, read(svc, g * span, r)], axis=0).astype(BF16)
                s = lax.dot_general(q, kk, (((1,), (1,)), ((), ())),
                                    preferred_element_type=F32)
                s = jnp.where(band_first if g == 0 else band, s, NEG)
                m = jnp.max(s, axis=-1, keepdims=True)
                p = jnp.exp2(s - m).astype(BF16)
                pv = jnp.dot(p, jnp.concatenate([vv, ones], axis=1), preferred_element_type=F32)
                den = pv[:, HEAD_DIM:]
                o_view, idx = rows_of(so, g * span, r)
                o_view[idx, :] = pv[:, :HEAD_DIM] / den
                l_view, idx = rows_of(sl, g * span, r)
                l_view[idx, :] = m + jnp.log2(den)

        if staged:
            for rr in range(inner):
                dst_rows = pl.ds(rr, rows // inner, stride=inner)
                o_nat[bi][dst_rows, :] = so[rr]
                l_nat[bi][dst_rows, :] = sl[rr]

    ch = _tile(rows, MIX_ROW_CHUNK)
    for c in range(rows // ch):
        rs = slice(c * ch, (c + 1) * ch)
        lses = [l[rs, :] for l in l_nat]
        top = functools.reduce(jnp.maximum, lses)
        es = [jnp.exp2(l - top) for l in lses]
        num = functools.reduce(lambda x, y: x + y, [e * o[rs, :] for e, o in zip(es, o_nat)])
        den = functools.reduce(lambda x, y: x + y, es)
        a_ref[rs, :] = (num / den).astype(a_ref.dtype)


def _dilated_mix(hproj, a_width, dilations):
    s = hproj.shape[0]
    heads = a_width // HEAD_DIM
    rows = _tile(s, DIL_ROWS)
    assert all(rows % (DIL_BLOCK * d) == 0 for d in dilations)
    cur = lambda col0: pl.BlockSpec((rows, HEAD_DIM), lambda n, h: (n, col0 + h))
    prev = lambda col0: pl.BlockSpec((rows, HEAD_DIM),
                                     lambda n, h: (jnp.maximum(n - 1, 0), col0 + h))
    natural = pltpu.VMEM((rows, HEAD_DIM), F32)
    staged = pltpu.VMEM((DIL_STAGE_STRIDE, rows // DIL_STAGE_STRIDE, HEAD_DIM), F32)
    return pl.pallas_call(
        functools.partial(_dilated_mix_kernel, dilations=tuple(dilations)),
        out_shape=jax.ShapeDtypeStruct((s, a_width), BF16),
        grid=(s // rows, heads),
        in_specs=[cur(0), prev(heads), cur(heads), prev(2 * heads), cur(2 * heads)],
        out_specs=pl.BlockSpec((rows, HEAD_DIM), lambda n, h: (n, h)),
        scratch_shapes=[natural] * (2 * len(dilations)) + [staged] * 7,
        compiler_params=_params("parallel", "arbitrary"),
    )(hproj, hproj, hproj, hproj, hproj)


GATE_CHANNELS = 256


def _gate_conv_kernel(a_ref, wb_ref, wc_ref, wh_ref, cw_ref, out_ref, tail_ref):
    i = pl.program_id(0)
    j = pl.program_id(1)
    a = a_ref[...]
    gb = jnp.dot(a, wb_ref[...], preferred_element_type=F32)
    gc = jnp.dot(a, wc_ref[...], preferred_element_type=F32)
    hin = jnp.dot(a, wh_ref[...], preferred_element_type=F32)
    u = gc * hin
    tm = u.shape[0]

    @pl.when(i == 0)
    def _():
        tail_ref[j] = jnp.zeros(tail_ref.shape[1:], F32)

    halo = tail_ref[j]
    tail_ref[j] = u[tm - SUBLANES:, :]
    row = lax.broadcasted_iota(jnp.int32, u.shape, 0)
    hm1 = halo[SUBLANES - 1:SUBLANES, :]
    hm2 = halo[SUBLANES - 2:SUBLANES - 1, :]
    u1 = jnp.where(row == 0, hm1, pltpu.roll(u, 1, axis=0))
    u2 = jnp.where(row == 0, hm2, jnp.where(row == 1, hm1, pltpu.roll(u, 2, axis=0)))
    cw = cw_ref[...]
    y = cw[0:1, :] * u2 + cw[1:2, :] * u1 + cw[2:3, :] * u
    out_ref[...] = (gb * y).astype(out_ref.dtype)


def _gate_conv(a, w, conv_w, col0, b_width, tm_pref=1024):
    m, k = a.shape
    tm = _tile(m, tm_pref)
    cb = _tile(b_width, GATE_CHANNELS)
    assert col0 % cb == 0 and tm >= SUBLANES
    nct = b_width // cb
    wblk = lambda g: pl.BlockSpec((k, cb), lambda i, j: (0, (col0 + g * b_width) // cb + j))
    return pl.pallas_call(
        _gate_conv_kernel,
        out_shape=jax.ShapeDtypeStruct((m, b_width), BF16),
        grid=(m // tm, nct),
        in_specs=[pl.BlockSpec((tm, k), lambda i, j: (i, 0)), wblk(0), wblk(1), wblk(2),
                  pl.BlockSpec((CONV_WIDTH, cb), lambda i, j: (0, j))],
        out_specs=pl.BlockSpec((tm, cb), lambda i, j: (i, j)),
        scratch_shapes=[pltpu.VMEM((nct, SUBLANES, cb), F32)],
        compiler_params=_params("arbitrary", "arbitrary"),
    )(a, w, w, w, conv_w)


LN_COL_CHUNK = 1024
LN_ROW_CHUNK = 8


def _residual_ln(x_ref, g_ref, b_ref, o_ref, ob_ref, alpha):
    tm = o_ref.shape[0]
    rc = _tile(tm, LN_ROW_CHUNK)
    for r in range(tm // rc):
        rows = slice(r * rc, (r + 1) * rc)
        z = alpha * x_ref[rows, :] + o_ref[rows, :]
        mu = jnp.mean(z, axis=-1, keepdims=True)
        zc = z - mu
        var = jnp.mean(zc * zc, axis=-1, keepdims=True)
        y = zc * lax.rsqrt(var + LN_EPS) * g_ref[...] + b_ref[...]
        o_ref[rows, :] = y
        if ob_ref is not None:
            ob_ref[rows, :] = y.astype(BF16)


def _mm_res_ln_kernel(a_ref, w_ref, x_ref, g_ref, b_ref, o_ref, *maybe_ob, alpha):
    k = pl.program_id(1)
    n = o_ref.shape[1]

    @pl.when(k == 0)
    def _():
        o_ref[...] = jnp.zeros(o_ref.shape, F32)

    a = a_ref[...]
    nc = _tile(n, LN_COL_CHUNK)
    for c in range(n // nc):
        sl = slice(c * nc, (c + 1) * nc)
        o_ref[:, sl] += jnp.dot(a, w_ref[:, sl], preferred_element_type=F32)

    @pl.when(k == pl.num_programs(1) - 1)
    def _():
        _residual_ln(x_ref, g_ref, b_ref, o_ref, maybe_ob[0] if maybe_ob else None, alpha)


def _mm_res_ln(a, w, xres, g, b, alpha, *, want_bf16, tm_pref=512, tk_pref=512):
    m, k = a.shape
    n = w.shape[1]
    tm = _tile(m, tm_pref)
    tk = _tile(k, tk_pref)
    out_shape = [jax.ShapeDtypeStruct((m, n), F32)]
    out_specs = [pl.BlockSpec((tm, n), lambda i, kk: (i, 0))]
    if want_bf16:
        out_shape.append(jax.ShapeDtypeStruct((m, n), BF16))
        out_specs.append(pl.BlockSpec((tm, n), lambda i, kk: (i, 0)))
    res = pl.pallas_call(
        functools.partial(_mm_res_ln_kernel, alpha=alpha),
        out_shape=tuple(out_shape),
        grid=(m // tm, k // tk),
        in_specs=[
            pl.BlockSpec((tm, tk), lambda i, kk: (i, kk)),
            pl.BlockSpec((tk, n), lambda i, kk: (kk, 0)),
            pl.BlockSpec((tm, n), lambda i, kk: (i, 0)),
            pl.BlockSpec((1, n), lambda i, kk: (0, 0)),
            pl.BlockSpec((1, n), lambda i, kk: (0, 0)),
        ],
        out_specs=tuple(out_specs),
        compiler_params=_params("parallel", "arbitrary"),
    )(a, w, xres, g.reshape(1, n), b.reshape(1, n))
    return res if want_bf16 else (res[0], None)


def _proj_res_ln_kernel(*refs, n_parts, alpha):
    a_refs = refs[:n_parts]
    w_ref, x_ref, g_ref, b_ref, o_ref, ob_ref = refs[n_parts:]
    n = o_ref.shape[1]
    nc = _tile(n, LN_COL_CHUNK)
    for c in range(n // nc):
        sl = slice(c * nc, (c + 1) * nc)
        k0 = 0
        acc = None
        for a_ref in a_refs:
            kw = a_ref.shape[1]
            part = jnp.dot(a_ref[...], w_ref[k0:k0 + kw, sl], preferred_element_type=F32)
            acc = part if acc is None else acc + part
            k0 += kw
        o_ref[:, sl] = acc
    _residual_ln(x_ref, g_ref, b_ref, o_ref, ob_ref, alpha)


def _proj_res_ln(parts, w, xres, g, b, alpha, tm_pref=128):
    m = parts[0].shape[0]
    k, n = w.shape
    assert sum(p.shape[1] for p in parts) == k
    tm = _tile(m, tm_pref)
    row = lambda width: pl.BlockSpec((tm, width), lambda i: (i, 0))
    vec = pl.BlockSpec((1, n), lambda i: (0, 0))
    return pl.pallas_call(
        functools.partial(_proj_res_ln_kernel, n_parts=len(parts), alpha=alpha),
        out_shape=(jax.ShapeDtypeStruct((m, n), F32), jax.ShapeDtypeStruct((m, n), BF16)),
        grid=(m // tm,),
        in_specs=[row(p.shape[1]) for p in parts]
        + [pl.BlockSpec((k, n), lambda i: (0, 0), pipeline_mode=pl.Buffered(1)),
           row(n), vec, vec],
        out_specs=(row(n), row(n)),
        compiler_params=_params("arbitrary"),
    )(*parts, w, xres, g.reshape(1, n), b.reshape(1, n))


def _ffn_up_kernel(a_ref, wg_ref, wu_ref, o_ref):
    a = a_ref[...]
    g = jnp.dot(a, wg_ref[...], preferred_element_type=F32)
    u = jnp.dot(a, wu_ref[...], preferred_element_type=F32)
    o_ref[...] = (g * jax.nn.sigmoid(g) * u).astype(o_ref.dtype)


def _ffn_up(a, wg, wu, tm_pref=1024, tn_pref=512):
    m, k = a.shape
    n = wg.shape[1]
    tm = _tile(m, tm_pref)
    tn = _tile(n, tn_pref)
    return pl.pallas_call(
        _ffn_up_kernel,
        out_shape=jax.ShapeDtypeStruct((m, n), BF16),
        grid=(m // tm, n // tn),
        in_specs=[pl.BlockSpec((tm, k), lambda i, j: (i, 0)),
                  pl.BlockSpec((k, tn), lambda i, j: (0, j)),
                  pl.BlockSpec((k, tn), lambda i, j: (0, j))],
        out_specs=pl.BlockSpec((tm, tn), lambda i, j: (i, j)),
        compiler_params=_params("parallel", "arbitrary"),
    )(a, wg, wu)


def _mla_in_kernel(a_ref, w_ref, qn_ref, kvn_ref, cos_ref, sin_ref, cq_ref, ckv_ref, kr_ref):
    def rms(t, gain):
        r = lax.rsqrt(jnp.mean(t * t, axis=-1, keepdims=True) + RMS_EPS)
        return t * r * gain

    a = a_ref[...]
    kv0 = Q_LORA + KV_LORA
    cq = jnp.dot(a, w_ref[:, :Q_LORA], preferred_element_type=F32)
    cq_ref[...] = rms(cq, qn_ref[...]).astype(BF16)
    ckv = jnp.dot(a, w_ref[:, Q_LORA:kv0], preferred_element_type=F32)
    ckv_ref[...] = rms(ckv, kvn_ref[...]).astype(BF16)
    y = jnp.dot(a, w_ref[:, kv0:], preferred_element_type=F32)
    kr_ref[...] = _rope_half_lanes(y, cos_ref[...], sin_ref[...]).astype(BF16)


def _mla_in(a, w, q_norm, kv_norm, tables, tm_pref=512):
    m, k = a.shape
    n = w.shape[1]
    tm = _tile(m, tm_pref)
    row = lambda width: pl.BlockSpec((tm, width), lambda i: (i, 0))
    return pl.pallas_call(
        _mla_in_kernel,
        out_shape=(jax.ShapeDtypeStruct((m, Q_LORA), BF16),
                   jax.ShapeDtypeStruct((m, KV_LORA), BF16),
                   jax.ShapeDtypeStruct((m, LANES), BF16)),
        grid=(m // tm,),
        in_specs=[row(k),
                  pl.BlockSpec((k, n), lambda i: (0, 0), pipeline_mode=pl.Buffered(1)),
                  pl.BlockSpec((1, Q_LORA), lambda i: (0, 0)),
                  pl.BlockSpec((1, KV_LORA), lambda i: (0, 0)),
                  row(LANES), row(LANES)],
        out_specs=(row(Q_LORA), row(KV_LORA), row(LANES)),
        compiler_params=_params("arbitrary"),
    )(a, w, q_norm.reshape(1, Q_LORA), kv_norm.reshape(1, KV_LORA), *tables)


QK_DIM = QK_NOPE + QK_ROPE
UP_HEADS_PER_DOT = 4


ATTN_BLOCK = 512
ONES_ROWS = 16


def _rope_tables_t(seq, dim):
    half = dim // 2
    inv = ROPE_THETA ** (-jnp.arange(half, dtype=F32) * 2.0 / dim)
    ang = inv[:, None] * jnp.arange(seq).astype(F32)[None, :]
    return jnp.cos(ang), jnp.sin(ang)


def _q_up_t_kernel(a_ref, wn_ref, wr_ref, cos_ref, sin_ref, q_ref, *, heads_per_step, scale):
    a = a_ref[...]
    cos, sin = cos_ref[...], sin_ref[...]
    hpd = UP_HEADS_PER_DOT
    half = QK_ROPE // 2
    tn = (((0,), (1,)), ((), ()))
    for c in range(heads_per_step // hpd):
        nope = lax.dot_general(wn_ref[:, c * hpd * QK_NOPE:(c + 1) * hpd * QK_NOPE], a, tn,
                               preferred_element_type=F32)
        rope = lax.dot_general(wr_ref[:, c * hpd * QK_ROPE:(c + 1) * hpd * QK_ROPE], a, tn,
                               preferred_element_type=F32)
        for h in range(hpd):
            y1 = rope[h * QK_ROPE:h * QK_ROPE + half, :]
            y2 = rope[h * QK_ROPE + half:(h + 1) * QK_ROPE, :]
            q_ref[c * hpd + h, :QK_NOPE, :] = (
                nope[h * QK_NOPE:(h + 1) * QK_NOPE, :] * scale).astype(BF16)
            q_ref[c * hpd + h, QK_NOPE:QK_NOPE + half, :] = (
                (y1 * cos - y2 * sin) * scale).astype(BF16)
            q_ref[c * hpd + h, QK_NOPE + half:, :] = ((y2 * cos + y1 * sin) * scale).astype(BF16)


def _q_up_t(cq, w_uq_g, tables_t, heads, scale, tm_pref=1024, hps_pref=8):
    m, k = cq.shape
    tm = _tile(m, tm_pref)
    hps = _tile(heads, hps_pref)
    assert hps % UP_HEADS_PER_DOT == 0
    half = QK_ROPE // 2
    rope_blk0 = heads * QK_NOPE // (hps * QK_ROPE)
    return pl.pallas_call(
        functools.partial(_q_up_t_kernel, heads_per_step=hps, scale=scale),
        out_shape=jax.ShapeDtypeStruct((heads, QK_DIM, m), BF16),
        grid=(m // tm, heads // hps),
        in_specs=[pl.BlockSpec((tm, k), lambda i, j: (i, 0)),
                  pl.BlockSpec((k, hps * QK_NOPE), lambda i, j: (0, j)),
                  pl.BlockSpec((k, hps * QK_ROPE), lambda i, j: (0, rope_blk0 + j)),
                  pl.BlockSpec((half, tm), lambda i, j: (0, i)),
                  pl.BlockSpec((half, tm), lambda i, j: (0, i))],
        out_specs=pl.BlockSpec((hps, QK_DIM, tm), lambda i, j: (j, 0, i)),
        compiler_params=_params("parallel", "arbitrary"),
    )(cq, w_uq_g, w_uq_g, *tables_t)


def _kv_up_t_kernel(a_ref, wk_ref, wv_ref, kr_ref, k_ref, v_ref, *, heads_per_step, blk):
    a = a_ref[...]
    kr = kr_ref[:, :QK_ROPE]
    hpd = UP_HEADS_PER_DOT
    tn = (((0,), (1,)), ((), ()))
    for c in range(heads_per_step // hpd):
        kn = jnp.dot(a, wk_ref[:, c * hpd * QK_NOPE:(c + 1) * hpd * QK_NOPE],
                     preferred_element_type=F32)
        vt = lax.dot_general(wv_ref[:, c * hpd * V_DIM:(c + 1) * hpd * V_DIM], a, tn,
                             preferred_element_type=F32)
        for h in range(hpd):
            k_ref[c * hpd + h, :, :QK_NOPE] = kn[:, h * QK_NOPE:(h + 1) * QK_NOPE].astype(BF16)
            k_ref[c * hpd + h, :, QK_NOPE:] = kr
            for b in range(a.shape[0] // blk):
                v_ref[c * hpd + h, b] = vt[h * V_DIM:(h + 1) * V_DIM,
                                           b * blk:(b + 1) * blk].astype(BF16)


def _kv_up_t(ckv, w_ukv_g, kr, heads, blk, tm_pref=1024, hps_pref=16):
    m, k = ckv.shape
    tm = _tile(m, tm_pref)
    hps = _tile(heads, hps_pref)
    assert hps % UP_HEADS_PER_DOT == 0 and tm % blk == 0
    v_blk0 = heads * QK_NOPE // (hps * V_DIM)
    return pl.pallas_call(
        functools.partial(_kv_up_t_kernel, heads_per_step=hps, blk=blk),
        out_shape=(jax.ShapeDtypeStruct((heads, m, QK_DIM), BF16),
                   jax.ShapeDtypeStruct((heads, m // blk, V_DIM, blk), BF16)),
        grid=(m // tm, heads // hps),
        in_specs=[pl.BlockSpec((tm, k), lambda i, j: (i, 0)),
                  pl.BlockSpec((k, hps * QK_NOPE), lambda i, j: (0, j)),
                  pl.BlockSpec((k, hps * V_DIM), lambda i, j: (0, v_blk0 + j)),
                  pl.BlockSpec((tm, LANES), lambda i, j: (i, 0))],
        out_specs=(pl.BlockSpec((hps, tm, QK_DIM), lambda i, j: (j, i, 0)),
                   pl.BlockSpec((hps, tm // blk, V_DIM, blk), lambda i, j: (j, i, 0, 0))),
        compiler_params=_params("parallel", "arbitrary"),
    )(ckv, w_ukv_g, w_ukv_g, kr)


def _mla_attn_t_kernel(q_ref, k_ref, v_ref, o_ref, m_ref, acc_ref, s0_ref, s1_ref, *, blk, nsub):
    qi = pl.program_id(1)
    m_ref[...] = jnp.full(m_ref.shape, NEG, F32)
    acc_ref[...] = jnp.zeros(acc_ref.shape, F32)
    ones = jnp.ones((ONES_ROWS, blk), BF16)
    slots = (s0_ref, s1_ref)

    def scores(sub, j, slot):
        start = pl.multiple_of(j * blk, blk)
        kk = k_ref[0, pl.ds(start, blk), :]
        qt = q_ref[0, :, sub * blk:(sub + 1) * blk]
        slot[sub] = jnp.dot(kk, qt, preferred_element_type=F32)

    def softmax_pv(sub, j, slot, masked):
        vt = jnp.concatenate([v_ref[0, j], ones], axis=0)
        s = slot[sub]
        if masked:
            key = lax.broadcasted_iota(jnp.int32, s.shape, 0)
            qry = lax.broadcasted_iota(jnp.int32, s.shape, 1)
            s = jnp.where(key <= qry, s, NEG)
        m_prev = m_ref[sub]
        m_new = jnp.maximum(m_prev, jnp.max(s, axis=0, keepdims=True))
        corr = jnp.exp2(m_prev - m_new)
        p = jnp.exp2(s - m_new).astype(BF16)
        acc_ref[sub] = corr * acc_ref[sub] + jnp.dot(vt, p, preferred_element_type=F32)
        m_ref[sub] = m_new

    for sub in range(nsub):
        scores(sub, 0, slots[0])

    def full_blocks(t):
        for c in range(nsub):
            j = t * nsub + c
            for sub in range(nsub):
                scores(sub, j + 1, slots[(c + 1) % 2])
                softmax_pv(sub, j, slots[c % 2], masked=False)

    def body(t, carry):
        full_blocks(t)
        return carry

    lax.fori_loop(0, qi, body, 0)
    base = qi * nsub
    for c in range(nsub):
        for sub in range(c, nsub):
            if sub > c:
                scores(sub, base + c + 1, slots[(c + 1) % 2])
            softmax_pv(sub, base + c, slots[c % 2], masked=(sub == c))
    for sub in range(nsub):
        acc = acc_ref[sub]
        o_t = acc[:V_DIM, :] / acc[V_DIM:V_DIM + 1, :]
        o_ref[sub * blk:(sub + 1) * blk, :] = o_t.T.astype(o_ref.dtype)


def _mla_attn_t(q_t, k, v_t, nsub_pref=4):
    heads, s, _ = k.shape
    blk = v_t.shape[3]
    nsub = _tile(s // blk, nsub_pref)
    assert nsub % 2 == 0, "score slots alternate with key-block parity"
    tq = blk * nsub
    return pl.pallas_call(
        functools.partial(_mla_attn_t_kernel, blk=blk, nsub=nsub),
        out_shape=jax.ShapeDtypeStruct((s, heads * V_DIM), BF16),
        grid=(heads, s // tq),
        in_specs=[pl.BlockSpec((1, QK_DIM, tq), lambda h, i: (h, 0, i)),
                  pl.BlockSpec((1, s, QK_DIM), lambda h, i: (h, 0, 0)),
                  pl.BlockSpec((1, s // blk, V_DIM, blk), lambda h, i: (h, 0, 0, 0))],
        out_specs=pl.BlockSpec((tq, V_DIM), lambda h, i: (i, h)),
        scratch_shapes=[pltpu.VMEM((nsub, 1, blk), F32),
                        pltpu.VMEM((nsub, V_DIM + ONES_ROWS, blk), F32),
                        pltpu.VMEM((nsub, blk, blk), F32),
                        pltpu.VMEM((nsub, blk, blk), F32)],
        compiler_params=_params("parallel", "arbitrary"),
    )(q_t, k, v_t)


def _ffn_block(x, xb, w_gate, w_up, w_down, l, g, b, alpha, want_bf16):
    hidden = w_gate.shape[2]
    hp = _round_up(hidden, 512)
    wg = _cast_pad(w_gate, l, cols_p=hp)
    wu = _cast_pad(w_up, l, cols_p=hp)
    wd = _cast_pad(w_down, l, rows_p=hp)
    hmid = _ffn_up(xb, wg, wu)
    return _mm_res_ln(hmid, wd, x, g, b, alpha, want_bf16=want_bf16)


def _attn_conv_layer(x, xb, w_in, conv_w, w_out, j, g, b, alpha):
    s, d = x.shape
    a_width = d // 2
    w_in_b = _cast_pad(w_in, j)
    qkv = _in_proj(xb, w_in_b, _rope_tables(s, HEAD_DIM), ncols=3 * a_width,
                   rope_cols=2 * a_width, scale=HEAD_DIM ** -0.5 * math.log2(math.e),
                   scaled_cols=a_width)
    assert all(window // d == DIL_BLOCK and s % window == 0 for window, d in DILATED_PATTERNS)
    attn = _dilated_mix(qkv, a_width, [d for _, d in DILATED_PATTERNS])
    gated = _gate_conv(xb, w_in_b, conv_w[j], 3 * a_width, d - a_width)
    return _proj_res_ln([attn, gated], _cast_pad(w_out, j), x, g, b, alpha)


def _mla_layer(x, xb, w_in, q_norm, kv_norm, w_uq, w_ukv, w_out, j, g, b, alpha):
    s, d = x.shape
    heads = d // 128
    w_in_p = _cast_pad(w_in, j, cols_p=w_in.shape[2] + LANES - QK_ROPE)
    cq, ckv, kr = _mla_in(xb, w_in_p, q_norm[j], kv_norm[j], _rope_tables(s, QK_ROPE))
    w_uq_h = w_uq[j].astype(BF16).reshape(Q_LORA, heads, QK_DIM)
    w_uq_g = jnp.concatenate([w_uq_h[:, :, :QK_NOPE].reshape(Q_LORA, heads * QK_NOPE),
                              w_uq_h[:, :, QK_NOPE:].reshape(Q_LORA, heads * QK_ROPE)], axis=1)
    q_t = _q_up_t(cq, w_uq_g, _rope_tables_t(s, QK_ROPE), heads, QK_DIM ** -0.5 * math.log2(math.e))
    w_ukv_h = w_ukv[j].astype(BF16).reshape(KV_LORA, heads, QK_NOPE + V_DIM)
    w_ukv_g = jnp.concatenate([w_ukv_h[:, :, :QK_NOPE].reshape(KV_LORA, heads * QK_NOPE),
                               w_ukv_h[:, :, QK_NOPE:].reshape(KV_LORA, heads * V_DIM)], axis=1)
    k, v_t = _kv_up_t(ckv, w_ukv_g, kr, heads, _tile(s, ATTN_BLOCK))
    o = _mla_attn_t(q_t, k, v_t)
    return _proj_res_ln([o], _cast_pad(w_out, j), x, g, b, alpha)


def kernel(x, w_in_a, conv_w, w_out_a, w_in_c, q_norm, kv_norm, w_uq, w_ukv, w_out_c,
           ln1_g, ln1_b, w_gate, w_up, w_down, ln2_g, ln2_b):
    batch, s, d = x.shape
    depth = ln1_g.shape[0]
    alpha = (2.0 * depth) ** 0.25
    outs = []
    for bi in range(batch):
        xf = x[bi]
        xb = xf.astype(BF16)
        for l in range(depth):
            j = l // 2
            if l % 2 == 0:
                xf, xb = _attn_conv_layer(xf, xb, w_in_a, conv_w, w_out_a, j,
                                          ln1_g[l], ln1_b[l], alpha)
            else:
                xf, xb = _mla_layer(xf, xb, w_in_c, q_norm, kv_norm, w_uq, w_ukv, w_out_c, j,
                                    ln1_g[l], ln1_b[l], alpha)
            xf, xb = _ffn_block(xf, xb, w_gate, w_up, w_down, l, ln2_g[l], ln2_b[l],
                                alpha, want_bf16=l + 1 < depth)
        outs.append(xf)
    return jnp.stack(outs)
```
